```python
import jax, jax.numpy as jnp
from jax import lax
import numpy as np

D_MODEL = 2048
BATCH = 4
SEQ = 4096
DEPTH = 1

HEAD_DIM = 128
ATTN_HEADS = 8
MLSTM_HEADS = 8
ATTN_WIDTH = ATTN_HEADS * HEAD_DIM
MLSTM_WIDTH = MLSTM_HEADS * HEAD_DIM
MIX_WIDTH = ATTN_WIDTH + MLSTM_WIDTH
ATTN_PATTERNS = ((128, 1), (512, 4), (2048, 16))
ATTN_BLOCK = 128
MLSTM_CHUNK = 128
CONV_WIDTH = 4
FFN_HIDDEN = ((8 * D_MODEL + 3 * 256 - 1) // (3 * 256)) * 256
IN_WIDTH = 3 * ATTN_WIDTH + 4 * MLSTM_WIDTH + 2 * MLSTM_HEADS
EPS = 1e-6
MASK_VALUE = -1e30
M_INIT = -1e30

kernel_name = "hybrid_dilated_attn_mlstm_block"


def rmsnorm(x, g):
    x32 = x.astype(jnp.float32)
    y = x32 * lax.rsqrt(jnp.mean(x32 * x32, axis=-1, keepdims=True) + EPS)
    return (y * g.astype(jnp.float32)).astype(x.dtype)


def causal_depthwise_conv(u, w, b):
    c = u.shape[-1]
    y = lax.conv_general_dilated(u, w[:, None, :].astype(u.dtype), window_strides=(1,),
                                 padding=[(CONV_WIDTH - 1, 0)],
                                 dimension_numbers=("NWC", "WIO", "NWC"),
                                 feature_group_count=c)
    return y + b.astype(u.dtype)


def strided_window_attention(q, k, v, span, dil):
    assert span <= ATTN_BLOCK
    B, S, H, Dh = q.shape
    n = S // dil
    nb = -(-n // ATTN_BLOCK)
    n_pad = nb * ATTN_BLOCK

    def to_blocks(t):
        t = t.reshape(B, n, dil, H, Dh)
        t = jnp.pad(t, ((0, 0), (0, n_pad - n), (0, 0), (0, 0), (0, 0)))
        return t.reshape(B, nb, ATTN_BLOCK, dil, H, Dh)

    def with_prev(t):
        prev = jnp.pad(t, ((0, 0), (1, 0), (0, 0), (0, 0), (0, 0), (0, 0)))[:, :-1]
        return jnp.concatenate([prev, t], axis=2)

    qb = to_blocks(q)
    kw = with_prev(to_blocks(k))
    vw = with_prev(to_blocks(v))
    scores = jnp.einsum("bnqrhd,bnkrhd->bnrhqk", qb, kw).astype(jnp.float32)
    q_idx = jnp.arange(ATTN_BLOCK)[:, None]
    k_idx = jnp.arange(2 * ATTN_BLOCK)[None, :] - ATTN_BLOCK
    dist = q_idx - k_idx
    blk = jnp.arange(nb)[:, None, None]
    valid = (dist >= 0) & (dist <= span) & (blk * ATTN_BLOCK + k_idx >= 0)
    scores = jnp.where(valid[None, :, None, None], scores, MASK_VALUE)
    m = jnp.max(scores, axis=-1, keepdims=True)
    p = jnp.exp(scores - m)
    s = jnp.sum(p, axis=-1, keepdims=True)
    o = jnp.einsum("bnrhqk,bnkrhd->bnqrhd", p / s, vw.astype(jnp.float32))
    lse = (m + jnp.log(s))[..., 0]
    o = o.reshape(B, n_pad, dil, H, Dh)[:, :n].reshape(B, S, H, Dh)
    lse = lse.transpose(0, 1, 4, 2, 3).reshape(B, n_pad, dil, H)[:, :n].reshape(B, S, H)
    return o, lse


def dilated_window_attention(q, k, v):
    outs, lses = [], []
    for window, dil in ATTN_PATTERNS:
        o, lse = strided_window_attention(q, k, v, window // dil, dil)
        outs.append(o)
        lses.append(lse)
    alpha = jax.nn.softmax(jnp.stack(lses, axis=0), axis=0)
    out = jnp.sum(alpha[..., None] * jnp.stack(outs, axis=0), axis=0)
    return out.astype(q.dtype)


def mlstm_chunkwise(q, k, v, i_pre, f_pre):
    B, S, H, Dh = q.shape
    L = MLSTM_CHUNK
    nc = S // L

    def chunks(t):
        return t.astype(jnp.float32).transpose(0, 2, 1, 3).reshape(B, H, nc, L, Dh)

    def gchunks(t):
        return t.astype(jnp.float32).transpose(0, 2, 1).reshape(B, H, nc, L)

    qc, kc, vc = chunks(q), chunks(k) * (Dh ** -0.5), chunks(v)
    ic = gchunks(i_pre)
    b = jnp.cumsum(jax.nn.log_sigmoid(gchunks(f_pre)), axis=-1)

    b_last = b[..., -1]
    a = b_last[..., None] - b + ic
    m_chunk = jnp.max(a, axis=-1)
    wa = jnp.exp(a - m_chunk[..., None])
    kv_chunk = jnp.einsum("bhcl,bhcld,bhcle->bhcde", wa, kc, vc)
    n_chunk = jnp.einsum("bhcl,bhcld->bhcd", wa, kc)

    def step(carry, xs):
        C, nvec, m = carry
        bl, mc, kvc, nch = xs
        m_new = jnp.maximum(bl + m, mc)
        decay = jnp.exp(bl + m - m_new)
        scale = jnp.exp(mc - m_new)
        C_new = decay[..., None, None] * C + scale[..., None, None] * kvc
        n_new = decay[..., None] * nvec + scale[..., None] * nch
        return (C_new, n_new, m_new), (C, nvec, m)

    init = (jnp.zeros((B, H, Dh, Dh), jnp.float32), jnp.zeros((B, H, Dh), jnp.float32),
            jnp.full((B, H), M_INIT, jnp.float32))
    xs = (jnp.moveaxis(b_last, 2, 0), jnp.moveaxis(m_chunk, 2, 0),
          jnp.moveaxis(kv_chunk, 2, 0), jnp.moveaxis(n_chunk, 2, 0))
    _, (C_prev, n_prev, m_prev) = lax.scan(step, init, xs)
    C_prev = jnp.moveaxis(C_prev, 0, 2)
    n_prev = jnp.moveaxis(n_prev, 0, 2)
    m_prev = jnp.moveaxis(m_prev, 0, 2)

    causal = jnp.tril(jnp.ones((L, L), dtype=bool))
    log_d = jnp.where(causal, b[..., :, None] - b[..., None, :] + ic[..., None, :], MASK_VALUE)
    inter = b + m_prev[..., None]
    m_t = jnp.maximum(inter, jnp.max(log_d, axis=-1))
    w = jnp.exp(log_d - m_t[..., None]) * jnp.einsum("bhcld,bhcsd->bhcls", qc, kc)
    g = jnp.exp(inter - m_t)
    num = g[..., None] * jnp.einsum("bhcld,bhcde->bhcle", qc, C_prev) + jnp.einsum("bhcls,bhcse->bhcle", w, vc)
    den = g * jnp.einsum("bhcld,bhcd->bhcl", qc, n_prev) + jnp.sum(w, axis=-1)
    h = num / jnp.maximum(jnp.abs(den), jnp.exp(-m_t))[..., None]
    return h.reshape(B, H, S, Dh).transpose(0, 2, 1, 3).astype(q.dtype)


def setup_inputs(seed: int = 0) -> dict:
    key = jax.random.key(seed)
    ks = jax.random.split(key, 16)

    def normal(k, shape, scale):
        return jax.random.normal(k, shape, jnp.float32) * scale

    x = normal(ks[0], (BATCH, SEQ, D_MODEL), 1.0)
    norm_mix_g = 1.0 + normal(ks[1], (DEPTH, D_MODEL), 0.02)
    w_in = normal(ks[2], (DEPTH, D_MODEL, IN_WIDTH), D_MODEL ** -0.5)
    conv_w = normal(ks[3], (DEPTH, CONV_WIDTH, 2 * MLSTM_WIDTH), CONV_WIDTH ** -0.5)
    conv_b = normal(ks[4], (DEPTH, 2 * MLSTM_WIDTH), 0.02)
    i_bias = normal(ks[5], (DEPTH, MLSTM_HEADS), 0.1)
    f_bias = jnp.linspace(3.0, 6.0, MLSTM_HEADS, dtype=jnp.float32)[None] + normal(ks[6], (DEPTH, MLSTM_HEADS), 0.1)
    gate_b = jnp.concatenate([i_bias, f_bias], axis=-1)
    q_norm_g = 1.0 + normal(ks[7], (DEPTH, HEAD_DIM), 0.02)
    k_norm_g = 1.0 + normal(ks[8], (DEPTH, HEAD_DIM), 0.02)
    mlstm_norm_g = 1.0 + normal(ks[9], (DEPTH, MLSTM_HEADS, HEAD_DIM), 0.02)
    w_out = normal(ks[10], (DEPTH, MIX_WIDTH, D_MODEL), MIX_WIDTH ** -0.5)
    norm_ffn_g = 1.0 + normal(ks[11], (DEPTH, D_MODEL), 0.02)
    w_gate = normal(ks[12], (DEPTH, D_MODEL, FFN_HIDDEN), D_MODEL ** -0.5)
    w_up = normal(ks[13], (DEPTH, D_MODEL, FFN_HIDDEN), D_MODEL ** -0.5)
    w_down = normal(ks[14], (DEPTH, FFN_HIDDEN, D_MODEL), FFN_HIDDEN ** -0.5)
    return {"x": x, "norm_mix_g": norm_mix_g, "w_in": w_in, "conv_w": conv_w, "conv_b": conv_b,
            "gate_b": gate_b, "q_norm_g": q_norm_g, "k_norm_g": k_norm_g, "mlstm_norm_g": mlstm_norm_g,
            "w_out": w_out, "norm_ffn_g": norm_ffn_g, "w_gate": w_gate, "w_up": w_up, "w_down": w_down}


def reference(x, norm_mix_g, w_in, conv_w, conv_b, gate_b, q_norm_g, k_norm_g, mlstm_norm_g,
              w_out, norm_ffn_g, w_gate, w_up, w_down):
    B, S, _ = x.shape
    offsets = [ATTN_WIDTH * i for i in (1, 2, 3)] + [3 * ATTN_WIDTH + MLSTM_WIDTH * i for i in (1, 2, 3, 4)]

    def heads(t, n_heads):
        return t.reshape(B, S, n_heads, HEAD_DIM)

    for layer in range(DEPTH):
        h = rmsnorm(x, norm_mix_g[layer])
        proj = h @ w_in[layer]
        aq, ak, av, mq, mk, mv, mo, gates = jnp.split(proj, offsets, axis=-1)

        aq = rmsnorm(heads(aq, ATTN_HEADS), q_norm_g[layer]) * (HEAD_DIM ** -0.5)
        ak = rmsnorm(heads(ak, ATTN_HEADS), k_norm_g[layer])
        attn_out = dilated_window_attention(aq, ak, heads(av, ATTN_HEADS)).reshape(B, S, ATTN_WIDTH)

        mqk = jax.nn.silu(causal_depthwise_conv(jnp.concatenate([mq, mk], axis=-1), conv_w[layer], conv_b[layer]))
        mq, mk = jnp.split(mqk, 2, axis=-1)
        gates = gates + gate_b[layer]
        cell = mlstm_chunkwise(heads(mq, MLSTM_HEADS), heads(mk, MLSTM_HEADS), heads(mv, MLSTM_HEADS),
                               gates[..., :MLSTM_HEADS], gates[..., MLSTM_HEADS:])
        mlstm_out = jax.nn.sigmoid(mo) * rmsnorm(cell, mlstm_norm_g[layer]).reshape(B, S, MLSTM_WIDTH)

        x = x + jnp.concatenate([attn_out, mlstm_out], axis=-1) @ w_out[layer]

        h = rmsnorm(x, norm_ffn_g[layer])
        x = x + (jax.nn.silu(h @ w_gate[layer]) * (h @ w_up[layer])) @ w_down[layer]
    return x
```

```python
import functools

import jax
import jax.numpy as jnp
from jax import lax
from jax.experimental import pallas as pl
from jax.experimental.pallas import tpu as pltpu

D_MODEL = 2048
HEAD_DIM = 128
ATTN_HEADS = 8
MLSTM_HEADS = 8
ATTN_WIDTH = ATTN_HEADS * HEAD_DIM
MLSTM_WIDTH = MLSTM_HEADS * HEAD_DIM
ATTN_PATTERNS = ((128, 1), (512, 4), (2048, 16))
ATTN_BLOCK = 128
MLSTM_CHUNK = 128
CONV_WIDTH = 4
FFN_HIDDEN = 5632
EPS = 1e-6
MASK_VALUE = -1e30
M_INIT = -1e30

LANES = 128
VMEM_LIMIT_BYTES = 56 * 1024 * 1024

F32 = jnp.float32
BF16 = jnp.bfloat16


def _in_proj_body(x_ref, g_ref, w_ref, wg_ref, bg_ref, hg_ref,
                  oatt_ref, omf_ref, omv_ref, ogate_ref, h_scr):
    j = pl.program_id(1)

    @pl.when(j == 0)
    def _():
        x = x_ref[...]
        ms = jnp.mean(x * x, axis=-1, keepdims=True)
        hb = (x * lax.rsqrt(ms + EPS) * g_ref[...]).astype(BF16)
        h_scr[...] = hb
        ogate_ref[...] = jnp.dot(hb, wg_ref[...], preferred_element_type=F32) + bg_ref[...]

    acc = jnp.dot(h_scr[...], w_ref[...], preferred_element_type=F32)

    @pl.when(j < 2)
    def _():
        for hd in range(ATTN_HEADS):
            sl = slice(hd * HEAD_DIM, (hd + 1) * HEAD_DIM)
            a = acc[:, sl]
            ms = jnp.mean(a * a, axis=-1, keepdims=True)
            oatt_ref[:, sl] = (a * lax.rsqrt(ms + EPS) * hg_ref[:, sl]).astype(BF16)

    @pl.when(j == 2)
    def _():
        oatt_ref[...] = acc.astype(BF16)

    @pl.when(jnp.logical_and(j >= 3, j < 6))
    def _():
        omf_ref[...] = acc

    @pl.when(j == 6)
    def _():
        omv_ref[...] = acc.astype(BF16)


def _in_proj(x2, g_mix, w_main, w_gate, b_gate, head_gain, tm):
    t = x2.shape[0]
    tn = ATTN_WIDTH
    n_col = w_main.shape[1] // tn
    return pl.pallas_call(
        _in_proj_body,
        grid=(t // tm, n_col),
        in_specs=[
            pl.BlockSpec((tm, D_MODEL), lambda i, j: (i, 0)),
            pl.BlockSpec((1, D_MODEL), lambda i, j: (0, 0)),
            pl.BlockSpec((D_MODEL, tn), lambda i, j: (0, j)),
            pl.BlockSpec((D_MODEL, LANES), lambda i, j: (0, 0)),
            pl.BlockSpec((1, LANES), lambda i, j: (0, 0)),
            pl.BlockSpec((None, 1, tn), lambda i, j: (jnp.minimum(j, 1), 0, 0)),
        ],
        out_specs=[
            pl.BlockSpec((tm, tn), lambda i, j: (i, jnp.minimum(j, 2))),
            pl.BlockSpec((tm, tn), lambda i, j: (i, jnp.clip(j - 3, 0, 2))),
            pl.BlockSpec((tm, tn), lambda i, j: (i, 0)),
            pl.BlockSpec((tm, LANES), lambda i, j: (i, 0)),
        ],
        out_shape=[
            jax.ShapeDtypeStruct((t, 3 * tn), BF16),
            jax.ShapeDtypeStruct((t, 3 * tn), F32),
            jax.ShapeDtypeStruct((t, tn), BF16),
            jax.ShapeDtypeStruct((t, LANES), F32),
        ],
        scratch_shapes=[pltpu.VMEM((tm, D_MODEL), BF16)],
        compiler_params=pltpu.CompilerParams(
            dimension_semantics=("arbitrary", "arbitrary"),
            vmem_limit_bytes=VMEM_LIMIT_BYTES),
        name="in_proj",
    )(x2, g_mix, w_main, w_gate, b_gate, head_gain)


def _out_proj_body(a_ref, m_ref, x_ref, wa_ref, wm_ref, g_ref, x1_ref, h2_ref):
    y = jnp.dot(a_ref[...], wa_ref[...], preferred_element_type=F32)
    y = y + jnp.dot(m_ref[...], wm_ref[...], preferred_element_type=F32)
    y = y + x_ref[...]
    x1_ref[...] = y
    ms = jnp.mean(y * y, axis=-1, keepdims=True)
    h2_ref[...] = (y * lax.rsqrt(ms + EPS) * g_ref[...]).astype(BF16)


def _out_proj(attn, mlstm, x2, w_a, w_m, g_ffn, tm):
    t = x2.shape[0]
    return pl.pallas_call(
        _out_proj_body,
        grid=(t // tm,),
        in_specs=[
            pl.BlockSpec((tm, ATTN_WIDTH), lambda i: (i, 0)),
            pl.BlockSpec((tm, MLSTM_WIDTH), lambda i: (i, 0)),
            pl.BlockSpec((tm, D_MODEL), lambda i: (i, 0)),
            pl.BlockSpec((ATTN_WIDTH, D_MODEL), lambda i: (0, 0)),
            pl.BlockSpec((MLSTM_WIDTH, D_MODEL), lambda i: (0, 0)),
            pl.BlockSpec((1, D_MODEL), lambda i: (0, 0)),
        ],
        out_specs=[
            pl.BlockSpec((tm, D_MODEL), lambda i: (i, 0)),
            pl.BlockSpec((tm, D_MODEL), lambda i: (i, 0)),
        ],
        out_shape=[
            jax.ShapeDtypeStruct((t, D_MODEL), F32),
            jax.ShapeDtypeStruct((t, D_MODEL), BF16),
        ],
        compiler_params=pltpu.CompilerParams(
            dimension_semantics=("arbitrary",),
            vmem_limit_bytes=VMEM_LIMIT_BYTES),
        name="out_proj",
    )(attn, mlstm, x2, w_a, w_m, g_ffn)


def _ffn_body(h_ref, wg_ref, wu_ref, wd_ref, x1_ref, o_ref):
    c = pl.program_id(1)
    h = h_ref[...]
    g = jnp.dot(h, wg_ref[...], preferred_element_type=F32)
    u = jnp.dot(h, wu_ref[...], preferred_element_type=F32)
    a = (g * jax.nn.sigmoid(g) * u).astype(BF16)
    d = jnp.dot(a, wd_ref[...], preferred_element_type=F32)

    @pl.when(c == 0)
    def _():
        o_ref[...] = x1_ref[...] + d

    @pl.when(c > 0)
    def _():
        o_ref[...] += d


def _ffn(h2, w_g, w_u, w_d, x1, tm, th):
    t = h2.shape[0]
    return pl.pallas_call(
        _ffn_body,
        grid=(t // tm, FFN_HIDDEN // th),
        in_specs=[
            pl.BlockSpec((tm, D_MODEL), lambda i, c: (i, 0)),
            pl.BlockSpec((D_MODEL, th), lambda i, c: (0, c)),
            pl.BlockSpec((D_MODEL, th), lambda i, c: (0, c)),
            pl.BlockSpec((th, D_MODEL), lambda i, c: (c, 0)),
            pl.BlockSpec((tm, D_MODEL), lambda i, c: (i, 0)),
        ],
        out_specs=pl.BlockSpec((tm, D_MODEL), lambda i, c: (i, 0)),
        out_shape=jax.ShapeDtypeStruct((t, D_MODEL), F32),
        compiler_params=pltpu.CompilerParams(
            dimension_semantics=("arbitrary", "arbitrary"),
            vmem_limit_bytes=VMEM_LIMIT_BYTES),
        name="ffn",
    )(h2, w_g, w_u, w_d, x1)


def _strided_window_attention(q, k, v, span, dil):
    B, S, H, Dh = q.shape
    n = S // dil
    nb = -(-n // ATTN_BLOCK)
    n_pad = nb * ATTN_BLOCK

    def to_blocks(t):
        t = t.reshape(B, n, dil, H, Dh)
        t = jnp.pad(t, ((0, 0), (0, n_pad - n), (0, 0), (0, 0), (0, 0)))
        return t.reshape(B, nb, ATTN_BLOCK, dil, H, Dh)

    def with_prev(t):
        prev = jnp.pad(t, ((0, 0), (1, 0), (0, 0), (0, 0), (0, 0), (0, 0)))[:, :-1]
        return jnp.concatenate([prev, t], axis=2)

    qb = to_blocks(q)
    kw = with_prev(to_blocks(k))
    vw = with_prev(to_blocks(v))
    scores = jnp.einsum("bnqrhd,bnkrhd->bnrhqk", qb, kw).astype(jnp.float32)
    q_idx = jnp.arange(ATTN_BLOCK)[:, None]
    k_idx = jnp.arange(2 * ATTN_BLOCK)[None, :] - ATTN_BLOCK
    dist = q_idx - k_idx
    blk = jnp.arange(nb)[:, None, None]
    valid = (dist >= 0) & (dist <= span) & (blk * ATTN_BLOCK + k_idx >= 0)
    scores = jnp.where(valid[None, :, None, None], scores, MASK_VALUE)
    m = jnp.max(scores, axis=-1, keepdims=True)
    p = jnp.exp(scores - m)
    s = jnp.sum(p, axis=-1, keepdims=True)
    o = jnp.einsum("bnrhqk,bnkrhd->bnqrhd", p / s, vw.astype(jnp.float32))
    lse = (m + jnp.log(s))[..., 0]
    o = o.reshape(B, n_pad, dil, H, Dh)[:, :n].reshape(B, S, H, Dh)
    lse = lse.transpose(0, 1, 4, 2, 3).reshape(B, n_pad, dil, H)[:, :n].reshape(B, S, H)
    return o, lse


def _dilated_window_attention(q, k, v):
    outs, lses = [], []
    for window, dil in ATTN_PATTERNS:
        o, lse = _strided_window_attention(q, k, v, window // dil, dil)
        outs.append(o)
        lses.append(lse)
    alpha = jax.nn.softmax(jnp.stack(lses, axis=0), axis=0)
    return jnp.sum(alpha[..., None] * jnp.stack(outs, axis=0), axis=0)


def _causal_depthwise_conv(u, w, b):
    c = u.shape[-1]
    y = lax.conv_general_dilated(u, w[:, None, :].astype(u.dtype), window_strides=(1,),
                                 padding=[(CONV_WIDTH - 1, 0)],
                                 dimension_numbers=("NWC", "WIO", "NWC"),
                                 feature_group_count=c)
    return y + b.astype(u.dtype)


def _mlstm_chunkwise(q, k, v, i_pre, f_pre):
    B, S, H, Dh = q.shape
    L = MLSTM_CHUNK
    nc = S // L

    def chunks(t):
        return t.astype(jnp.float32).transpose(0, 2, 1, 3).reshape(B, H, nc, L, Dh)

    def gchunks(t):
        return t.astype(jnp.float32).transpose(0, 2, 1).reshape(B, H, nc, L)

    qc, kc, vc = chunks(q), chunks(k) * (Dh ** -0.5), chunks(v)
    ic = gchunks(i_pre)
    b = jnp.cumsum(jax.nn.log_sigmoid(gchunks(f_pre)), axis=-1)
    b_last = b[..., -1]
    a = b_last[..., None] - b + ic
    m_chunk = jnp.max(a, axis=-1)
    wa = jnp.exp(a - m_chunk[..., None])
    kv_chunk = jnp.einsum("bhcl,bhcld,bhcle->bhcde", wa, kc, vc)
    n_chunk = jnp.einsum("bhcl,bhcld->bhcd", wa, kc)

    def step(carry, xs):
        C, nvec, m = carry
        bl, mc, kvc, nch = xs
        m_new = jnp.maximum(bl + m, mc)
        decay = jnp.exp(bl + m - m_new)
        scale = jnp.exp(mc - m_new)
        C_new = decay[..., None, None] * C + scale[..., None, None] * kvc
        n_new = decay[..., None] * nvec + scale[..., None] * nch
        return (C_new, n_new, m_new), (C, nvec, m)

    init = (jnp.zeros((B, H, Dh, Dh), jnp.float32), jnp.zeros((B, H, Dh), jnp.float32),
            jnp.full((B, H), M_INIT, jnp.float32))
    xs = (jnp.moveaxis(b_last, 2, 0), jnp.moveaxis(m_chunk, 2, 0),
          jnp.moveaxis(kv_chunk, 2, 0), jnp.moveaxis(n_chunk, 2, 0))
    _, (C_prev, n_prev, m_prev) = lax.scan(step, init, xs)
    C_prev = jnp.moveaxis(C_prev, 0, 2)
    n_prev = jnp.moveaxis(n_prev, 0, 2)
    m_prev = jnp.moveaxis(m_prev, 0, 2)
    causal = jnp.tril(jnp.ones((L, L), dtype=bool))
    log_d = jnp.where(causal, b[..., :, None] - b[..., None, :] + ic[..., None, :], MASK_VALUE)
    inter = b + m_prev[..., None]
    m_t = jnp.maximum(inter, jnp.max(log_d, axis=-1))
    w = jnp.exp(log_d - m_t[..., None]) * jnp.einsum("bhcld,bhcsd->bhcls", qc, kc)
    g = jnp.exp(inter - m_t)
    num = g[..., None] * jnp.einsum("bhcld,bhcde->bhcle", qc, C_prev) + jnp.einsum("bhcls,bhcse->bhcle", w, vc)
    den = g * jnp.einsum("bhcld,bhcd->bhcl", qc, n_prev) + jnp.sum(w, axis=-1)
    h = num / jnp.maximum(jnp.abs(den), jnp.exp(-m_t))[..., None]
    return h.reshape(B, H, S, Dh).transpose(0, 2, 1, 3)


def _rmsnorm(x, g):
    y = x * lax.rsqrt(jnp.mean(x * x, axis=-1, keepdims=True) + EPS)
    return y * g


def kernel(x, norm_mix_g, w_in, conv_w, conv_b, gate_b, q_norm_g, k_norm_g, mlstm_norm_g,
           w_out, norm_ffn_g, w_gate, w_up, w_down):
    B, S, _ = x.shape
    t = B * S
    layer = 0
    x2 = x.reshape(t, D_MODEL)

    w = w_in[layer]
    a3 = 3 * ATTN_WIDTH
    mw = MLSTM_WIDTH
    w_main = jnp.concatenate(
        [w[:, :a3 + 2 * mw], w[:, a3 + 3 * mw:a3 + 4 * mw], w[:, a3 + 2 * mw:a3 + 3 * mw]],
        axis=1).astype(BF16)
    n_gate = 2 * MLSTM_HEADS
    w_g16 = jnp.pad(w[:, a3 + 4 * mw:], ((0, 0), (0, LANES - n_gate))).astype(BF16)
    b_g16 = jnp.pad(gate_b[layer], (0, LANES - n_gate)).reshape(1, LANES)
    head_gain = jnp.stack([
        jnp.tile(q_norm_g[layer] * (HEAD_DIM ** -0.5), ATTN_HEADS),
        jnp.tile(k_norm_g[layer], ATTN_HEADS)]).reshape(2, 1, ATTN_WIDTH)

    att_qkv, m_f32, m_v, gates = _in_proj(
        x2, norm_mix_g[layer].reshape(1, D_MODEL), w_main, w_g16, b_g16, head_gain, tm=1024)

    def heads(tn, n_heads):
        return tn.reshape(B, S, n_heads, HEAD_DIM)

    aq = att_qkv[:, :ATTN_WIDTH].astype(F32)
    ak = att_qkv[:, ATTN_WIDTH:2 * ATTN_WIDTH].astype(F32)
    av = att_qkv[:, 2 * ATTN_WIDTH:].astype(F32)
    attn_out = _dilated_window_attention(heads(aq, 8), heads(ak, 8), heads(av, 8)).reshape(t, ATTN_WIDTH)

    mqk = m_f32[:, :2 * mw].reshape(B, S, 2 * mw)
    mqk = jax.nn.silu(_causal_depthwise_conv(mqk, conv_w[layer], conv_b[layer]))
    mq, mk = jnp.split(mqk, 2, axis=-1)
    mo = m_f32[:, 2 * mw:].reshape(B, S, mw)
    g3 = gates.reshape(B, S, LANES)
    cell = _mlstm_chunkwise(heads(mq, 8), heads(mk, 8), heads(m_v.astype(F32), 8),
                            g3[..., :MLSTM_HEADS], g3[..., MLSTM_HEADS:n_gate])
    mlstm_out = (jax.nn.sigmoid(mo) * _rmsnorm(cell, mlstm_norm_g[layer]).reshape(B, S, mw)).reshape(t, mw)

    wo = w_out[layer].astype(BF16)
    x1, h2 = _out_proj(attn_out.astype(BF16), mlstm_out.astype(BF16), x2,
                       wo[:ATTN_WIDTH], wo[ATTN_WIDTH:], norm_ffn_g[layer].reshape(1, D_MODEL), tm=512)

    out = _ffn(h2, w_gate[layer].astype(BF16), w_up[layer].astype(BF16), w_down[layer].astype(BF16),
               x1, tm=512, th=512)
    return out.reshape(B, S, D_MODEL)
```

```python
import functools

import jax
import jax.numpy as jnp
from jax import lax
from jax.experimental import pallas as pl
from jax.experimental.pallas import tpu as pltpu

D_MODEL = 2048
HEAD_DIM = 128
ATTN_HEADS = 8
MLSTM_HEADS = 8
ATTN_WIDTH = ATTN_HEADS * HEAD_DIM
MLSTM_WIDTH = MLSTM_HEADS * HEAD_DIM
ATTN_PATTERNS = ((128, 1), (512, 4), (2048, 16))
ATTN_BLOCK = 128
MLSTM_CHUNK = 128
CONV_WIDTH = 4
FFN_HIDDEN = 5632
EPS = 1e-6
MASK_VALUE = -1e30
M_INIT = -1e30

N_GATE = 2 * MLSTM_HEADS
LANES = 128
VMEM_LIMIT_BYTES = 56 * 1024 * 1024

F32 = jnp.float32
BF16 = jnp.bfloat16


def _in_proj_body(x_ref, g_ref, w_ref, wg_ref, bg_ref, hg_ref,
                  oatt_ref, omf_ref, omv_ref, ogate_ref, h_scr):
    j = pl.program_id(1)

    @pl.when(j == 0)
    def _():
        x = x_ref[...]
        ms = jnp.mean(x * x, axis=-1, keepdims=True)
        hb = (x * lax.rsqrt(ms + EPS) * g_ref[...]).astype(BF16)
        h_scr[...] = hb
        ogate_ref[...] = lax.dot_general(wg_ref[...], hb, (((1,), (1,)), ((), ())),
                                         preferred_element_type=F32) + bg_ref[...]

    acc = jnp.dot(h_scr[...], w_ref[...], preferred_element_type=F32)

    @pl.when(j < 2)
    def _():
        for hd in range(ATTN_HEADS):
            sl = slice(hd * HEAD_DIM, (hd + 1) * HEAD_DIM)
            a = acc[:, sl]
            ms = jnp.mean(a * a, axis=-1, keepdims=True)
            oatt_ref[hd] = (a * lax.rsqrt(ms + EPS) * hg_ref[:, sl]).astype(BF16)

    @pl.when(j == 2)
    def _():
        for hd in range(ATTN_HEADS):
            oatt_ref[hd] = acc[:, hd * HEAD_DIM:(hd + 1) * HEAD_DIM].astype(BF16)

    @pl.when(jnp.logical_and(j >= 3, j < 6))
    def _():
        omf_ref[...] = acc

    @pl.when(j == 6)
    def _():
        omv_ref[...] = acc.astype(BF16)


def _in_proj(x2, g_mix, w_main, w_gate_t, b_gate, head_gain, tm):
    t = x2.shape[0]
    tn = ATTN_WIDTH
    n_col = w_main.shape[1] // tn
    return pl.pallas_call(
        _in_proj_body,
        grid=(t // tm, n_col),
        in_specs=[
            pl.BlockSpec((tm, D_MODEL), lambda i, j: (i, 0)),
            pl.BlockSpec((1, D_MODEL), lambda i, j: (0, 0)),
            pl.BlockSpec((D_MODEL, tn), lambda i, j: (0, j)),
            pl.BlockSpec((N_GATE, D_MODEL), lambda i, j: (0, 0)),
            pl.BlockSpec((N_GATE, 1), lambda i, j: (0, 0)),
            pl.BlockSpec((None, 1, tn), lambda i, j: (jnp.minimum(j, 1), 0, 0)),
        ],
        out_specs=[
            pl.BlockSpec((None, ATTN_HEADS, tm, HEAD_DIM), lambda i, j: (jnp.minimum(j, 2), 0, i, 0)),
            pl.BlockSpec((tm, tn), lambda i, j: (i, jnp.clip(j - 3, 0, 2))),
            pl.BlockSpec((tm, tn), lambda i, j: (i, 0)),
            pl.BlockSpec((N_GATE, tm), lambda i, j: (0, i)),
        ],
        out_shape=[
            jax.ShapeDtypeStruct((3, ATTN_HEADS, t, HEAD_DIM), BF16),
            jax.ShapeDtypeStruct((t, 3 * tn), F32),
            jax.ShapeDtypeStruct((t, tn), BF16),
            jax.ShapeDtypeStruct((N_GATE, t), F32),
        ],
        scratch_shapes=[pltpu.VMEM((tm, D_MODEL), BF16)],
        compiler_params=pltpu.CompilerParams(
            dimension_semantics=("arbitrary", "arbitrary"),
            vmem_limit_bytes=VMEM_LIMIT_BYTES),
        name="in_proj",
    )(x2, g_mix, w_main, w_gate_t, b_gate, head_gain)


def _out_proj_body(a_ref, m_ref, x_ref, wa_ref, wm_ref, g_ref, x1_ref, h2_ref):
    a = jnp.concatenate([a_ref[hd] for hd in range(ATTN_HEADS)], axis=1)
    y = jnp.dot(a, wa_ref[...], preferred_element_type=F32)
    y = y + jnp.dot(m_ref[...], wm_ref[...], preferred_element_type=F32)
    y = y + x_ref[...]
    x1_ref[...] = y
    ms = jnp.mean(y * y, axis=-1, keepdims=True)
    h2_ref[...] = (y * lax.rsqrt(ms + EPS) * g_ref[...]).astype(BF16)


def _out_proj(attn, mlstm, x2, w_a, w_m, g_ffn, tm):
    t = x2.shape[0]
    return pl.pallas_call(
        _out_proj_body,
        grid=(t // tm,),
        in_specs=[
            pl.BlockSpec((ATTN_HEADS, tm, HEAD_DIM), lambda i: (0, i, 0)),
            pl.BlockSpec((tm, MLSTM_WIDTH), lambda i: (i, 0)),
            pl.BlockSpec((tm, D_MODEL), lambda i: (i, 0)),
            pl.BlockSpec((ATTN_WIDTH, D_MODEL), lambda i: (0, 0)),
            pl.BlockSpec((MLSTM_WIDTH, D_MODEL), lambda i: (0, 0)),
            pl.BlockSpec((1, D_MODEL), lambda i: (0, 0)),
        ],
        out_specs=[
            pl.BlockSpec((tm, D_MODEL), lambda i: (i, 0)),
            pl.BlockSpec((tm, D_MODEL), lambda i: (i, 0)),
        ],
        out_shape=[
            jax.ShapeDtypeStruct((t, D_MODEL), F32),
            jax.ShapeDtypeStruct((t, D_MODEL), BF16),
        ],
        compiler_params=pltpu.CompilerParams(
            dimension_semantics=("arbitrary",),
            vmem_limit_bytes=VMEM_LIMIT_BYTES),
        name="out_proj",
    )(attn, mlstm, x2, w_a, w_m, g_ffn)


def _ffn_body(h_ref, wg_ref, wu_ref, wd_ref, x1_ref, o_ref):
    c = pl.program_id(1)
    h = h_ref[...]
    g = jnp.dot(h, wg_ref[...], preferred_element_type=F32)
    u = jnp.dot(h, wu_ref[...], preferred_element_type=F32)
    a = (g * jax.nn.sigmoid(g) * u).astype(BF16)
    d = jnp.dot(a, wd_ref[...], preferred_element_type=F32)

    @pl.when(c == 0)
    def _():
        o_ref[...] = x1_ref[...] + d

    @pl.when(c > 0)
    def _():
        o_ref[...] += d


def _ffn(h2, w_g, w_u, w_d, x1, tm, th):
    t = h2.shape[0]
    return pl.pallas_call(
        _ffn_body,
        grid=(t // tm, FFN_HIDDEN // th),
        in_specs=[
            pl.BlockSpec((tm, D_MODEL), lambda i, c: (i, 0)),
            pl.BlockSpec((D_MODEL, th), lambda i, c: (0, c)),
            pl.BlockSpec((D_MODEL, th), lambda i, c: (0, c)),
            pl.BlockSpec((th, D_MODEL), lambda i, c: (c, 0)),
            pl.BlockSpec((tm, D_MODEL), lambda i, c: (i, 0)),
        ],
        out_specs=pl.BlockSpec((tm, D_MODEL), lambda i, c: (i, 0)),
        out_shape=jax.ShapeDtypeStruct((t, D_MODEL), F32),
        compiler_params=pltpu.CompilerParams(
            dimension_semantics=("arbitrary", "arbitrary"),
            vmem_limit_bytes=VMEM_LIMIT_BYTES),
        name="ffn",
    )(h2, w_g, w_u, w_d, x1)


ATTN_GROUPS = 16
ATTN_TILE = ATTN_BLOCK * ATTN_GROUPS
TILE_LANES = ATTN_GROUPS * HEAD_DIM
F32_SUBLANES = 8
BF16_SUBLANES = 16


def _band_mask(slab, n_slab, span):
    blk = ATTN_BLOCK
    shift = slab.bit_length() - 1
    row = lax.broadcasted_iota(jnp.int32, (blk, 2 * blk), 0)
    col = lax.broadcasted_iota(jnp.int32, (blk, 2 * blk), 1)

    def pos(r):
        return (r & (slab - 1)) * n_slab + lax.shift_right_logical(r, shift)

    k_rel = pos(col & (blk - 1)) - jnp.where(col < blk, blk, 0)
    dist = pos(row) - k_rel
    return jnp.logical_and(dist >= 0, dist <= span), col


def _attn_body(q_ref, k_ref, v_ref, kp_ref, vp_ref, o_ref,
               acc_scr, m_scr, l_scr, qf_scr, kf_scr, vf_scr, kpf_scr, vpf_scr, *, group):
    jt = pl.program_id(2)
    blk = ATTN_BLOCK
    ones_v = jnp.ones((2 * blk, HEAD_DIM), BF16)

    qf_scr[...] = q_ref[...].astype(F32)
    kf_scr[...] = k_ref[...].astype(F32)
    vf_scr[...] = v_ref[...].astype(F32)
    kpf_scr[...] = kp_ref[blk - BF16_SUBLANES:, :].astype(F32)
    vpf_scr[...] = vp_ref[blk - BF16_SUBLANES:, :].astype(F32)

    def lanes(g):
        return slice(g * HEAD_DIM, (g + 1) * HEAD_DIM)

    n_pat = len(ATTN_PATTERNS)
    for pi, (window, dil) in enumerate(reversed(ATTN_PATTERNS)):
        span = window // dil
        n_slab = ATTN_GROUPS // dil
        slab = blk // n_slab
        use_f32 = slab % BF16_SUBLANES != 0
        band, col = _band_mask(slab, n_slab, span)
        band_first = jnp.logical_and(band, jnp.logical_or(col >= blk, jt > 0))

        def gather(cur_ref, cur_f32, prev_ref, prev_f32, res, kb, prev):
            if not prev:
                src, rows = (cur_f32 if use_f32 else cur_ref), slice(slab * kb, slab * (kb + 1))
            elif kb > 0:
                src, rows = (cur_f32 if use_f32 else cur_ref), slice(slab * (kb - 1), slab * kb)
            elif use_f32:
                src, rows = prev_f32, slice(BF16_SUBLANES - slab, BF16_SUBLANES)
            else:
                src, rows = prev_ref, slice(blk - slab, blk)
            parts = [src[rows, lanes(u * dil + res)] for u in range(n_slab)]
            out = parts[0] if n_slab == 1 else jnp.concatenate(parts, axis=0)
            return out.astype(BF16)

        def st_load(scr, res, kb):
            rows = slice(slab * kb, slab * (kb + 1))
            parts = [scr[rows, lanes(u * dil + res)] for u in range(n_slab)]
            return parts[0] if n_slab == 1 else jnp.concatenate(parts, axis=0)

        def st_store(scr, res, kb, val):
            rows = slice(slab * kb, slab * (kb + 1))
            for u in range(n_slab):
                scr[rows, lanes(u * dil + res)] = val[u * slab:(u + 1) * slab, :]

        blocks = [(res, kb) for res in range(dil) for kb in range(n_slab)]
        for g0 in range(0, len(blocks), group):
            grp = blocks[g0:g0 + group]
            scores = []
            for res, kb in grp:
                q = gather(q_ref, qf_scr, None, None, res, kb, False)
                kw = jnp.concatenate([gather(k_ref, kf_scr, kp_ref, kpf_scr, res, kb, True),
                                      gather(k_ref, kf_scr, kp_ref, kpf_scr, res, kb, False)], axis=0)
                s = lax.dot_general(q, kw, (((1,), (1,)), ((), ())), preferred_element_type=F32)
                scores.append(jnp.where(band_first if kb == 0 else band, s, MASK_VALUE))
            probs = []
            for s in scores:
                m = jnp.max(s, axis=-1, keepdims=True)
                probs.append((jnp.exp(s - m).astype(BF16), m))
            pvs = []
            for (res, kb), (p, _) in zip(grp, probs):
                vw = jnp.concatenate([gather(v_ref, vf_scr, vp_ref, vpf_scr, res, kb, True),
                                      gather(v_ref, vf_scr, vp_ref, vpf_scr, res, kb, False)], axis=0)
                pvs.append(jnp.dot(p, jnp.concatenate([vw, ones_v], axis=1), preferred_element_type=F32))
            for (res, kb), (_, m), pv in zip(grp, probs, pvs):
                num, den = pv[:, :HEAD_DIM], pv[:, HEAD_DIM:]
                m_new = jnp.broadcast_to(m, (blk, HEAD_DIM))
                if pi > 0:
                    m_old = st_load(m_scr, res, kb)
                    m_new = jnp.maximum(m_old, m_new)
                    w_old = jnp.exp(m_old - m_new)
                    w_new = jnp.exp(m - m_new)
                    num = st_load(acc_scr, res, kb) * w_old + num * w_new
                    den = st_load(l_scr, res, kb) * w_old + den * w_new
                if pi < n_pat - 1:
                    st_store(acc_scr, res, kb, num)
                    st_store(m_scr, res, kb, m_new)
                    st_store(l_scr, res, kb, den)
                else:
                    st_store(acc_scr, res, kb, num / den)

    o_ref[...] = acc_scr[...].astype(BF16)


def _dilated_attention(att, batch, seq, group=8):
    assert seq % ATTN_TILE == 0
    n_tile = seq // ATTN_TILE
    att_v = att.reshape(3, ATTN_HEADS, batch, n_tile * ATTN_BLOCK, TILE_LANES)

    def spec(which, prev):
        def index(bi, hd, jt):
            return (which, hd, bi, jnp.maximum(jt - 1, 0) if prev else jt, 0)
        return pl.BlockSpec((None, None, None, ATTN_BLOCK, TILE_LANES), index)

    tile_f32 = pltpu.VMEM((ATTN_BLOCK, TILE_LANES), F32)
    tail_f32 = pltpu.VMEM((BF16_SUBLANES, TILE_LANES), F32)
    out = pl.pallas_call(
        functools.partial(_attn_body, group=group),
        grid=(batch, ATTN_HEADS, n_tile),
        in_specs=[spec(0, False), spec(1, False), spec(2, False), spec(1, True), spec(2, True)],
        out_specs=pl.BlockSpec((None, None, ATTN_BLOCK, TILE_LANES), lambda bi, hd, jt: (hd, bi, jt, 0)),
        out_shape=jax.ShapeDtypeStruct((ATTN_HEADS, batch, n_tile * ATTN_BLOCK, TILE_LANES), BF16),
        scratch_shapes=[tile_f32, tile_f32, tile_f32, tile_f32, tile_f32, tile_f32, tail_f32, tail_f32],
        compiler_params=pltpu.CompilerParams(
            dimension_semantics=("arbitrary", "arbitrary", "arbitrary"),
            vmem_limit_bytes=VMEM_LIMIT_BYTES),
        name="dilated_attn",
    )(att_v, att_v, att_v, att_v, att_v)
    return out.reshape(ATTN_HEADS, batch * seq, HEAD_DIM)


CONV_HALO = 8


def _mlstm_body(gt_ref, q_ref, k_ref, qp_ref, kp_ref, v_ref, mo_ref, cw_ref, cb_ref, ng_ref,
                o_ref, cn_scr, m_scr):
    c = pl.program_id(1)
    L = MLSTM_CHUNK
    W = MLSTM_WIDTH

    @pl.when(c == 0)
    def _():
        cn_scr[...] = jnp.zeros_like(cn_scr)
        m_scr[...] = jnp.full_like(m_scr, M_INIT)

    has_prev = c > 0
    sub = lax.broadcasted_iota(jnp.int32, (CONV_HALO, W), 0)

    def conv_silu(cur_ref, prev_ref, col0):
        cur = cur_ref[...]
        prev = jnp.where(has_prev, prev_ref[...], 0.0)
        y = cur * cw_ref[CONV_WIDTH - 1:CONV_WIDTH, col0:col0 + W] + cb_ref[:, col0:col0 + W]
        for s in range(1, CONV_WIDTH):
            r = pltpu.roll(cur, s, 0)
            head = jnp.where(sub < s, pltpu.roll(prev, s, 0), r[:CONV_HALO])
            shifted = jnp.concatenate([head, r[CONV_HALO:]], axis=0)
            y = y + shifted * cw_ref[CONV_WIDTH - 1 - s:CONV_WIDTH - s, col0:col0 + W]
        return y * jax.nn.sigmoid(y)

    qf = conv_silu(q_ref, qp_ref, 0)
    kf = conv_silu(k_ref, kp_ref, W) * (HEAD_DIM ** -0.5)
    qb = qf.astype(BF16)

    gt = gt_ref[...]
    ig = gt[:MLSTM_HEADS]
    fg = gt[MLSTM_HEADS:]
    lf = jnp.minimum(fg, 0.0) - jnp.log1p(jnp.exp(-jnp.abs(fg)))
    lane = lax.broadcasted_iota(jnp.int32, (MLSTM_HEADS, L), 1)
    b = lf
    s = 1
    while s < L:
        b = b + jnp.where(lane >= s, pltpu.roll(b, s, 1), 0.0)
        s *= 2
    c_row = ig - b
    b_last = b[:, L - 1:L]
    c_max = jnp.max(c_row, axis=1, keepdims=True)
    m_prev = m_scr[:, :1]
    m_chunk = b_last + c_max
    m_new = jnp.maximum(b_last + m_prev, m_chunk)
    decay = jnp.exp(b_last + m_prev - m_new)
    scale = jnp.exp(m_chunk - m_new)
    wa = jnp.exp(c_row - c_max)
    m_scr[...] = jnp.broadcast_to(m_new, m_scr.shape)

    row = lax.broadcasted_iota(jnp.int32, (L, L), 0)
    col = lax.broadcasted_iota(jnp.int32, (L, L), 1)
    causal = col <= row
    ones_v = jnp.ones((L, HEAD_DIM), BF16)

    for hd in range(MLSTM_HEADS):
        hs = slice(hd * HEAD_DIM, (hd + 1) * HEAD_DIM)
        one = slice(hd, hd + 1)
        q_h = qb[:, hs]
        k_t = kf[:, hs].T
        v_h = v_ref[:, hs]
        m_prev_h = m_prev[one, :]

        c_b = jnp.where(causal, jnp.broadcast_to(c_row[one, :], (L, L)), MASK_VALUE)
        mu = jnp.maximum(jnp.max(c_b, axis=-1, keepdims=True), m_prev_h)
        b_col = jnp.sum(jnp.where(causal, jnp.broadcast_to(lf[one, :], (L, L)), 0.0),
                        axis=-1, keepdims=True)
        s_qk = jnp.dot(q_h, k_t.astype(BF16), preferred_element_type=F32)
        w = jnp.exp(c_b - mu) * s_qk
        state = cn_scr[hd]
        inter = jnp.dot(q_h, state.astype(BF16), preferred_element_type=F32)
        g = jnp.exp(m_prev_h - mu)
        num = g * inter[:, :HEAD_DIM] + jnp.dot(w.astype(BF16), v_h, preferred_element_type=F32)
        den = g * inter[:, HEAD_DIM:HEAD_DIM + 1] + jnp.sum(w, axis=-1, keepdims=True)
        cell = num / jnp.maximum(jnp.abs(den), jnp.exp(-(b_col + mu)))

        ms = jnp.mean(cell * cell, axis=-1, keepdims=True)
        y = cell * lax.rsqrt(ms + EPS) * ng_ref[:, hs]
        o_ref[:, hs] = (jax.nn.sigmoid(mo_ref[:, hs]) * y).astype(BF16)

        a_t = (k_t * jnp.broadcast_to(wa[one, :], (HEAD_DIM, L))).astype(BF16)
        kvn = jnp.dot(a_t, jnp.concatenate([v_h, ones_v], axis=1), preferred_element_type=F32)
        cn_scr[hd] = decay[one, :] * state + scale[one, :] * kvn


def _mlstm(gates_t, m_f32, m_v, conv_w, conv_b, norm_g, batch, seq):
    t = batch * seq
    L = MLSTM_CHUNK
    W = MLSTM_WIDTH
    nc = seq // L
    halo_per_chunk = L // CONV_HALO

    def rows(bi, c):
        return bi * nc + c

    def halo(bi, c):
        return jnp.maximum(rows(bi, c) * halo_per_chunk - 1, 0)

    return pl.pallas_call(
        _mlstm_body,
        grid=(batch, nc),
        in_specs=[
            pl.BlockSpec((N_GATE, L), lambda bi, c: (0, rows(bi, c))),
            pl.BlockSpec((L, W), lambda bi, c: (rows(bi, c), 0)),
            pl.BlockSpec((L, W), lambda bi, c: (rows(bi, c), 1)),
            pl.BlockSpec((CONV_HALO, W), lambda bi, c: (halo(bi, c), 0)),
            pl.BlockSpec((CONV_HALO, W), lambda bi, c: (halo(bi, c), 1)),
            pl.BlockSpec((L, W), lambda bi, c: (rows(bi, c), 0)),
            pl.BlockSpec((L, W), lambda bi, c: (rows(bi, c), 2)),
            pl.BlockSpec((CONV_WIDTH, 2 * W), lambda bi, c: (0, 0)),
            pl.BlockSpec((1, 2 * W), lambda bi, c: (0, 0)),
            pl.BlockSpec((1, W), lambda bi, c: (0, 0)),
        ],
        out_specs=pl.BlockSpec((L, W), lambda bi, c: (rows(bi, c), 0)),
        out_shape=jax.ShapeDtypeStruct((t, W), BF16),
        scratch_shapes=[pltpu.VMEM((MLSTM_HEADS, HEAD_DIM, 2 * HEAD_DIM), F32),
                        pltpu.VMEM((MLSTM_HEADS, LANES), F32)],
        compiler_params=pltpu.CompilerParams(
            dimension_semantics=("arbitrary", "arbitrary"),
            vmem_limit_bytes=VMEM_LIMIT_BYTES),
        name="mlstm",
    )(gates_t, m_f32, m_f32, m_f32, m_f32, m_v, m_f32, conv_w, conv_b, norm_g)


def kernel(x, norm_mix_g, w_in, conv_w, conv_b, gate_b, q_norm_g, k_norm_g, mlstm_norm_g,
           w_out, norm_ffn_g, w_gate, w_up, w_down):
    B, S, _ = x.shape
    t = B * S
    layer = 0
    x2 = x.reshape(t, D_MODEL)

    w = w_in[layer]
    a3 = 3 * ATTN_WIDTH
    mw = MLSTM_WIDTH
    w_main = jnp.concatenate(
        [w[:, :a3 + 2 * mw], w[:, a3 + 3 * mw:a3 + 4 * mw], w[:, a3 + 2 * mw:a3 + 3 * mw]],
        axis=1).astype(BF16)
    w_gate_t = w[:, a3 + 4 * mw:].T.astype(BF16)
    b_gate = gate_b[layer].reshape(N_GATE, 1)
    head_gain = jnp.stack([
        jnp.tile(q_norm_g[layer] * (HEAD_DIM ** -0.5), ATTN_HEADS),
        jnp.tile(k_norm_g[layer], ATTN_HEADS)]).reshape(2, 1, ATTN_WIDTH)

    att, m_f32, m_v, gates_t = _in_proj(
        x2, norm_mix_g[layer].reshape(1, D_MODEL), w_main, w_gate_t, b_gate, head_gain, tm=1024)

    attn_out = _dilated_attention(att, B, S)

    mlstm_out = _mlstm(gates_t, m_f32, m_v, conv_w[layer], conv_b[layer].reshape(1, 2 * mw),
                       mlstm_norm_g[layer].reshape(1, mw), B, S)

    wo = w_out[layer].astype(BF16)
    x1, h2 = _out_proj(attn_out, mlstm_out, x2,
                       wo[:ATTN_WIDTH], wo[ATTN_WIDTH:], norm_ffn_g[layer].reshape(1, D_MODEL), tm=512)

    out = _ffn(h2, w_gate[layer].astype(BF16), w_up[layer].astype(BF16), w_down[layer].astype(BF16),
               x1, tm=512, th=512)
    return out.reshape(B, S, D_MODEL)
```

```python
import functools

import jax
import jax.numpy as jnp
from jax import lax
from jax.experimental import pallas as pl
from jax.experimental.pallas import tpu as pltpu

D_MODEL = 2048
HEAD_DIM = 128
ATTN_HEADS = 8
MLSTM_HEADS = 8
ATTN_WIDTH = ATTN_HEADS * HEAD_DIM
MLSTM_WIDTH = MLSTM_HEADS * HEAD_DIM
ATTN_PATTERNS = ((128, 1), (512, 4), (2048, 16))
ATTN_BLOCK = 128
MLSTM_CHUNK = 128
CONV_WIDTH = 4
FFN_HIDDEN = 5632
EPS = 1e-6
MASK_VALUE = -1e30
M_INIT = -1e30

N_GATE = 2 * MLSTM_HEADS
LANES = 128
VMEM_LIMIT_BYTES = 56 * 1024 * 1024

F32 = jnp.float32
BF16 = jnp.bfloat16


def _in_proj_body(x_ref, g_ref, w_ref, wg_ref, bg_ref, hg_ref,
                  oatt_ref, omf_ref, omv_ref, ogate_ref, h_scr):
    j = pl.program_id(1)

    @pl.when(j == 0)
    def _():
        x = x_ref[...]
        ms = jnp.mean(x * x, axis=-1, keepdims=True)
        hb = (x * lax.rsqrt(ms + EPS) * g_ref[...]).astype(BF16)
        h_scr[...] = hb
        ogate_ref[...] = lax.dot_general(wg_ref[...], hb, (((1,), (1,)), ((), ())),
                                         preferred_element_type=F32) + bg_ref[...]

    acc = jnp.dot(h_scr[...], w_ref[...], preferred_element_type=F32)

    @pl.when(j < 2)
    def _():
        for hd in range(ATTN_HEADS):
            sl = slice(hd * HEAD_DIM, (hd + 1) * HEAD_DIM)
            a = acc[:, sl]
            ms = jnp.mean(a * a, axis=-1, keepdims=True)
            oatt_ref[hd] = (a * lax.rsqrt(ms + EPS) * hg_ref[:, sl]).astype(BF16)

    @pl.when(j == 2)
    def _():
        for hd in range(ATTN_HEADS):
            oatt_ref[hd] = acc[:, hd * HEAD_DIM:(hd + 1) * HEAD_DIM].astype(BF16)

    @pl.when(jnp.logical_and(j >= 3, j < 6))
    def _():
        omf_ref[...] = acc

    @pl.when(j == 6)
    def _():
        omv_ref[...] = acc.astype(BF16)


def _in_proj(x2, g_mix, w_main, w_gate_t, b_gate, head_gain, tm):
    t = x2.shape[0]
    tn = ATTN_WIDTH
    n_col = w_main.shape[1] // tn
    return pl.pallas_call(
        _in_proj_body,
        grid=(t // tm, n_col),
        in_specs=[
            pl.BlockSpec((tm, D_MODEL), lambda i, j: (i, 0)),
            pl.BlockSpec((1, D_MODEL), lambda i, j: (0, 0)),
            pl.BlockSpec((D_MODEL, tn), lambda i, j: (0, j)),
            pl.BlockSpec((N_GATE, D_MODEL), lambda i, j: (0, 0)),
            pl.BlockSpec((N_GATE, 1), lambda i, j: (0, 0)),
            pl.BlockSpec((None, 1, tn), lambda i, j: (jnp.minimum(j, 1), 0, 0)),
        ],
        out_specs=[
            pl.BlockSpec((None, ATTN_HEADS, tm, HEAD_DIM), lambda i, j: (jnp.minimum(j, 2), 0, i, 0)),
            pl.BlockSpec((tm, tn), lambda i, j: (i, jnp.clip(j - 3, 0, 2))),
            pl.BlockSpec((tm, tn), lambda i, j: (i, 0)),
            pl.BlockSpec((N_GATE, tm), lambda i, j: (0, i)),
        ],
        out_shape=[
            jax.ShapeDtypeStruct((3, ATTN_HEADS, t, HEAD_DIM), BF16),
            jax.ShapeDtypeStruct((t, 3 * tn), F32),
            jax.ShapeDtypeStruct((t, tn), BF16),
            jax.ShapeDtypeStruct((N_GATE, t), F32),
        ],
        scratch_shapes=[pltpu.VMEM((tm, D_MODEL), BF16)],
        compiler_params=pltpu.CompilerParams(
            dimension_semantics=("arbitrary", "arbitrary"),
            vmem_limit_bytes=VMEM_LIMIT_BYTES),
        name="in_proj",
    )(x2, g_mix, w_main, w_gate_t, b_gate, head_gain)


def _out_proj_body(a_ref, m_ref, x_ref, wa_ref, wm_ref, g_ref, x1_ref, h2_ref):
    a = jnp.concatenate([a_ref[hd] for hd in range(ATTN_HEADS)], axis=1)
    y = jnp.dot(a, wa_ref[...], preferred_element_type=F32)
    y = y + jnp.dot(m_ref[...], wm_ref[...], preferred_element_type=F32)
    y = y + x_ref[...]
    x1_ref[...] = y
    ms = jnp.mean(y * y, axis=-1, keepdims=True)
    h2_ref[...] = (y * lax.rsqrt(ms + EPS) * g_ref[...]).astype(BF16)


def _out_proj(attn, mlstm, x2, w_a, w_m, g_ffn, tm):
    t = x2.shape[0]
    return pl.pallas_call(
        _out_proj_body,
        grid=(t // tm,),
        in_specs=[
            pl.BlockSpec((ATTN_HEADS, tm, HEAD_DIM), lambda i: (0, i, 0)),
            pl.BlockSpec((tm, MLSTM_WIDTH), lambda i: (i, 0)),
            pl.BlockSpec((tm, D_MODEL), lambda i: (i, 0)),
            pl.BlockSpec((ATTN_WIDTH, D_MODEL), lambda i: (0, 0)),
            pl.BlockSpec((MLSTM_WIDTH, D_MODEL), lambda i: (0, 0)),
            pl.BlockSpec((1, D_MODEL), lambda i: (0, 0)),
        ],
        out_specs=[
            pl.BlockSpec((tm, D_MODEL), lambda i: (i, 0)),
            pl.BlockSpec((tm, D_MODEL), lambda i: (i, 0)),
        ],
        out_shape=[
            jax.ShapeDtypeStruct((t, D_MODEL), F32),
            jax.ShapeDtypeStruct((t, D_MODEL), BF16),
        ],
        compiler_params=pltpu.CompilerParams(
            dimension_semantics=("arbitrary",),
            vmem_limit_bytes=VMEM_LIMIT_BYTES),
        name="out_proj",
    )(attn, mlstm, x2, w_a, w_m, g_ffn)


def _ffn_body(h_ref, wg_ref, wu_ref, wd_ref, x1_ref, o_ref):
    c = pl.program_id(1)
    h = h_ref[...]
    g = jnp.dot(h, wg_ref[...], preferred_element_type=F32)
    u = jnp.dot(h, wu_ref[...], preferred_element_type=F32)
    a = (g * jax.nn.sigmoid(g) * u).astype(BF16)
    d = jnp.dot(a, wd_ref[...], preferred_element_type=F32)

    @pl.when(c == 0)
    def _():
        o_ref[...] = x1_ref[...] + d

    @pl.when(c > 0)
    def _():
        o_ref[...] += d


def _ffn(h2, w_g, w_u, w_d, x1, tm, th):
    t = h2.shape[0]
    return pl.pallas_call(
        _ffn_body,
        grid=(t // tm, FFN_HIDDEN // th),
        in_specs=[
            pl.BlockSpec((tm, D_MODEL), lambda i, c: (i, 0)),
            pl.BlockSpec((D_MODEL, th), lambda i, c: (0, c)),
            pl.BlockSpec((D_MODEL, th), lambda i, c: (0, c)),
            pl.BlockSpec((th, D_MODEL), lambda i, c: (c, 0)),
            pl.BlockSpec((tm, D_MODEL), lambda i, c: (i, 0)),
        ],
        out_specs=pl.BlockSpec((tm, D_MODEL), lambda i, c: (i, 0)),
        out_shape=jax.ShapeDtypeStruct((t, D_MODEL), F32),
        compiler_params=pltpu.CompilerParams(
            dimension_semantics=("arbitrary", "arbitrary"),
            vmem_limit_bytes=VMEM_LIMIT_BYTES),
        name="ffn",
    )(h2, w_g, w_u, w_d, x1)


ATTN_GROUPS = 16
ATTN_TILE = ATTN_BLOCK * ATTN_GROUPS
TILE_LANES = ATTN_GROUPS * HEAD_DIM
F32_SUBLANES = 8
BF16_SUBLANES = 16
STAGE_PITCH = 24


def _band_mask(slab, n_slab, span):
    blk = ATTN_BLOCK
    shift = slab.bit_length() - 1
    row = lax.broadcasted_iota(jnp.int32, (blk, 2 * blk), 0)
    col = lax.broadcasted_iota(jnp.int32, (blk, 2 * blk), 1)

    def pos(r):
        return (r & (slab - 1)) * n_slab + lax.shift_right_logical(r, shift)

    k_rel = pos(col & (blk - 1)) - jnp.where(col < blk, blk, 0)
    dist = pos(row) - k_rel
    return jnp.logical_and(dist >= 0, dist <= span), col


def _attn_body(q_in, k_in, v_in, o_ref,
               acc_scr, m_scr, l_scr, stage_q, stage_k, stage_v,
               q_ref, k_ref, v_ref, qf_scr, kf_scr, vf_scr,
               kp_ref, vp_ref, kpf_scr, vpf_scr, *, group):
    jt = pl.program_id(2)
    blk = ATTN_BLOCK
    ones_v = jnp.ones((2 * blk, HEAD_DIM), BF16)

    def lanes(g):
        return slice(g * HEAD_DIM, (g + 1) * HEAD_DIM)

    @pl.when(jt == 0)
    def _():
        kp_ref[...] = jnp.zeros_like(kp_ref)
        vp_ref[...] = jnp.zeros_like(vp_ref)
        kpf_scr[...] = jnp.zeros_like(kpf_scr)
        vpf_scr[...] = jnp.zeros_like(vpf_scr)

    for src, stage, dst_bf16, dst_f32 in ((q_in, stage_q, q_ref, qf_scr),
                                          (k_in, stage_k, k_ref, kf_scr),
                                          (v_in, stage_v, v_ref, vf_scr)):
        for i in range(blk):
            stage[STAGE_PITCH * i:STAGE_PITCH * i + ATTN_GROUPS, :] = (
                src[ATTN_GROUPS * i:ATTN_GROUPS * (i + 1), :].astype(F32))
        for g in range(ATTN_GROUPS):
            cls = stage[pl.ds(g, blk, stride=STAGE_PITCH), :]
            dst_f32[:, lanes(g)] = cls
            dst_bf16[:, lanes(g)] = cls.astype(BF16)

    n_pat = len(ATTN_PATTERNS)
    for pi, (window, dil) in enumerate(reversed(ATTN_PATTERNS)):
        span = window // dil
        n_slab = ATTN_GROUPS // dil
        slab = blk // n_slab
        use_f32 = slab % BF16_SUBLANES != 0
        band, col = _band_mask(slab, n_slab, span)
        band_first = jnp.logical_and(band, jnp.logical_or(col >= blk, jt > 0))

        def gather(cur_ref, cur_f32, prev_ref, prev_f32, res, kb, prev):
            if not prev:
                src, rows = (cur_f32 if use_f32 else cur_ref), slice(slab * kb, slab * (kb + 1))
            elif kb > 0:
                src, rows = (cur_f32 if use_f32 else cur_ref), slice(slab * (kb - 1), slab * kb)
            elif use_f32:
                src, rows = prev_f32, slice(F32_SUBLANES - slab, F32_SUBLANES)
            else:
                src, rows = prev_ref, slice(blk - slab, blk)
            parts = [src[rows, lanes(u * dil + res)] for u in range(n_slab)]
            out = parts[0] if n_slab == 1 else jnp.concatenate(parts, axis=0)
            return out.astype(BF16)

        def st_load(scr, res, kb):
            rows = slice(slab * kb, slab * (kb + 1))
            parts = [scr[rows, lanes(u * dil + res)] for u in range(n_slab)]
            return parts[0] if n_slab == 1 else jnp.concatenate(parts, axis=0)

        def st_store(scr, res, kb, val):
            rows = slice(slab * kb, slab * (kb + 1))
            for u in range(n_slab):
                scr[rows, lanes(u * dil + res)] = val[u * slab:(u + 1) * slab, :]

        blocks = [(res, kb) for res in range(dil) for kb in range(n_slab)]
        for g0 in range(0, len(blocks), group):
            grp = blocks[g0:g0 + group]
            scores = []
            for res, kb in grp:
                q = gather(q_ref, qf_scr, None, None, res, kb, False)
                kw = jnp.concatenate([gather(k_ref, kf_scr, kp_ref, kpf_scr, res, kb, True),
                                      gather(k_ref, kf_scr, kp_ref, kpf_scr, res, kb, False)], axis=0)
                s = lax.dot_general(q, kw, (((1,), (1,)), ((), ())), preferred_element_type=F32)
                scores.append(jnp.where(band_first if kb == 0 else band, s, MASK_VALUE))
            probs = []
            for s in scores:
                m = jnp.max(s, axis=-1, keepdims=True)
                probs.append((jnp.exp(s - m).astype(BF16), m))
            pvs = []
            for (res, kb), (p, _) in zip(grp, probs):
                vw = jnp.concatenate([gather(v_ref, vf_scr, vp_ref, vpf_scr, res, kb, True),
                                      gather(v_ref, vf_scr, vp_ref, vpf_scr, res, kb, False)], axis=0)
                pvs.append(jnp.dot(p, jnp.concatenate([vw, ones_v], axis=1), preferred_element_type=F32))
            for (res, kb), (_, m), pv in zip(grp, probs, pvs):
                num, den = pv[:, :HEAD_DIM], pv[:, HEAD_DIM:]
                m_new = jnp.broadcast_to(m, (blk, HEAD_DIM))
                if pi > 0:
                    m_old = st_load(m_scr, res, kb)
                    m_new = jnp.maximum(m_old, m_new)
                    w_old = jnp.exp(m_old - m_new)
                    w_new = jnp.exp(m - m_new)
                    num = st_load(acc_scr, res, kb) * w_old + num * w_new
                    den = st_load(l_scr, res, kb) * w_old + den * w_new
                if pi < n_pat - 1:
                    st_store(acc_scr, res, kb, num)
                    st_store(m_scr, res, kb, m_new)
                    st_store(l_scr, res, kb, den)
                else:
                    st_store(acc_scr, res, kb, num / den)

    for g in range(ATTN_GROUPS):
        stage_q[pl.ds(g, blk, stride=STAGE_PITCH), :] = acc_scr[:, lanes(g)]
    for i in range(blk):
        o_ref[ATTN_GROUPS * i:ATTN_GROUPS * (i + 1), :] = (
            stage_q[STAGE_PITCH * i:STAGE_PITCH * i + ATTN_GROUPS, :].astype(BF16))

    kp_ref[...] = k_ref[...]
    vp_ref[...] = v_ref[...]
    kpf_scr[...] = kf_scr[blk - F32_SUBLANES:, :]
    vpf_scr[...] = vf_scr[blk - F32_SUBLANES:, :]


def _dilated_attention(att, batch, seq, group=8):
    assert seq % ATTN_TILE == 0
    n_tile = seq // ATTN_TILE

    def spec(which):
        return pl.BlockSpec((None, None, ATTN_TILE, HEAD_DIM),
                            lambda bi, hd, jt: (which, hd, bi * n_tile + jt, 0))

    tile_f32 = pltpu.VMEM((ATTN_BLOCK, TILE_LANES), F32)
    tile_bf16 = pltpu.VMEM((ATTN_BLOCK, TILE_LANES), BF16)
    tail_f32 = pltpu.VMEM((F32_SUBLANES, TILE_LANES), F32)
    stage = pltpu.VMEM((ATTN_BLOCK * STAGE_PITCH, HEAD_DIM), F32)
    return pl.pallas_call(
        functools.partial(_attn_body, group=group),
        grid=(batch, ATTN_HEADS, n_tile),
        in_specs=[spec(0), spec(1), spec(2)],
        out_specs=pl.BlockSpec((None, ATTN_TILE, HEAD_DIM), lambda bi, hd, jt: (hd, bi * n_tile + jt, 0)),
        out_shape=jax.ShapeDtypeStruct((ATTN_HEADS, batch * seq, HEAD_DIM), BF16),
        scratch_shapes=[tile_f32, tile_f32, tile_f32,
                        stage, stage, stage,
                        tile_bf16, tile_bf16, tile_bf16,
                        tile_f32, tile_f32, tile_f32,
                        tile_bf16, tile_bf16,
                        tail_f32, tail_f32],
        compiler_params=pltpu.CompilerParams(
            dimension_semantics=("arbitrary", "arbitrary", "arbitrary"),
            vmem_limit_bytes=VMEM_LIMIT_BYTES),
        name="dilated_attn",
    )(att, att, att)


CONV_HALO = 8


def _mlstm_body(gt_ref, q_ref, k_ref, qp_ref, kp_ref, v_ref, mo_ref, cw_ref, cb_ref, ng_ref,
                o_ref, cn_scr, m_scr, qext_scr, kext_scr):
    c = pl.program_id(1)
    L = MLSTM_CHUNK
    W = MLSTM_WIDTH

    @pl.when(c == 0)
    def _():
        cn_scr[...] = jnp.zeros_like(cn_scr)
        m_scr[...] = jnp.full_like(m_scr, M_INIT)

    has_prev = c > 0

    def conv_silu(cur_ref, prev_ref, ext, col0):
        ext[:CONV_HALO, :] = jnp.where(has_prev, prev_ref[...], 0.0)
        ext[CONV_HALO:, :] = cur_ref[...]
        y = cb_ref[:, col0:col0 + W]
        for s in range(CONV_WIDTH):
            tap = cw_ref[CONV_WIDTH - 1 - s:CONV_WIDTH - s, col0:col0 + W]
            y = y + ext[CONV_HALO - s:CONV_HALO - s + L, :] * tap
        return y * jax.nn.sigmoid(y)

    qf = conv_silu(q_ref, qp_ref, qext_scr, 0)
    kf = conv_silu(k_ref, kp_ref, kext_scr, W) * (HEAD_DIM ** -0.5)
    qb = qf.astype(BF16)

    gt = gt_ref[...]
    ig = gt[:MLSTM_HEADS]
    fg = gt[MLSTM_HEADS:]
    lf = jnp.minimum(fg, 0.0) - jnp.log1p(jnp.exp(-jnp.abs(fg)))
    lane = lax.broadcasted_iota(jnp.int32, (MLSTM_HEADS, L), 1)
    b = lf
    s = 1
    while s < L:
        b = b + jnp.where(lane >= s, pltpu.roll(b, s, 1), 0.0)
        s *= 2
    c_row = ig - b
    b_last = b[:, L - 1:L]
    c_max = jnp.max(c_row, axis=1, keepdims=True)
    m_prev = m_scr[:, :1]
    m_chunk = b_last + c_max
    m_new = jnp.maximum(b_last + m_prev, m_chunk)
    decay = jnp.exp(b_last + m_prev - m_new)
    scale = jnp.exp(m_chunk - m_new)
    wa = jnp.exp(c_row - c_max)
    m_scr[...] = jnp.broadcast_to(m_new, m_scr.shape)

    row = lax.broadcasted_iota(jnp.int32, (L, L), 0)
    col = lax.broadcasted_iota(jnp.int32, (L, L), 1)
    causal = col <= row
    ones_v = jnp.ones((L, HEAD_DIM), BF16)

    heads = range(MLSTM_HEADS)

    def hs(hd):
        return slice(hd * HEAD_DIM, (hd + 1) * HEAD_DIM)

    def one(hd):
        return slice(hd, hd + 1)

    k_ts, states, s_qks, inters = [], [], [], []
    for hd in heads:
        q_h = qb[:, hs(hd)]
        k_t = kf[:, hs(hd)].T
        state = cn_scr[hd]
        k_ts.append(k_t)
        states.append(state)
        s_qks.append(jnp.dot(q_h, k_t.astype(BF16), preferred_element_type=F32))
        inters.append(jnp.dot(q_h, state.astype(BF16), preferred_element_type=F32))

    ws, mus, gs = [], [], []
    for hd in heads:
        c_b = jnp.where(causal, jnp.broadcast_to(c_row[one(hd), :], (L, L)), MASK_VALUE)
        mu = jnp.maximum(jnp.max(c_b, axis=-1, keepdims=True), m_prev[one(hd), :])
        ws.append(jnp.exp(c_b - mu) * s_qks[hd])
        mus.append(mu)
        gs.append(jnp.exp(m_prev[one(hd), :] - mu))

    ones_sq = jnp.ones((L, L), BF16)

    def split(x):
        hi = x.astype(BF16)
        return hi, (x - hi.astype(F32)).astype(BF16)

    lf_rows = jnp.concatenate([jnp.broadcast_to(lf[one(hd), :], (HEAD_DIM, L)) for hd in heads], axis=0)
    causal_b = jnp.where(causal, 1.0, 0.0).astype(BF16)
    nt = (((1,), (1,)), ((), ()))
    lf_hi, lf_lo = split(lf_rows)
    b_all = (lax.dot_general(causal_b, lf_hi, nt, preferred_element_type=F32)
             + lax.dot_general(causal_b, lf_lo, nt, preferred_element_type=F32))

    intras, kvns, sum_los = [], [], []
    for hd in heads:
        v_aug = jnp.concatenate([v_ref[:, hs(hd)], ones_v], axis=1)
        w_hi, w_lo = split(ws[hd])
        intras.append(jnp.dot(w_hi, v_aug, preferred_element_type=F32))
        sum_los.append(jnp.dot(w_lo, ones_sq, preferred_element_type=F32))
        a_t = (k_ts[hd] * jnp.broadcast_to(wa[one(hd), :], (HEAD_DIM, L))).astype(BF16)
        kvns.append(jnp.dot(a_t, v_aug, preferred_element_type=F32))

    cells = []
    for hd in heads:
        g, mu = gs[hd], mus[hd]
        num = g * inters[hd][:, :HEAD_DIM] + intras[hd][:, :HEAD_DIM]
        den = g * inters[hd][:, HEAD_DIM:] + intras[hd][:, HEAD_DIM:] + sum_los[hd]
        cells.append(num / jnp.maximum(jnp.abs(den), jnp.exp(-(b_all[:, hs(hd)] + mu))))

    sq_sums = []
    for hd in heads:
        c_hi, c_lo = split(cells[hd] * cells[hd])
        sq_sums.append(jnp.dot(c_hi, ones_sq, preferred_element_type=F32)
                       + jnp.dot(c_lo, ones_sq, preferred_element_type=F32))
    for hd in heads:
        y = cells[hd] * lax.rsqrt(sq_sums[hd] * (1.0 / HEAD_DIM) + EPS) * ng_ref[:, hs(hd)]
        o_ref[:, hs(hd)] = (jax.nn.sigmoid(mo_ref[:, hs(hd)]) * y).astype(BF16)
        cn_scr[hd] = decay[one(hd), :] * states[hd] + scale[one(hd), :] * kvns[hd]


def _mlstm(gates_t, m_f32, m_v, conv_w, conv_b, norm_g, batch, seq):
    t = batch * seq
    L = MLSTM_CHUNK
    W = MLSTM_WIDTH
    nc = seq // L
    halo_per_chunk = L // CONV_HALO

    def rows(bi, c):
        return bi * nc + c

    def halo(bi, c):
        return jnp.maximum(rows(bi, c) * halo_per_chunk - 1, 0)

    return pl.pallas_call(
        _mlstm_body,
        grid=(batch, nc),
        in_specs=[
            pl.BlockSpec((N_GATE, L), lambda bi, c: (0, rows(bi, c))),
            pl.BlockSpec((L, W), lambda bi, c: (rows(bi, c), 0)),
            pl.BlockSpec((L, W), lambda bi, c: (rows(bi, c), 1)),
            pl.BlockSpec((CONV_HALO, W), lambda bi, c: (halo(bi, c), 0)),
            pl.BlockSpec((CONV_HALO, W), lambda bi, c: (halo(bi, c), 1)),
            pl.BlockSpec((L, W), lambda bi, c: (rows(bi, c), 0)),
            pl.BlockSpec((L, W), lambda bi, c: (rows(bi, c), 2)),
            pl.BlockSpec((CONV_WIDTH, 2 * W), lambda bi, c: (0, 0)),
            pl.BlockSpec((1, 2 * W), lambda bi, c: (0, 0)),
            pl.BlockSpec((1, W), lambda bi, c: (0, 0)),
        ],
        out_specs=pl.BlockSpec((L, W), lambda bi, c: (rows(bi, c), 0)),
        out_shape=jax.ShapeDtypeStruct((t, W), BF16),
        scratch_shapes=[pltpu.VMEM((MLSTM_HEADS, HEAD_DIM, 2 * HEAD_DIM), F32),
                        pltpu.VMEM((MLSTM_HEADS, LANES), F32),
                        pltpu.VMEM((CONV_HALO + L, W), F32),
                        pltpu.VMEM((CONV_HALO + L, W), F32)],
        compiler_params=pltpu.CompilerParams(
            dimension_semantics=("arbitrary", "arbitrary"),
            vmem_limit_bytes=VMEM_LIMIT_BYTES),
        name="mlstm",
    )(gates_t, m_f32, m_f32, m_f32, m_f32, m_v, m_f32, conv_w, conv_b, norm_g)


def kernel(x, norm_mix_g, w_in, conv_w, conv_b, gate_b, q_norm_g, k_norm_g, mlstm_norm_g,
           w_out, norm_ffn_g, w_gate, w_up, w_down):
    B, S, _ = x.shape
    t = B * S
    layer = 0
    x2 = x.reshape(t, D_MODEL)

    w = w_in[layer]
    a3 = 3 * ATTN_WIDTH
    mw = MLSTM_WIDTH
    w_main = jnp.concatenate(
        [w[:, :a3 + 2 * mw], w[:, a3 + 3 * mw:a3 + 4 * mw], w[:, a3 + 2 * mw:a3 + 3 * mw]],
        axis=1).astype(BF16)
    w_gate_t = w[:, a3 + 4 * mw:].T.astype(BF16)
    b_gate = gate_b[layer].reshape(N_GATE, 1)
    head_gain = jnp.stack([
        jnp.tile(q_norm_g[layer] * (HEAD_DIM ** -0.5), ATTN_HEADS),
        jnp.tile(k_norm_g[layer], ATTN_HEADS)]).reshape(2, 1, ATTN_WIDTH)

    att, m_f32, m_v, gates_t = _in_proj(
        x2, norm_mix_g[layer].reshape(1, D_MODEL), w_main, w_gate_t, b_gate, head_gain, tm=1024)

    attn_out = _dilated_attention(att, B, S)

    mlstm_out = _mlstm(gates_t, m_f32, m_v, conv_w[layer], conv_b[layer].reshape(1, 2 * mw),
                       mlstm_norm_g[layer].reshape(1, mw), B, S)

    wo = w_out[layer].astype(BF16)
    x1, h2 = _out_proj(attn_out, mlstm_out, x2,
                       wo[:ATTN_WIDTH], wo[ATTN_WIDTH:], norm_ffn_g[layer].reshape(1, D_MODEL), tm=512)

    out = _ffn(h2, w_gate[layer].astype(BF16), w_up[layer].astype(BF16), w_down[layer].astype(BF16),
               x1, tm=512, th=512)
    return out.reshape(B, S, D_MODEL)
```

```python
import functools

import jax
import jax.numpy as jnp
from jax import lax
from jax.experimental import pallas as pl
from jax.experimental.pallas import tpu as pltpu

D_MODEL = 2048
HEAD_DIM = 128
ATTN_HEADS = 8
MLSTM_HEADS = 8
ATTN_WIDTH = ATTN_HEADS * HEAD_DIM
MLSTM_WIDTH = MLSTM_HEADS * HEAD_DIM
ATTN_PATTERNS = ((128, 1), (512, 4), (2048, 16))
ATTN_BLOCK = 128
MLSTM_CHUNK = 128
CONV_WIDTH = 4
FFN_HIDDEN = 5632
EPS = 1e-6
MASK_VALUE = -1e30
M_INIT = -1e30

LOG2_E = 1.4426950408889634
N_GATE = 2 * MLSTM_HEADS
LANES = 128
VMEM_LIMIT_BYTES = 56 * 1024 * 1024

F32 = jnp.float32
BF16 = jnp.bfloat16


def _in_proj_body(x_ref, g_ref, w_ref, wg_ref, bg_ref, hg_ref,
                  oatt_ref, omf_ref, omv_ref, ogate_ref, h_scr):
    j = pl.program_id(1)

    @pl.when(j == 0)
    def _():
        x = x_ref[...]
        ms = jnp.mean(x * x, axis=-1, keepdims=True)
        hb = (x * lax.rsqrt(ms + EPS) * g_ref[...]).astype(BF16)
        h_scr[...] = hb
        ogate_ref[...] = lax.dot_general(wg_ref[...], hb, (((1,), (1,)), ((), ())),
                                         preferred_element_type=F32) + bg_ref[...]

    def project():
        return jnp.dot(h_scr[...], w_ref[...], preferred_element_type=F32)

    @pl.when(j < 2)
    def _():
        acc = project()
        for hd in range(ATTN_HEADS):
            sl = slice(hd * HEAD_DIM, (hd + 1) * HEAD_DIM)
            a = acc[:, sl]
            ms = jnp.mean(a * a, axis=-1, keepdims=True)
            oatt_ref[hd] = (a * lax.rsqrt(ms + EPS) * hg_ref[:, sl]).astype(BF16)

    @pl.when(j == 2)
    def _():
        acc = project()
        for hd in range(ATTN_HEADS):
            oatt_ref[hd] = acc[:, hd * HEAD_DIM:(hd + 1) * HEAD_DIM].astype(BF16)

    @pl.when(jnp.logical_and(j >= 3, j < 6))
    def _():
        omf_ref[...] = project()

    @pl.when(j == 6)
    def _():
        omv_ref[...] = project().astype(BF16)


def _in_proj(x2, g_mix, w_main, w_gate_t, b_gate, head_gain, tm):
    t = x2.shape[0]
    tn = ATTN_WIDTH
    n_col = w_main.shape[1] // tn

    def w_col(j):
        return jnp.where(j == 5, 6, jnp.where(j == 6, 5, j))

    return pl.pallas_call(
        _in_proj_body,
        grid=(t // tm, n_col),
        in_specs=[
            pl.BlockSpec((tm, D_MODEL), lambda i, j: (i, 0)),
            pl.BlockSpec((1, D_MODEL), lambda i, j: (0, 0)),
            pl.BlockSpec((D_MODEL, tn), lambda i, j: (0, w_col(j))),
            pl.BlockSpec((N_GATE, D_MODEL), lambda i, j: (0, 0)),
            pl.BlockSpec((N_GATE, 1), lambda i, j: (0, 0)),
            pl.BlockSpec((None, 1, tn), lambda i, j: (jnp.minimum(j, 1), 0, 0)),
        ],
        out_specs=[
            pl.BlockSpec((None, ATTN_HEADS, tm, HEAD_DIM), lambda i, j: (jnp.minimum(j, 2), 0, i, 0)),
            pl.BlockSpec((tm, tn), lambda i, j: (i, jnp.clip(j - 3, 0, 2))),
            pl.BlockSpec((tm, tn), lambda i, j: (i, 0)),
            pl.BlockSpec((N_GATE, tm), lambda i, j: (0, i)),
        ],
        out_shape=[
            jax.ShapeDtypeStruct((3, ATTN_HEADS, t, HEAD_DIM), BF16),
            jax.ShapeDtypeStruct((t, 3 * tn), F32),
            jax.ShapeDtypeStruct((t, tn), BF16),
            jax.ShapeDtypeStruct((N_GATE, t), F32),
        ],
        scratch_shapes=[pltpu.VMEM((tm, D_MODEL), BF16)],
        compiler_params=pltpu.CompilerParams(
            dimension_semantics=("arbitrary", "arbitrary"),
            vmem_limit_bytes=VMEM_LIMIT_BYTES),
        name="in_proj",
    )(x2, g_mix, w_main, w_gate_t, b_gate, head_gain)


def _out_proj_body(a_ref, m_ref, x_ref, wa_ref, wm_ref, g_ref, x1_ref, h2_ref):
    a = jnp.concatenate([a_ref[hd] for hd in range(ATTN_HEADS)], axis=1)
    y = jnp.dot(a, wa_ref[...], preferred_element_type=F32)
    y = y + jnp.dot(m_ref[...], wm_ref[...], preferred_element_type=F32)
    y = y + x_ref[...]
    x1_ref[...] = y
    ms = jnp.mean(y * y, axis=-1, keepdims=True)
    h2_ref[...] = (y * lax.rsqrt(ms + EPS) * g_ref[...]).astype(BF16)


def _out_proj(attn, mlstm, x2, w_o, g_ffn, tm):
    t = x2.shape[0]
    assert ATTN_WIDTH == MLSTM_WIDTH
    return pl.pallas_call(
        _out_proj_body,
        grid=(t // tm,),
        in_specs=[
            pl.BlockSpec((ATTN_HEADS, tm, HEAD_DIM), lambda i: (0, i, 0)),
            pl.BlockSpec((tm, MLSTM_WIDTH), lambda i: (i, 0)),
            pl.BlockSpec((tm, D_MODEL), lambda i: (i, 0)),
            pl.BlockSpec((ATTN_WIDTH, D_MODEL), lambda i: (0, 0)),
            pl.BlockSpec((MLSTM_WIDTH, D_MODEL), lambda i: (1, 0)),
            pl.BlockSpec((1, D_MODEL), lambda i: (0, 0)),
        ],
        out_specs=[
            pl.BlockSpec((tm, D_MODEL), lambda i: (i, 0)),
            pl.BlockSpec((tm, D_MODEL), lambda i: (i, 0)),
        ],
        out_shape=[
            jax.ShapeDtypeStruct((t, D_MODEL), F32),
            jax.ShapeDtypeStruct((t, D_MODEL), BF16),
        ],
        compiler_params=pltpu.CompilerParams(
            dimension_semantics=("arbitrary",),
            vmem_limit_bytes=VMEM_LIMIT_BYTES),
        name="out_proj",
    )(attn, mlstm, x2, w_o, w_o, g_ffn)


FFN_DOWN_CHUNK = 512


def _ffn_body(h_ref, wg_ref, wu_ref, wd_ref, x1_ref, o_ref):
    c = pl.program_id(1)

    @pl.when(c == 0)
    def _():
        o_ref[...] = x1_ref[...]

    h = h_ref[...]
    g = jnp.dot(h, wg_ref[...], preferred_element_type=F32)
    u = jnp.dot(h, wu_ref[...], preferred_element_type=F32)
    a = (g * jax.nn.sigmoid(g) * u).astype(BF16)
    for n0 in range(0, D_MODEL, FFN_DOWN_CHUNK):
        cols = slice(n0, n0 + FFN_DOWN_CHUNK)
        o_ref[:, cols] += jnp.dot(a, wd_ref[:, cols], preferred_element_type=F32)


def _ffn(h2, w_g, w_u, w_d, x1, tm, th):
    t = h2.shape[0]
    return pl.pallas_call(
        _ffn_body,
        grid=(t // tm, FFN_HIDDEN // th),
        in_specs=[
            pl.BlockSpec((tm, D_MODEL), lambda i, c: (i, 0)),
            pl.BlockSpec((D_MODEL, th), lambda i, c: (0, c)),
            pl.BlockSpec((D_MODEL, th), lambda i, c: (0, c)),
            pl.BlockSpec((th, D_MODEL), lambda i, c: (c, 0)),
            pl.BlockSpec((tm, D_MODEL), lambda i, c: (i, 0), pipeline_mode=pl.Buffered(1)),
        ],
        out_specs=pl.BlockSpec((tm, D_MODEL), lambda i, c: (i, 0)),
        out_shape=jax.ShapeDtypeStruct((t, D_MODEL), F32),
        compiler_params=pltpu.CompilerParams(
            dimension_semantics=("arbitrary", "arbitrary"),
            vmem_limit_bytes=VMEM_LIMIT_BYTES),
        name="ffn",
    )(h2, w_g, w_u, w_d, x1)


ATTN_GROUPS = 16
ATTN_TILE = ATTN_BLOCK * ATTN_GROUPS
TILE_LANES = ATTN_GROUPS * HEAD_DIM
F32_SUBLANES = 8
BF16_SUBLANES = 16
STAGE_PITCH = 24


def _band_mask(slab, n_slab, span):
    blk = ATTN_BLOCK
    shift = slab.bit_length() - 1
    row = lax.broadcasted_iota(jnp.int32, (blk, 2 * blk), 0)
    col = lax.broadcasted_iota(jnp.int32, (blk, 2 * blk), 1)

    def pos(r):
        return (r & (slab - 1)) * n_slab + lax.shift_right_logical(r, shift)

    k_rel = pos(col & (blk - 1)) - jnp.where(col < blk, blk, 0)
    dist = pos(row) - k_rel
    return jnp.logical_and(dist >= 0, dist <= span), col


def _attn_body(q_in, k_in, v_in, o_ref,
               acc_scr, m_scr, l_scr, stage_q, stage_k, stage_v,
               q_ref, k_ref, v_ref, qf_scr, kf_scr, vf_scr,
               kp_ref, vp_ref, kpf_scr, vpf_scr, *, group):
    jt = pl.program_id(2)
    blk = ATTN_BLOCK
    ones_v = jnp.ones((2 * blk, HEAD_DIM), BF16)

    def lanes(g):
        return slice(g * HEAD_DIM, (g + 1) * HEAD_DIM)

    @pl.when(jt == 0)
    def _():
        kp_ref[...] = jnp.zeros_like(kp_ref)
        vp_ref[...] = jnp.zeros_like(vp_ref)
        kpf_scr[...] = jnp.zeros_like(kpf_scr)
        vpf_scr[...] = jnp.zeros_like(vpf_scr)

    for src, stage, dst_bf16, dst_f32 in ((q_in, stage_q, q_ref, qf_scr),
                                          (k_in, stage_k, k_ref, kf_scr),
                                          (v_in, stage_v, v_ref, vf_scr)):
        for i in range(blk):
            stage[STAGE_PITCH * i:STAGE_PITCH * i + ATTN_GROUPS, :] = (
                src[ATTN_GROUPS * i:ATTN_GROUPS * (i + 1), :].astype(F32))
        for g in range(ATTN_GROUPS):
            cls = stage[pl.ds(g, blk, stride=STAGE_PITCH), :]
            dst_f32[:, lanes(g)] = cls
            dst_bf16[:, lanes(g)] = cls.astype(BF16)

    n_pat = len(ATTN_PATTERNS)
    for pi, (window, dil) in enumerate(reversed(ATTN_PATTERNS)):
        span = window // dil
        n_slab = ATTN_GROUPS // dil
        slab = blk // n_slab
        use_f32 = slab % BF16_SUBLANES != 0
        band, col = _band_mask(slab, n_slab, span)
        band_first = jnp.logical_and(band, jnp.logical_or(col >= blk, jt > 0))

        def gather(cur_ref, cur_f32, prev_ref, prev_f32, res, kb, prev):
            if not prev:
                src, rows = (cur_f32 if use_f32 else cur_ref), slice(slab * kb, slab * (kb + 1))
            elif kb > 0:
                src, rows = (cur_f32 if use_f32 else cur_ref), slice(slab * (kb - 1), slab * kb)
            elif use_f32:
                src, rows = prev_f32, slice(F32_SUBLANES - slab, F32_SUBLANES)
            else:
                src, rows = prev_ref, slice(blk - slab, blk)
            parts = [src[rows, lanes(u * dil + res)] for u in range(n_slab)]
            out = parts[0] if n_slab == 1 else jnp.concatenate(parts, axis=0)
            return out.astype(BF16)

        def st_load(scr, res, kb):
            rows = slice(slab * kb, slab * (kb + 1))
            parts = [scr[rows, lanes(u * dil + res)] for u in range(n_slab)]
            return parts[0] if n_slab == 1 else jnp.concatenate(parts, axis=0)

        def st_store(scr, res, kb, val):
            rows = slice(slab * kb, slab * (kb + 1))
            for u in range(n_slab):
                scr[rows, lanes(u * dil + res)] = val[u * slab:(u + 1) * slab, :]

        blocks = [(res, kb) for res in range(dil) for kb in range(n_slab)]
        for g0 in range(0, len(blocks), group):
            grp = blocks[g0:g0 + group]
            scores = []
            for res, kb in grp:
                q = gather(q_ref, qf_scr, None, None, res, kb, False)
                kw = jnp.concatenate([gather(k_ref, kf_scr, kp_ref, kpf_scr, res, kb, True),
                                      gather(k_ref, kf_scr, kp_ref, kpf_scr, res, kb, False)], axis=0)
                s = lax.dot_general(q, kw, (((1,), (1,)), ((), ())), preferred_element_type=F32)
                scores.append(jnp.where(band_first if kb == 0 else band, s, MASK_VALUE))
            probs = []
            for s in scores:
                m = jnp.max(s, axis=-1, keepdims=True)
                probs.append((jnp.exp2(s - m).astype(BF16), m))
            pvs = []
            for (res, kb), (p, _) in zip(grp, probs):
                vw = jnp.concatenate([gather(v_ref, vf_scr, vp_ref, vpf_scr, res, kb, True),
                                      gather(v_ref, vf_scr, vp_ref, vpf_scr, res, kb, False)], axis=0)
                pvs.append(jnp.dot(p, jnp.concatenate([vw, ones_v], axis=1), preferred_element_type=F32))
            for (res, kb), (_, m), pv in zip(grp, probs, pvs):
                num, den = pv[:, :HEAD_DIM], pv[:, HEAD_DIM:]
                m_new = jnp.broadcast_to(m, (blk, HEAD_DIM))
                if pi > 0:
                    m_old = st_load(m_scr, res, kb)
                    m_new = jnp.maximum(m_old, m_new)
                    w_old = jnp.exp2(m_old - m_new)
                    w_new = jnp.exp2(m - m_new)
                    num = st_load(acc_scr, res, kb) * w_old + num * w_new
                    den = st_load(l_scr, res, kb) * w_old + den * w_new
                if pi < n_pat - 1:
                    st_store(acc_scr, res, kb, num)
                    st_store(m_scr, res, kb, m_new)
                    st_store(l_scr, res, kb, den)
                else:
                    st_store(acc_scr, res, kb, num / den)

    for g in range(ATTN_GROUPS):
        stage_q[pl.ds(g, blk, stride=STAGE_PITCH), :] = acc_scr[:, lanes(g)]
    for i in range(blk):
        o_ref[ATTN_GROUPS * i:ATTN_GROUPS * (i + 1), :] = (
            stage_q[STAGE_PITCH * i:STAGE_PITCH * i + ATTN_GROUPS, :].astype(BF16))

    kp_ref[...] = k_ref[...]
    vp_ref[...] = v_ref[...]
    kpf_scr[...] = kf_scr[blk - F32_SUBLANES:, :]
    vpf_scr[...] = vf_scr[blk - F32_SUBLANES:, :]


def _dilated_attention(att, batch, seq, group=8):
    assert seq % ATTN_TILE == 0
    n_tile = seq // ATTN_TILE

    def spec(which):
        return pl.BlockSpec((None, None, ATTN_TILE, HEAD_DIM),
                            lambda bi, hd, jt: (which, hd, bi * n_tile + jt, 0))

    tile_f32 = pltpu.VMEM((ATTN_BLOCK, TILE_LANES), F32)
    tile_bf16 = pltpu.VMEM((ATTN_BLOCK, TILE_LANES), BF16)
    tail_f32 = pltpu.VMEM((F32_SUBLANES, TILE_LANES), F32)
    stage = pltpu.VMEM((ATTN_BLOCK * STAGE_PITCH, HEAD_DIM), F32)
    return pl.pallas_call(
        functools.partial(_attn_body, group=group),
        grid=(batch, ATTN_HEADS, n_tile),
        in_specs=[spec(0), spec(1), spec(2)],
        out_specs=pl.BlockSpec((None, ATTN_TILE, HEAD_DIM), lambda bi, hd, jt: (hd, bi * n_tile + jt, 0)),
        out_shape=jax.ShapeDtypeStruct((ATTN_HEADS, batch * seq, HEAD_DIM), BF16),
        scratch_shapes=[tile_f32, tile_f32, tile_f32,
                        stage, stage, stage,
                        tile_bf16, tile_bf16, tile_bf16,
                        tile_f32, tile_f32, tile_f32,
                        tile_bf16, tile_bf16,
                        tail_f32, tail_f32],
        compiler_params=pltpu.CompilerParams(
            dimension_semantics=("arbitrary", "arbitrary", "arbitrary"),
            vmem_limit_bytes=VMEM_LIMIT_BYTES),
        name="dilated_attn",
    )(att, att, att)


CONV_HALO = 8


def _mlstm_body(gt_ref, q_ref, k_ref, qp_ref, kp_ref, v_ref, mo_ref, cw_ref, cb_ref, ng_ref,
                o_ref, cn_scr, m_scr, qext_scr, kext_scr):
    c = pl.program_id(1)
    L = MLSTM_CHUNK
    W = MLSTM_WIDTH

    @pl.when(c == 0)
    def _():
        cn_scr[...] = jnp.zeros_like(cn_scr)
        m_scr[...] = jnp.full_like(m_scr, M_INIT)

    has_prev = c > 0

    def conv_silu(cur_ref, prev_ref, ext, col0):
        ext[:CONV_HALO, :] = jnp.where(has_prev, prev_ref[...], 0.0)
        ext[CONV_HALO:, :] = cur_ref[...]
        y = cb_ref[:, col0:col0 + W]
        for s in range(CONV_WIDTH):
            tap = cw_ref[CONV_WIDTH - 1 - s:CONV_WIDTH - s, col0:col0 + W]
            y = y + ext[CONV_HALO - s:CONV_HALO - s + L, :] * tap
        return y * jax.nn.sigmoid(y)

    qf = conv_silu(q_ref, qp_ref, qext_scr, 0)
    kf = conv_silu(k_ref, kp_ref, kext_scr, W) * (HEAD_DIM ** -0.5)
    qb = qf.astype(BF16)

    gt = gt_ref[...]
    ig = gt[:MLSTM_HEADS]
    fg = gt[MLSTM_HEADS:]
    lf = jnp.minimum(fg, 0.0) - jnp.log1p(jnp.exp(-jnp.abs(fg)))
    lane = lax.broadcasted_iota(jnp.int32, (MLSTM_HEADS, L), 1)
    b = lf
    s = 1
    while s < L:
        b = b + jnp.where(lane >= s, pltpu.roll(b, s, 1), 0.0)
        s *= 2
    c_row = ig - b
    b_last = b[:, L - 1:L]
    c_max = jnp.max(c_row, axis=1, keepdims=True)
    m_prev = m_scr[:, :1]
    m_chunk = b_last + c_max
    m_new = jnp.maximum(b_last + m_prev, m_chunk)
    decay = jnp.exp(b_last + m_prev - m_new)
    scale = jnp.exp(m_chunk - m_new)
    wa = jnp.exp(c_row - c_max)
    m_scr[...] = jnp.broadcast_to(m_new, m_scr.shape)

    row = lax.broadcasted_iota(jnp.int32, (L, L), 0)
    col = lax.broadcasted_iota(jnp.int32, (L, L), 1)
    causal = col <= row
    ones_v = jnp.ones((L, HEAD_DIM), BF16)

    heads = range(MLSTM_HEADS)

    def hs(hd):
        return slice(hd * HEAD_DIM, (hd + 1) * HEAD_DIM)

    def one(hd):
        return slice(hd, hd + 1)

    k_ts, states, s_qks, inters = [], [], [], []
    for hd in heads:
        q_h = qb[:, hs(hd)]
        k_t = kf[:, hs(hd)].T
        state = cn_scr[hd]
        k_ts.append(k_t)
        states.append(state)
        s_qks.append(jnp.dot(q_h, k_t.astype(BF16), preferred_element_type=F32))
        inters.append(jnp.dot(q_h, state.astype(BF16), preferred_element_type=F32))

    ws, mus, gs = [], [], []
    for hd in heads:
        c_b = jnp.where(causal, jnp.broadcast_to(c_row[one(hd), :], (L, L)), MASK_VALUE)
        mu = jnp.maximum(jnp.max(c_b, axis=-1, keepdims=True), m_prev[one(hd), :])
        ws.append(jnp.exp(c_b - mu) * s_qks[hd])
        mus.append(mu)
        gs.append(jnp.exp(m_prev[one(hd), :] - mu))

    ones_sq = jnp.ones((L, L), BF16)

    def split(x):
        hi = x.astype(BF16)
        return hi, (x - hi.astype(F32)).astype(BF16)

    lf_rows = jnp.concatenate([jnp.broadcast_to(lf[one(hd), :], (HEAD_DIM, L)) for hd in heads], axis=0)
    causal_b = jnp.where(causal, 1.0, 0.0).astype(BF16)
    nt = (((1,), (1,)), ((), ()))
    lf_hi, lf_lo = split(lf_rows)
    b_all = (lax.dot_general(causal_b, lf_hi, nt, preferred_element_type=F32)
             + lax.dot_general(causal_b, lf_lo, nt, preferred_element_type=F32))

    intras, kvns, sum_los = [], [], []
    for hd in heads:
        v_aug = jnp.concatenate([v_ref[:, hs(hd)], ones_v], axis=1)
        w_hi, w_lo = split(ws[hd])
        intras.append(jnp.dot(w_hi, v_aug, preferred_element_type=F32))
        sum_los.append(jnp.dot(w_lo, ones_sq, preferred_element_type=F32))
        a_t = (k_ts[hd] * jnp.broadcast_to(wa[one(hd), :], (HEAD_DIM, L))).astype(BF16)
        kvns.append(jnp.dot(a_t, v_aug, preferred_element_type=F32))

    cells = []
    for hd in heads:
        g, mu = gs[hd], mus[hd]
        num = g * inters[hd][:, :HEAD_DIM] + intras[hd][:, :HEAD_DIM]
        den = g * inters[hd][:, HEAD_DIM:] + intras[hd][:, HEAD_DIM:] + sum_los[hd]
        cells.append(num / jnp.maximum(jnp.abs(den), jnp.exp(-(b_all[:, hs(hd)] + mu))))

    sq_sums = []
    for hd in heads:
        c_hi, c_lo = split(cells[hd] * cells[hd])
        sq_sums.append(jnp.dot(c_hi, ones_sq, preferred_element_type=F32)
                       + jnp.dot(c_lo, ones_sq, preferred_element_type=F32))
    for hd in heads:
        y = cells[hd] * lax.rsqrt(sq_sums[hd] * (1.0 / HEAD_DIM) + EPS) * ng_ref[:, hs(hd)]
        o_ref[:, hs(hd)] = (jax.nn.sigmoid(mo_ref[:, hs(hd)]) * y).astype(BF16)
        cn_scr[hd] = decay[one(hd), :] * states[hd] + scale[one(hd), :] * kvns[hd]


def _mlstm(gates_t, m_f32, m_v, conv_w, conv_b, norm_g, batch, seq):
    t = batch * seq
    L = MLSTM_CHUNK
    W = MLSTM_WIDTH
    nc = seq // L
    halo_per_chunk = L // CONV_HALO

    def rows(bi, c):
        return bi * nc + c

    def halo(bi, c):
        return jnp.maximum(rows(bi, c) * halo_per_chunk - 1, 0)

    return pl.pallas_call(
        _mlstm_body,
        grid=(batch, nc),
        in_specs=[
            pl.BlockSpec((N_GATE, L), lambda bi, c: (0, rows(bi, c))),
            pl.BlockSpec((L, W), lambda bi, c: (rows(bi, c), 0)),
            pl.BlockSpec((L, W), lambda bi, c: (rows(bi, c), 1)),
            pl.BlockSpec((CONV_HALO, W), lambda bi, c: (halo(bi, c), 0)),
            pl.BlockSpec((CONV_HALO, W), lambda bi, c: (halo(bi, c), 1)),
            pl.BlockSpec((L, W), lambda bi, c: (rows(bi, c), 0)),
            pl.BlockSpec((L, W), lambda bi, c: (rows(bi, c), 2)),
            pl.BlockSpec((CONV_WIDTH, 2 * W), lambda bi, c: (0, 0)),
            pl.BlockSpec((1, 2 * W), lambda bi, c: (0, 0)),
            pl.BlockSpec((1, W), lambda bi, c: (0, 0)),
        ],
        out_specs=pl.BlockSpec((L, W), lambda bi, c: (rows(bi, c), 0)),
        out_shape=jax.ShapeDtypeStruct((t, W), BF16),
        scratch_shapes=[pltpu.VMEM((MLSTM_HEADS, HEAD_DIM, 2 * HEAD_DIM), F32),
                        pltpu.VMEM((MLSTM_HEADS, LANES), F32),
                        pltpu.VMEM((CONV_HALO + L, W), F32),
                        pltpu.VMEM((CONV_HALO + L, W), F32)],
        compiler_params=pltpu.CompilerParams(
            dimension_semantics=("arbitrary", "arbitrary"),
            vmem_limit_bytes=VMEM_LIMIT_BYTES),
        name="mlstm",
    )(gates_t, m_f32, m_f32, m_f32, m_f32, m_v, m_f32, conv_w, conv_b, norm_g)


def kernel(x, norm_mix_g, w_in, conv_w, conv_b, gate_b, q_norm_g, k_norm_g, mlstm_norm_g,
           w_out, norm_ffn_g, w_gate, w_up, w_down):
    B, S, _ = x.shape
    t = B * S
    layer = 0
    x2 = x.reshape(t, D_MODEL)

    w = w_in[layer]
    a3 = 3 * ATTN_WIDTH
    mw = MLSTM_WIDTH
    w_main = w.astype(BF16)
    w_gate_t = w[:, a3 + 4 * mw:].T.astype(BF16)
    b_gate = gate_b[layer].reshape(N_GATE, 1)
    head_gain = jnp.stack([
        jnp.tile(q_norm_g[layer] * (HEAD_DIM ** -0.5 * LOG2_E), ATTN_HEADS),
        jnp.tile(k_norm_g[layer], ATTN_HEADS)]).reshape(2, 1, ATTN_WIDTH)

    att, m_f32, m_v, gates_t = _in_proj(
        x2, norm_mix_g[layer].reshape(1, D_MODEL), w_main, w_gate_t, b_gate, head_gain, tm=1024)

    attn_out = _dilated_attention(att, B, S)

    mlstm_out = _mlstm(gates_t, m_f32, m_v, conv_w[layer], conv_b[layer].reshape(1, 2 * mw),
                       mlstm_norm_g[layer].reshape(1, mw), B, S)

    x1, h2 = _out_proj(attn_out, mlstm_out, x2, w_out[layer].astype(BF16),
                       norm_ffn_g[layer].reshape(1, D_MODEL), tm=512)

    out = _ffn(h2, w_gate[layer].astype(BF16), w_up[layer].astype(BF16), w_down[layer].astype(BF16),
               x1, tm=1024, th=512)
    return out.reshape(B, S, D_MODEL)
```

```python
import functools

import jax
import jax.numpy as jnp
from jax import lax
from jax.experimental import pallas as pl
from jax.experimental.pallas import tpu as pltpu

D_MODEL = 2048
HEAD_DIM = 128
ATTN_HEADS = 8
MLSTM_HEADS = 8
ATTN_WIDTH = ATTN_HEADS * HEAD_DIM
MLSTM_WIDTH = MLSTM_HEADS * HEAD_DIM
ATTN_PATTERNS = ((128, 1), (512, 4), (2048, 16))
ATTN_BLOCK = 128
MLSTM_CHUNK = 128
CONV_WIDTH = 4
FFN_HIDDEN = 5632
EPS = 1e-6
MASK_VALUE = -1e30
M_INIT = -1e30

LOG2_E = 1.4426950408889634
N_GATE = 2 * MLSTM_HEADS
LANES = 128
VMEM_LIMIT_BYTES = 56 * 1024 * 1024

F32 = jnp.float32
BF16 = jnp.bfloat16


def _in_proj_body(x_ref, g_ref, w_ref, wg_ref, bg_ref, hg_ref, *rest, n_cast):
    cast_in, rest = rest[:n_cast], rest[n_cast:]
    oatt_ref, omf_ref, omv_ref, ogate_ref = rest[:4]
    cast_out, h_scr = rest[4:4 + n_cast], rest[4 + n_cast]
    j = pl.program_id(1)

    for src, dst in zip(cast_in, cast_out):
        dst[...] = src[...].astype(BF16)

    def project(h=None):
        h = h_scr[...] if h is None else h
        return jnp.dot(h, w_ref[...], preferred_element_type=F32)

    def store_head_normed(acc):
        for hd in range(ATTN_HEADS):
            sl = slice(hd * HEAD_DIM, (hd + 1) * HEAD_DIM)
            a = acc[:, sl]
            ms = jnp.mean(a * a, axis=-1, keepdims=True)
            oatt_ref[hd] = (a * lax.rsqrt(ms + EPS) * hg_ref[:, sl]).astype(BF16)

    @pl.when(j == 0)
    def _():
        x = x_ref[...]
        ms = jnp.mean(x * x, axis=-1, keepdims=True)
        hb = (x * lax.rsqrt(ms + EPS) * g_ref[...]).astype(BF16)
        h_scr[...] = hb
        ogate_ref[...] = lax.dot_general(wg_ref[...].astype(BF16), hb, (((1,), (1,)), ((), ())),
                                         preferred_element_type=F32) + bg_ref[...]
        store_head_normed(project(hb))

    @pl.when(j == 1)
    def _():
        store_head_normed(project())

    @pl.when(j == 2)
    def _():
        acc = project()
        for hd in range(ATTN_HEADS):
            oatt_ref[hd] = acc[:, hd * HEAD_DIM:(hd + 1) * HEAD_DIM].astype(BF16)

    @pl.when(jnp.logical_and(j >= 3, j < 6))
    def _():
        omf_ref[...] = project()

    @pl.when(j == 6)
    def _():
        omv_ref[...] = project().astype(BF16)


CAST_BLOCK = (256, 512)


def _in_proj(x2, g_mix, w_main, w_gate_t, b_gate, head_gain, side_weights, tm):
    t = x2.shape[0]
    tn = ATTN_WIDTH
    n_col = w_main.shape[1] // tn
    n_step = (t // tm) * n_col

    def w_col(j):
        return jnp.where(j == 5, 6, jnp.where(j == 6, 5, j))

    cast_specs, cast_shapes = [], []
    for wt in side_weights:
        rows, cols = wt.shape
        blk = CAST_BLOCK if cols % CAST_BLOCK[1] == 0 and rows % CAST_BLOCK[0] == 0 else CAST_BLOCK[::-1]
        n_c = cols // blk[1]
        n_blk = (rows // blk[0]) * n_c
        assert rows % blk[0] == 0 and cols % blk[1] == 0 and n_blk <= n_step

        def index(i, j, n_c=n_c, n_blk=n_blk):
            k = jnp.minimum(i * n_col + j, n_blk - 1)
            return (k // n_c, k % n_c)

        cast_specs.append(pl.BlockSpec(blk, index))
        cast_shapes.append(jax.ShapeDtypeStruct(wt.shape, BF16))

    return pl.pallas_call(
        functools.partial(_in_proj_body, n_cast=len(side_weights)),
        grid=(t // tm, n_col),
        in_specs=[
            pl.BlockSpec((tm, D_MODEL), lambda i, j: (i, 0)),
            pl.BlockSpec((1, D_MODEL), lambda i, j: (0, 0)),
            pl.BlockSpec((D_MODEL, tn), lambda i, j: (0, w_col(j))),
            pl.BlockSpec((N_GATE, D_MODEL), lambda i, j: (0, 0)),
            pl.BlockSpec((N_GATE, 1), lambda i, j: (0, 0)),
            pl.BlockSpec((None, 1, tn), lambda i, j: (jnp.minimum(j, 1), 0, 0)),
        ] + cast_specs,
        out_specs=[
            pl.BlockSpec((None, ATTN_HEADS, tm, HEAD_DIM), lambda i, j: (jnp.minimum(j, 2), 0, i, 0)),
            pl.BlockSpec((tm, tn), lambda i, j: (i, jnp.clip(j - 3, 0, 2))),
            pl.BlockSpec((tm, tn), lambda i, j: (i, 0)),
            pl.BlockSpec((N_GATE, tm), lambda i, j: (0, i)),
        ] + cast_specs,
        out_shape=[
            jax.ShapeDtypeStruct((3, ATTN_HEADS, t, HEAD_DIM), BF16),
            jax.ShapeDtypeStruct((t, 3 * tn), F32),
            jax.ShapeDtypeStruct((t, tn), BF16),
            jax.ShapeDtypeStruct((N_GATE, t), F32),
        ] + cast_shapes,
        scratch_shapes=[pltpu.VMEM((tm, D_MODEL), BF16)],
        compiler_params=pltpu.CompilerParams(
            dimension_semantics=("arbitrary", "arbitrary"),
            vmem_limit_bytes=VMEM_LIMIT_BYTES),
        name="in_proj",
    )(x2, g_mix, w_main, w_gate_t, b_gate, head_gain, *side_weights)


def _out_proj_body(a_ref, m_ref, x_ref, wa_ref, wm_ref, g_ref, x1_ref, h2_ref):
    a = jnp.concatenate([a_ref[hd] for hd in range(ATTN_HEADS)], axis=1)
    y = jnp.dot(a, wa_ref[...], preferred_element_type=F32)
    y = y + jnp.dot(m_ref[...], wm_ref[...], preferred_element_type=F32)
    y = y + x_ref[...]
    x1_ref[...] = y
    ms = jnp.mean(y * y, axis=-1, keepdims=True)
    h2_ref[...] = (y * lax.rsqrt(ms + EPS) * g_ref[...]).astype(BF16)


def _out_proj(attn, mlstm, x2, w_o, g_ffn, tm):
    t = x2.shape[0]
    assert ATTN_WIDTH == MLSTM_WIDTH
    return pl.pallas_call(
        _out_proj_body,
        grid=(t // tm,),
        in_specs=[
            pl.BlockSpec((ATTN_HEADS, tm, HEAD_DIM), lambda i: (0, i, 0)),
            pl.BlockSpec((tm, MLSTM_WIDTH), lambda i: (i, 0)),
            pl.BlockSpec((tm, D_MODEL), lambda i: (i, 0)),
            pl.BlockSpec((ATTN_WIDTH, D_MODEL), lambda i: (0, 0)),
            pl.BlockSpec((MLSTM_WIDTH, D_MODEL), lambda i: (1, 0)),
            pl.BlockSpec((1, D_MODEL), lambda i: (0, 0)),
        ],
        out_specs=[
            pl.BlockSpec((tm, D_MODEL), lambda i: (i, 0)),
            pl.BlockSpec((tm, D_MODEL), lambda i: (i, 0)),
        ],
        out_shape=[
            jax.ShapeDtypeStruct((t, D_MODEL), F32),
            jax.ShapeDtypeStruct((t, D_MODEL), BF16),
        ],
        compiler_params=pltpu.CompilerParams(
            dimension_semantics=("arbitrary",),
            vmem_limit_bytes=VMEM_LIMIT_BYTES),
        name="out_proj",
    )(attn, mlstm, x2, w_o, w_o, g_ffn)


FFN_DOWN_CHUNK = 512


def _ffn_body(h_ref, wg_ref, wu_ref, wd_ref, x1_ref, o_ref, *, n_res):
    c = pl.program_id(1)

    @pl.when(c == 0)
    def _():
        o_ref[...] = jnp.zeros_like(o_ref)

    @pl.when(c < n_res)
    def _():
        slab = x1_ref.shape[0]
        rows = pl.ds(pl.multiple_of(c * slab, slab), slab)
        o_ref[rows, :] += x1_ref[...]

    h = h_ref[...]
    g = jnp.dot(h, wg_ref[...], preferred_element_type=F32)
    u = jnp.dot(h, wu_ref[...], preferred_element_type=F32)
    a = (g * jax.nn.sigmoid(g) * u).astype(BF16)
    for n0 in range(0, D_MODEL, FFN_DOWN_CHUNK):
        cols = slice(n0, n0 + FFN_DOWN_CHUNK)
        o_ref[:, cols] += jnp.dot(a, wd_ref[:, cols], preferred_element_type=F32)


def _ffn(h2, w_g, w_u, w_d, x1, tm, th):
    t = h2.shape[0]
    n_chunk = FFN_HIDDEN // th
    n_res = 8
    assert n_res <= n_chunk and tm % n_res == 0
    return pl.pallas_call(
        functools.partial(_ffn_body, n_res=n_res),
        grid=(t // tm, n_chunk),
        in_specs=[
            pl.BlockSpec((tm, D_MODEL), lambda i, c: (i, 0)),
            pl.BlockSpec((D_MODEL, th), lambda i, c: (0, c)),
            pl.BlockSpec((D_MODEL, th), lambda i, c: (0, c)),
            pl.BlockSpec((th, D_MODEL), lambda i, c: (c, 0)),
            pl.BlockSpec((tm // n_res, D_MODEL), lambda i, c: (i * n_res + jnp.minimum(c, n_res - 1), 0)),
        ],
        out_specs=pl.BlockSpec((tm, D_MODEL), lambda i, c: (i, 0)),
        out_shape=jax.ShapeDtypeStruct((t, D_MODEL), F32),
        compiler_params=pltpu.CompilerParams(
            dimension_semantics=("arbitrary", "arbitrary"),
            vmem_limit_bytes=VMEM_LIMIT_BYTES),
        name="ffn",
    )(h2, w_g, w_u, w_d, x1)


ATTN_GROUPS = 16
ATTN_TILE = ATTN_BLOCK * ATTN_GROUPS
TILE_LANES = ATTN_GROUPS * HEAD_DIM
F32_SUBLANES = 8
BF16_SUBLANES = 16
STAGE_PITCH = 24


def _band_mask(slab, n_slab, span):
    blk = ATTN_BLOCK
    shift = slab.bit_length() - 1
    row = lax.broadcasted_iota(jnp.int32, (blk, 2 * blk), 0)
    col = lax.broadcasted_iota(jnp.int32, (blk, 2 * blk), 1)

    def pos(r):
        return (r & (slab - 1)) * n_slab + lax.shift_right_logical(r, shift)

    k_rel = pos(col & (blk - 1)) - jnp.where(col < blk, blk, 0)
    dist = pos(row) - k_rel
    return jnp.logical_and(dist >= 0, dist <= span), col


def _attn_body(q_in, k_in, v_in, o_ref,
               acc_scr, m_scr, l_scr, stage_q, stage_k, stage_v,
               q_ref, k_ref, v_ref, qf_scr, kf_scr, vf_scr,
               kp_ref, vp_ref, kpf_scr, vpf_scr, *, group):
    jt = pl.program_id(2)
    blk = ATTN_BLOCK
    ones_v = jnp.ones((2 * blk, HEAD_DIM), BF16)

    def lanes(g):
        return slice(g * HEAD_DIM, (g + 1) * HEAD_DIM)

    @pl.when(jt == 0)
    def _():
        kp_ref[...] = jnp.zeros_like(kp_ref)
        vp_ref[...] = jnp.zeros_like(vp_ref)
        kpf_scr[...] = jnp.zeros_like(kpf_scr)
        vpf_scr[...] = jnp.zeros_like(vpf_scr)

    for src, stage, dst_bf16, dst_f32 in ((q_in, stage_q, q_ref, qf_scr),
                                          (k_in, stage_k, k_ref, kf_scr),
                                          (v_in, stage_v, v_ref, vf_scr)):
        for i in range(blk):
            stage[STAGE_PITCH * i:STAGE_PITCH * i + ATTN_GROUPS, :] = (
                src[ATTN_GROUPS * i:ATTN_GROUPS * (i + 1), :].astype(F32))
        for g in range(ATTN_GROUPS):
            cls = stage[pl.ds(g, blk, stride=STAGE_PITCH), :]
            dst_f32[:, lanes(g)] = cls
            dst_bf16[:, lanes(g)] = cls.astype(BF16)

    n_pat = len(ATTN_PATTERNS)
    for pi, (window, dil) in enumerate(reversed(ATTN_PATTERNS)):
        span = window // dil
        n_slab = ATTN_GROUPS // dil
        slab = blk // n_slab
        use_f32 = slab % BF16_SUBLANES != 0
        band, col = _band_mask(slab, n_slab, span)
        band_first = jnp.logical_and(band, jnp.logical_or(col >= blk, jt > 0))

        def gather(cur_ref, cur_f32, prev_ref, prev_f32, res, kb, prev):
            if not prev:
                src, rows = (cur_f32 if use_f32 else cur_ref), slice(slab * kb, slab * (kb + 1))
            elif kb > 0:
                src, rows = (cur_f32 if use_f32 else cur_ref), slice(slab * (kb - 1), slab * kb)
            elif use_f32:
                src, rows = prev_f32, slice(F32_SUBLANES - slab, F32_SUBLANES)
            else:
                src, rows = prev_ref, slice(blk - slab, blk)
            parts = [src[rows, lanes(u * dil + res)] for u in range(n_slab)]
            out = parts[0] if n_slab == 1 else jnp.concatenate(parts, axis=0)
            return out.astype(BF16)

        def st_load(scr, res, kb):
            rows = slice(slab * kb, slab * (kb + 1))
            parts = [scr[rows, lanes(u * dil + res)] for u in range(n_slab)]
            return parts[0] if n_slab == 1 else jnp.concatenate(parts, axis=0)

        def st_store(scr, res, kb, val):
            rows = slice(slab * kb, slab * (kb + 1))
            for u in range(n_slab):
                scr[rows, lanes(u * dil + res)] = val[u * slab:(u + 1) * slab, :]

        blocks = [(res, kb) for res in range(dil) for kb in range(n_slab)]
        for g0 in range(0, len(blocks), group):
            grp = blocks[g0:g0 + group]
            scores = []
            for res, kb in grp:
                q = gather(q_ref, qf_scr, None, None, res, kb, False)
                kw = jnp.concatenate([gather(k_ref, kf_scr, kp_ref, kpf_scr, res, kb, True),
                                      gather(k_ref, kf_scr, kp_ref, kpf_scr, res, kb, False)], axis=0)
                s = lax.dot_general(q, kw, (((1,), (1,)), ((), ())), preferred_element_type=F32)
                scores.append(jnp.where(band_first if kb == 0 else band, s, MASK_VALUE))
            probs = []
            for s in scores:
                m = jnp.max(s, axis=-1, keepdims=True)
                probs.append((jnp.exp2(s - m).astype(BF16), m))
            pvs = []
            for (res, kb), (p, _) in zip(grp, probs):
                vw = jnp.concatenate([gather(v_ref, vf_scr, vp_ref, vpf_scr, res, kb, True),
                                      gather(v_ref, vf_scr, vp_ref, vpf_scr, res, kb, False)], axis=0)
                pvs.append(jnp.dot(p, jnp.concatenate([vw, ones_v], axis=1), preferred_element_type=F32))
            for (res, kb), (_, m), pv in zip(grp, probs, pvs):
                num, den = pv[:, :HEAD_DIM], pv[:, HEAD_DIM:]
                m_new = jnp.broadcast_to(m, (blk, HEAD_DIM))
                if pi > 0:
                    m_old = st_load(m_scr, res, kb)
                    m_new = jnp.maximum(m_old, m_new)
                    w_old = jnp.exp2(m_old - m_new)
                    w_new = jnp.exp2(m - m_new)
                    num = st_load(acc_scr, res, kb) * w_old + num * w_new
                    den = st_load(l_scr, res, kb) * w_old + den * w_new
                if pi < n_pat - 1:
                    st_store(acc_scr, res, kb, num)
                    st_store(m_scr, res, kb, m_new)
                    st_store(l_scr, res, kb, den)
                else:
                    st_store(acc_scr, res, kb, num / den)

    for g in range(ATTN_GROUPS):
        stage_q[pl.ds(g, blk, stride=STAGE_PITCH), :] = acc_scr[:, lanes(g)]
    for i in range(blk):
        o_ref[ATTN_GROUPS * i:ATTN_GROUPS * (i + 1), :] = (
            stage_q[STAGE_PITCH * i:STAGE_PITCH * i + ATTN_GROUPS, :].astype(BF16))

    kp_ref[...] = k_ref[...]
    vp_ref[...] = v_ref[...]
    kpf_scr[...] = kf_scr[blk - F32_SUBLANES:, :]
    vpf_scr[...] = vf_scr[blk - F32_SUBLANES:, :]


def _dilated_attention(att, batch, seq, group=8):
    assert seq % ATTN_TILE == 0
    n_tile = seq // ATTN_TILE

    def spec(which):
        return pl.BlockSpec((None, None, ATTN_TILE, HEAD_DIM),
                            lambda bi, hd, jt: (which, hd, bi * n_tile + jt, 0))

    tile_f32 = pltpu.VMEM((ATTN_BLOCK, TILE_LANES), F32)
    tile_bf16 = pltpu.VMEM((ATTN_BLOCK, TILE_LANES), BF16)
    tail_f32 = pltpu.VMEM((F32_SUBLANES, TILE_LANES), F32)
    stage = pltpu.VMEM((ATTN_BLOCK * STAGE_PITCH, HEAD_DIM), F32)
    return pl.pallas_call(
        functools.partial(_attn_body, group=group),
        grid=(batch, ATTN_HEADS, n_tile),
        in_specs=[spec(0), spec(1), spec(2)],
        out_specs=pl.BlockSpec((None, ATTN_TILE, HEAD_DIM), lambda bi, hd, jt: (hd, bi * n_tile + jt, 0)),
        out_shape=jax.ShapeDtypeStruct((ATTN_HEADS, batch * seq, HEAD_DIM), BF16),
        scratch_shapes=[tile_f32, tile_f32, tile_f32,
                        stage, stage, stage,
                        tile_bf16, tile_bf16, tile_bf16,
                        tile_f32, tile_f32, tile_f32,
                        tile_bf16, tile_bf16,
                        tail_f32, tail_f32],
        compiler_params=pltpu.CompilerParams(
            dimension_semantics=("arbitrary", "arbitrary", "arbitrary"),
            vmem_limit_bytes=VMEM_LIMIT_BYTES),
        name="dilated_attn",
    )(att, att, att)


CONV_HALO = 8


def _mlstm_body(gt_ref, q_ref, k_ref, qp_ref, kp_ref, v_ref, mo_ref, cw_ref, cb_ref, ng_ref,
                o_ref, cn_scr, m_scr, qext_scr, kext_scr):
    c = pl.program_id(1)
    L = MLSTM_CHUNK
    W = MLSTM_WIDTH

    @pl.when(c == 0)
    def _():
        cn_scr[...] = jnp.zeros_like(cn_scr)
        m_scr[...] = jnp.full_like(m_scr, M_INIT)

    has_prev = c > 0

    def conv_silu(cur_ref, prev_ref, ext, col0):
        ext[:CONV_HALO, :] = jnp.where(has_prev, prev_ref[...], 0.0)
        ext[CONV_HALO:, :] = cur_ref[...]
        y = cb_ref[:, col0:col0 + W]
        for s in range(CONV_WIDTH):
            tap = cw_ref[CONV_WIDTH - 1 - s:CONV_WIDTH - s, col0:col0 + W]
            y = y + ext[CONV_HALO - s:CONV_HALO - s + L, :] * tap
        return y * jax.nn.sigmoid(y)

    qf = conv_silu(q_ref, qp_ref, qext_scr, 0)
    kf = conv_silu(k_ref, kp_ref, kext_scr, W) * (HEAD_DIM ** -0.5)
    qb = qf.astype(BF16)

    gt = gt_ref[...]
    ig = gt[:MLSTM_HEADS]
    fg = gt[MLSTM_HEADS:]
    lf = jnp.minimum(fg, 0.0) - jnp.log1p(jnp.exp(-jnp.abs(fg)))
    lane = lax.broadcasted_iota(jnp.int32, (MLSTM_HEADS, L), 1)
    b = lf
    s = 1
    while s < L:
        b = b + jnp.where(lane >= s, pltpu.roll(b, s, 1), 0.0)
        s *= 2
    c_row = ig - b
    b_last = b[:, L - 1:L]
    c_max = jnp.max(c_row, axis=1, keepdims=True)
    m_prev = m_scr[:, :1]
    m_chunk = b_last + c_max
    m_new = jnp.maximum(b_last + m_prev, m_chunk)
    decay = jnp.exp(b_last + m_prev - m_new)
    scale = jnp.exp(m_chunk - m_new)
    wa = jnp.exp(c_row - c_max)
    m_scr[...] = jnp.broadcast_to(m_new, m_scr.shape)

    row = lax.broadcasted_iota(jnp.int32, (L, L), 0)
    col = lax.broadcasted_iota(jnp.int32, (L, L), 1)
    causal = col <= row
    ones_v = jnp.ones((L, HEAD_DIM), BF16)

    heads = range(MLSTM_HEADS)

    def hs(hd):
        return slice(hd * HEAD_DIM, (hd + 1) * HEAD_DIM)

    def one(hd):
        return slice(hd, hd + 1)

    k_ts, states, s_qks, inters = [], [], [], []
    for hd in heads:
        q_h = qb[:, hs(hd)]
        k_t = kf[:, hs(hd)].T
        state = cn_scr[hd]
        k_ts.append(k_t)
        states.append(state)
        s_qks.append(jnp.dot(q_h, k_t.astype(BF16), preferred_element_type=F32))
        inters.append(jnp.dot(q_h, state.astype(BF16), preferred_element_type=F32))

    ws, mus, gs = [], [], []
    for hd in heads:
        c_b = jnp.where(causal, jnp.broadcast_to(c_row[one(hd), :], (L, L)), MASK_VALUE)
        mu = jnp.maximum(jnp.max(c_b, axis=-1, keepdims=True), m_prev[one(hd), :])
        ws.append(jnp.exp(c_b - mu) * s_qks[hd])
        mus.append(mu)
        gs.append(jnp.exp(m_prev[one(hd), :] - mu))

    ones_sq = jnp.ones((L, L), BF16)

    def split(x):
        hi = x.astype(BF16)
        return hi, (x - hi.astype(F32)).astype(BF16)

    lf_rows = jnp.concatenate([jnp.broadcast_to(lf[one(hd), :], (HEAD_DIM, L)) for hd in heads], axis=0)
    causal_b = jnp.where(causal, 1.0, 0.0).astype(BF16)
    nt = (((1,), (1,)), ((), ()))
    lf_hi, lf_lo = split(lf_rows)
    b_all = (lax.dot_general(causal_b, lf_hi, nt, preferred_element_type=F32)
             + lax.dot_general(causal_b, lf_lo, nt, preferred_element_type=F32))

    intras, kvns, sum_los = [], [], []
    for hd in heads:
        v_aug = jnp.concatenate([v_ref[:, hs(hd)], ones_v], axis=1)
        w_hi, w_lo = split(ws[hd])
        intras.append(jnp.dot(w_hi, v_aug, preferred_element_type=F32))
        sum_los.append(jnp.dot(w_lo, ones_sq, preferred_element_type=F32))
        a_t = (k_ts[hd] * jnp.broadcast_to(wa[one(hd), :], (HEAD_DIM, L))).astype(BF16)
        kvns.append(jnp.dot(a_t, v_aug, preferred_element_type=F32))

    cells = []
    for hd in heads:
        g, mu = gs[hd], mus[hd]
        num = g * inters[hd][:, :HEAD_DIM] + intras[hd][:, :HEAD_DIM]
        den = g * inters[hd][:, HEAD_DIM:] + intras[hd][:, HEAD_DIM:] + sum_los[hd]
        cells.append(num / jnp.maximum(jnp.abs(den), jnp.exp(-(b_all[:, hs(hd)] + mu))))

    sq_sums = []
    for hd in heads:
        c_hi, c_lo = split(cells[hd] * cells[hd])
        sq_sums.append(jnp.dot(c_hi, ones_sq, preferred_element_type=F32)
                       + jnp.dot(c_lo, ones_sq, preferred_element_type=F32))
    for hd in heads:
        y = cells[hd] * lax.rsqrt(sq_sums[hd] * (1.0 / HEAD_DIM) + EPS) * ng_ref[:, hs(hd)]
        o_ref[:, hs(hd)] = (jax.nn.sigmoid(mo_ref[:, hs(hd)]) * y).astype(BF16)
        cn_scr[hd] = decay[one(hd), :] * states[hd] + scale[one(hd), :] * kvns[hd]


def _mlstm(gates_t, m_f32, m_v, conv_w, conv_b, norm_g, batch, seq):
    t = batch * seq
    L = MLSTM_CHUNK
    W = MLSTM_WIDTH
    nc = seq // L
    halo_per_chunk = L // CONV_HALO

    def rows(bi, c):
        return bi * nc + c

    def halo(bi, c):
        return jnp.maximum(rows(bi, c) * halo_per_chunk - 1, 0)

    return pl.pallas_call(
        _mlstm_body,
        grid=(batch, nc),
        in_specs=[
            pl.BlockSpec((N_GATE, L), lambda bi, c: (0, rows(bi, c))),
            pl.BlockSpec((L, W), lambda bi, c: (rows(bi, c), 0)),
            pl.BlockSpec((L, W), lambda bi, c: (rows(bi, c), 1)),
            pl.BlockSpec((CONV_HALO, W), lambda bi, c: (halo(bi, c), 0)),
            pl.BlockSpec((CONV_HALO, W), lambda bi, c: (halo(bi, c), 1)),
            pl.BlockSpec((L, W), lambda bi, c: (rows(bi, c), 0)),
            pl.BlockSpec((L, W), lambda bi, c: (rows(bi, c), 2)),
            pl.BlockSpec((CONV_WIDTH, 2 * W), lambda bi, c: (0, 0)),
            pl.BlockSpec((1, 2 * W), lambda bi, c: (0, 0)),
            pl.BlockSpec((1, W), lambda bi, c: (0, 0)),
        ],
        out_specs=pl.BlockSpec((L, W), lambda bi, c: (rows(bi, c), 0)),
        out_shape=jax.ShapeDtypeStruct((t, W), BF16),
        scratch_shapes=[pltpu.VMEM((MLSTM_HEADS, HEAD_DIM, 2 * HEAD_DIM), F32),
                        pltpu.VMEM((MLSTM_HEADS, LANES), F32),
                        pltpu.VMEM((CONV_HALO + L, W), F32),
                        pltpu.VMEM((CONV_HALO + L, W), F32)],
        compiler_params=pltpu.CompilerParams(
            dimension_semantics=("arbitrary", "arbitrary"),
            vmem_limit_bytes=VMEM_LIMIT_BYTES),
        name="mlstm",
    )(gates_t, m_f32, m_f32, m_f32, m_f32, m_v, m_f32, conv_w, conv_b, norm_g)


def kernel(x, norm_mix_g, w_in, conv_w, conv_b, gate_b, q_norm_g, k_norm_g, mlstm_norm_g,
           w_out, norm_ffn_g, w_gate, w_up, w_down):
    B, S, _ = x.shape
    t = B * S
    layer = 0
    x2 = x.reshape(t, D_MODEL)

    w = w_in[layer]
    a3 = 3 * ATTN_WIDTH
    mw = MLSTM_WIDTH
    w_main = w.astype(BF16)
    w_gate_t = w[:, a3 + 4 * mw:].T
    b_gate = gate_b[layer].reshape(N_GATE, 1)
    head_gain = jnp.stack([
        jnp.tile(q_norm_g[layer] * (HEAD_DIM ** -0.5 * LOG2_E), ATTN_HEADS),
        jnp.tile(k_norm_g[layer], ATTN_HEADS)]).reshape(2, 1, ATTN_WIDTH)

    att, m_f32, m_v, gates_t, wo_b, wg_b, wu_b, wd_b = _in_proj(
        x2, norm_mix_g[layer].reshape(1, D_MODEL), w_main, w_gate_t, b_gate, head_gain,
        (w_out[layer], w_gate[layer], w_up[layer], w_down[layer]), tm=1024)

    attn_out = _dilated_attention(att, B, S)

    mlstm_out = _mlstm(gates_t, m_f32, m_v, conv_w[layer], conv_b[layer].reshape(1, 2 * mw),
                       mlstm_norm_g[layer].reshape(1, mw), B, S)

    x1, h2 = _out_proj(attn_out, mlstm_out, x2, wo_b, norm_ffn_g[layer].reshape(1, D_MODEL), tm=512)

    out = _ffn(h2, wg_b, wu_b, wd_b, x1, tm=1024, th=512)
    return out.reshape(B, S, D_MODEL)
```

```python
import functools

import jax
import jax.numpy as jnp
from jax import lax
from jax.experimental import pallas as pl
from jax.experimental.pallas import tpu as pltpu

D_MODEL = 2048
HEAD_DIM = 128
ATTN_HEADS = 8
MLSTM_HEADS = 8
ATTN_WIDTH = ATTN_HEADS * HEAD_DIM
MLSTM_WIDTH = MLSTM_HEADS * HEAD_DIM
ATTN_PATTERNS = ((128, 1), (512, 4), (2048, 16))
ATTN_BLOCK = 128
MLSTM_CHUNK = 128
CONV_WIDTH = 4
FFN_HIDDEN = 5632
EPS = 1e-6
MASK_VALUE = -1e30
M_INIT = -1e30

LOG2_E = 1.4426950408889634
N_GATE = 2 * MLSTM_HEADS
LANES = 128
VMEM_LIMIT_BYTES = 56 * 1024 * 1024

F32 = jnp.float32
BF16 = jnp.bfloat16


def _in_proj_body(x_ref, g_ref, w_ref, wg_ref, bg_ref, hg_ref, *rest, n_cast):
    cast_in, rest = rest[:n_cast], rest[n_cast:]
    oatt_ref, omf_ref, omv_ref, ogate_ref = rest[:4]
    cast_out, h_scr = rest[4:4 + n_cast], rest[4 + n_cast]
    j = pl.program_id(1)

    for src, dst in zip(cast_in, cast_out):
        dst[...] = src[...].astype(BF16)

    def project(h=None):
        h = h_scr[...] if h is None else h
        return jnp.dot(h, w_ref[...], preferred_element_type=F32)

    def store_head_normed(acc):
        for hd in range(ATTN_HEADS):
            sl = slice(hd * HEAD_DIM, (hd + 1) * HEAD_DIM)
            a = acc[:, sl]
            ms = jnp.mean(a * a, axis=-1, keepdims=True)
            oatt_ref[hd] = (a * lax.rsqrt(ms + EPS) * hg_ref[:, sl]).astype(BF16)

    @pl.when(j == 0)
    def _():
        x = x_ref[...]
        ms = jnp.mean(x * x, axis=-1, keepdims=True)
        hb = (x * lax.rsqrt(ms + EPS) * g_ref[...]).astype(BF16)
        h_scr[...] = hb
        ogate_ref[...] = lax.dot_general(wg_ref[...].astype(BF16), hb, (((1,), (1,)), ((), ())),
                                         preferred_element_type=F32) + bg_ref[...]
        store_head_normed(project(hb))

    @pl.when(j == 1)
    def _():
        store_head_normed(project())

    @pl.when(j == 2)
    def _():
        acc = project()
        for hd in range(ATTN_HEADS):
            oatt_ref[hd] = acc[:, hd * HEAD_DIM:(hd + 1) * HEAD_DIM].astype(BF16)

    @pl.when(jnp.logical_and(j >= 3, j < 6))
    def _():
        omf_ref[...] = project()

    @pl.when(j == 6)
    def _():
        omv_ref[...] = project().astype(BF16)


def _in_proj(x2, g_mix, w_main, w_gate_t, b_gate, head_gain, side_weights, tm):
    t = x2.shape[0]
    tn = ATTN_WIDTH
    n_col = w_main.shape[1] // tn
    n_step = (t // tm) * n_col

    def w_col(j):
        return jnp.where(j == 5, 6, jnp.where(j == 6, 5, j))

    cast_specs, cast_shapes = [], []
    for wt in side_weights:
        rows, cols = wt.shape
        n_blk = max(n for n in range(1, n_step + 1)
                    if rows % n == 0 and (rows // n) % BF16_SUBLANES == 0)
        cast_specs.append(pl.BlockSpec(
            (rows // n_blk, cols), lambda i, j, n_blk=n_blk: (jnp.minimum(i * n_col + j, n_blk - 1), 0)))
        cast_shapes.append(jax.ShapeDtypeStruct(wt.shape, BF16))

    return pl.pallas_call(
        functools.partial(_in_proj_body, n_cast=len(side_weights)),
        grid=(t // tm, n_col),
        in_specs=[
            pl.BlockSpec((tm, D_MODEL), lambda i, j: (i, 0)),
            pl.BlockSpec((1, D_MODEL), lambda i, j: (0, 0)),
            pl.BlockSpec((D_MODEL, tn), lambda i, j: (0, w_col(j))),
            pl.BlockSpec((N_GATE, D_MODEL), lambda i, j: (0, 0)),
            pl.BlockSpec((N_GATE, 1), lambda i, j: (0, 0)),
            pl.BlockSpec((None, 1, tn), lambda i, j: (jnp.minimum(j, 1), 0, 0)),
        ] + cast_specs,
        out_specs=[
            pl.BlockSpec((None, ATTN_HEADS, tm, HEAD_DIM), lambda i, j: (jnp.minimum(j, 2), 0, i, 0)),
            pl.BlockSpec((tm, tn), lambda i, j: (i, jnp.clip(j - 3, 0, 2))),
            pl.BlockSpec((tm, tn), lambda i, j: (i, 0)),
            pl.BlockSpec((N_GATE, tm), lambda i, j: (0, i)),
        ] + cast_specs,
        out_shape=[
            jax.ShapeDtypeStruct((3, ATTN_HEADS, t, HEAD_DIM), BF16),
            jax.ShapeDtypeStruct((t, 3 * tn), F32),
            jax.ShapeDtypeStruct((t, tn), BF16),
            jax.ShapeDtypeStruct((N_GATE, t), F32),
        ] + cast_shapes,
        scratch_shapes=[pltpu.VMEM((tm, D_MODEL), BF16)],
        compiler_params=pltpu.CompilerParams(
            dimension_semantics=("arbitrary", "arbitrary"),
            vmem_limit_bytes=VMEM_LIMIT_BYTES),
        name="in_proj",
    )(x2, g_mix, w_main, w_gate_t, b_gate, head_gain, *side_weights)


def _out_proj_body(a_ref, m_ref, x_ref, wa_ref, wm_ref, g_ref, x1_ref, h2_ref):
    a = jnp.concatenate([a_ref[hd] for hd in range(ATTN_HEADS)], axis=1)
    y = jnp.dot(a, wa_ref[...], preferred_element_type=F32)
    y = y + jnp.dot(m_ref[...], wm_ref[...], preferred_element_type=F32)
    y = y + x_ref[...]
    x1_ref[...] = y
    ms = jnp.mean(y * y, axis=-1, keepdims=True)
    h2_ref[...] = (y * lax.rsqrt(ms + EPS) * g_ref[...]).astype(BF16)


def _out_proj(attn, mlstm, x2, w_o, g_ffn, tm):
    t = x2.shape[0]
    assert ATTN_WIDTH == MLSTM_WIDTH
    return pl.pallas_call(
        _out_proj_body,
        grid=(t // tm,),
        in_specs=[
            pl.BlockSpec((ATTN_HEADS, tm, HEAD_DIM), lambda i: (0, i, 0)),
            pl.BlockSpec((tm, MLSTM_WIDTH), lambda i: (i, 0)),
            pl.BlockSpec((tm, D_MODEL), lambda i: (i, 0)),
            pl.BlockSpec((ATTN_WIDTH, D_MODEL), lambda i: (0, 0)),
            pl.BlockSpec((MLSTM_WIDTH, D_MODEL), lambda i: (1, 0)),
            pl.BlockSpec((1, D_MODEL), lambda i: (0, 0)),
        ],
        out_specs=[
            pl.BlockSpec((tm, D_MODEL), lambda i: (i, 0)),
            pl.BlockSpec((tm, D_MODEL), lambda i: (i, 0)),
        ],
        out_shape=[
            jax.ShapeDtypeStruct((t, D_MODEL), F32),
            jax.ShapeDtypeStruct((t, D_MODEL), BF16),
        ],
        compiler_params=pltpu.CompilerParams(
            dimension_semantics=("arbitrary",),
            vmem_limit_bytes=VMEM_LIMIT_BYTES),
        name="out_proj",
    )(attn, mlstm, x2, w_o, w_o, g_ffn)


FFN_DOWN_CHUNK = 512


def _ffn_body(h_ref, wg_ref, wu_ref, wd_ref, x1_ref, o_ref, *, n_res):
    c = pl.program_id(1)

    @pl.when(c == 0)
    def _():
        o_ref[...] = jnp.zeros_like(o_ref)

    @pl.when(c < n_res)
    def _():
        slab = x1_ref.shape[0]
        rows = pl.ds(pl.multiple_of(c * slab, slab), slab)
        o_ref[rows, :] += x1_ref[...]

    h = h_ref[...]
    g = jnp.dot(h, wg_ref[...], preferred_element_type=F32)
    u = jnp.dot(h, wu_ref[...], preferred_element_type=F32)
    a = (g * jax.nn.sigmoid(g) * u).astype(BF16)
    for n0 in range(0, D_MODEL, FFN_DOWN_CHUNK):
        cols = slice(n0, n0 + FFN_DOWN_CHUNK)
        o_ref[:, cols] += jnp.dot(a, wd_ref[:, cols], preferred_element_type=F32)


def _ffn(h2, w_g, w_u, w_d, x1, tm, th):
    t = h2.shape[0]
    n_chunk = FFN_HIDDEN // th
    n_res = 8
    assert n_res <= n_chunk and tm % n_res == 0
    return pl.pallas_call(
        functools.partial(_ffn_body, n_res=n_res),
        grid=(t // tm, n_chunk),
        in_specs=[
            pl.BlockSpec((tm, D_MODEL), lambda i, c: (i, 0)),
            pl.BlockSpec((D_MODEL, th), lambda i, c: (0, c)),
            pl.BlockSpec((D_MODEL, th), lambda i, c: (0, c)),
            pl.BlockSpec((th, D_MODEL), lambda i, c: (c, 0)),
            pl.BlockSpec((tm // n_res, D_MODEL), lambda i, c: (i * n_res + jnp.minimum(c, n_res - 1), 0)),
        ],
        out_specs=pl.BlockSpec((tm, D_MODEL), lambda i, c: (i, 0)),
        out_shape=jax.ShapeDtypeStruct((t, D_MODEL), F32),
        compiler_params=pltpu.CompilerParams(
            dimension_semantics=("arbitrary", "arbitrary"),
            vmem_limit_bytes=VMEM_LIMIT_BYTES),
        name="ffn",
    )(h2, w_g, w_u, w_d, x1)


ATTN_GROUPS = 16
ATTN_TILE = ATTN_BLOCK * ATTN_GROUPS
TILE_LANES = ATTN_GROUPS * HEAD_DIM
F32_SUBLANES = 8
BF16_SUBLANES = 16
STAGE_PITCH = 24


def _band_mask(slab, n_slab, span):
    blk = ATTN_BLOCK
    shift = slab.bit_length() - 1
    row = lax.broadcasted_iota(jnp.int32, (blk, 2 * blk), 0)
    col = lax.broadcasted_iota(jnp.int32, (blk, 2 * blk), 1)

    def pos(r):
        return (r & (slab - 1)) * n_slab + lax.shift_right_logical(r, shift)

    k_rel = pos(col & (blk - 1)) - jnp.where(col < blk, blk, 0)
    dist = pos(row) - k_rel
    return jnp.logical_and(dist >= 0, dist <= span), col


def _attn_body(q_in, k_in, v_in, o_ref,
               acc_scr, m_scr, l_scr, stage_q, stage_k, stage_v,
               q_ref, k_ref, v_ref, qf_scr, kf_scr, vf_scr,
               kp_ref, vp_ref, kpf_scr, vpf_scr, *, group):
    jt = pl.program_id(2)
    blk = ATTN_BLOCK
    ones_v = jnp.ones((2 * blk, HEAD_DIM), BF16)

    def lanes(g):
        return slice(g * HEAD_DIM, (g + 1) * HEAD_DIM)

    @pl.when(jt == 0)
    def _():
        kp_ref[...] = jnp.zeros_like(kp_ref)
        vp_ref[...] = jnp.zeros_like(vp_ref)
        kpf_scr[...] = jnp.zeros_like(kpf_scr)
        vpf_scr[...] = jnp.zeros_like(vpf_scr)

    for src, stage, dst_bf16, dst_f32 in ((q_in, stage_q, q_ref, qf_scr),
                                          (k_in, stage_k, k_ref, kf_scr),
                                          (v_in, stage_v, v_ref, vf_scr)):
        for i in range(blk):
            stage[STAGE_PITCH * i:STAGE_PITCH * i + ATTN_GROUPS, :] = (
                src[ATTN_GROUPS * i:ATTN_GROUPS * (i + 1), :].astype(F32))
        for g in range(ATTN_GROUPS):
            cls = stage[pl.ds(g, blk, stride=STAGE_PITCH), :]
            dst_f32[:, lanes(g)] = cls
            dst_bf16[:, lanes(g)] = cls.astype(BF16)

    n_pat = len(ATTN_PATTERNS)
    for pi, (window, dil) in enumerate(reversed(ATTN_PATTERNS)):
        span = window // dil
        n_slab = ATTN_GROUPS // dil
        slab = blk // n_slab
        use_f32 = slab % BF16_SUBLANES != 0
        band, col = _band_mask(slab, n_slab, span)
        band_first = jnp.logical_and(band, jnp.logical_or(col >= blk, jt > 0))

        def gather(cur_ref, cur_f32, prev_ref, prev_f32, res, kb, prev):
            if not prev:
                src, rows = (cur_f32 if use_f32 else cur_ref), slice(slab * kb, slab * (kb + 1))
            elif kb > 0:
                src, rows = (cur_f32 if use_f32 else cur_ref), slice(slab * (kb - 1), slab * kb)
            elif use_f32:
                src, rows = prev_f32, slice(F32_SUBLANES - slab, F32_SUBLANES)
            else:
                src, rows = prev_ref, slice(blk - slab, blk)
            parts = [src[rows, lanes(u * dil + res)] for u in range(n_slab)]
            out = parts[0] if n_slab == 1 else jnp.concatenate(parts, axis=0)
            return out.astype(BF16)

        def st_load(scr, res, kb):
            rows = slice(slab * kb, slab * (kb + 1))
            parts = [scr[rows, lanes(u * dil + res)] for u in range(n_slab)]
            return parts[0] if n_slab == 1 else jnp.concatenate(parts, axis=0)

        def st_store(scr, res, kb, val):
            rows = slice(slab * kb, slab * (kb + 1))
            for u in range(n_slab):
                scr[rows, lanes(u * dil + res)] = val[u * slab:(u + 1) * slab, :]

        blocks = [(res, kb) for res in range(dil) for kb in range(n_slab)]
        for g0 in range(0, len(blocks), group):
            grp = blocks[g0:g0 + group]
            scores = []
            for res, kb in grp:
                q = gather(q_ref, qf_scr, None, None, res, kb, False)
                kw = jnp.concatenate([gather(k_ref, kf_scr, kp_ref, kpf_scr, res, kb, True),
                                      gather(k_ref, kf_scr, kp_ref, kpf_scr, res, kb, False)], axis=0)
                s = lax.dot_general(q, kw, (((1,), (1,)), ((), ())), preferred_element_type=F32)
                scores.append(jnp.where(band_first if kb == 0 else band, s, MASK_VALUE))
            probs = []
            for s in scores:
                m = jnp.max(s, axis=-1, keepdims=True)
                probs.append((jnp.exp2(s - m).astype(BF16), m))
            pvs = []
            for (res, kb), (p, _) in zip(grp, probs):
                vw = jnp.concatenate([gather(v_ref, vf_scr, vp_ref, vpf_scr, res, kb, True),
                                      gather(v_ref, vf_scr, vp_ref, vpf_scr, res, kb, False)], axis=0)
                pvs.append(jnp.dot(p, jnp.concatenate([vw, ones_v], axis=1), preferred_element_type=F32))
            for (res, kb), (_, m), pv in zip(grp, probs, pvs):
                num, den = pv[:, :HEAD_DIM], pv[:, HEAD_DIM:]
                m_new = jnp.broadcast_to(m, (blk, HEAD_DIM))
                if pi > 0:
                    m_old = st_load(m_scr, res, kb)
                    m_new = jnp.maximum(m_old, m_new)
                    w_old = jnp.exp2(m_old - m_new)
                    w_new = jnp.exp2(m - m_new)
                    num = st_load(acc_scr, res, kb) * w_old + num * w_new
                    den = st_load(l_scr, res, kb) * w_old + den * w_new
                if pi < n_pat - 1:
                    st_store(acc_scr, res, kb, num)
                    st_store(m_scr, res, kb, m_new)
                    st_store(l_scr, res, kb, den)
                else:
                    st_store(acc_scr, res, kb, num / den)

    for g in range(ATTN_GROUPS):
        stage_q[pl.ds(g, blk, stride=STAGE_PITCH), :] = acc_scr[:, lanes(g)]
    for i in range(blk):
        o_ref[ATTN_GROUPS * i:ATTN_GROUPS * (i + 1), :] = (
            stage_q[STAGE_PITCH * i:STAGE_PITCH * i + ATTN_GROUPS, :].astype(BF16))

    kp_ref[...] = k_ref[...]
    vp_ref[...] = v_ref[...]
    kpf_scr[...] = kf_scr[blk - F32_SUBLANES:, :]
    vpf_scr[...] = vf_scr[blk - F32_SUBLANES:, :]


def _dilated_attention(att, batch, seq, group=8):
    assert seq % ATTN_TILE == 0
    n_tile = seq // ATTN_TILE

    def spec(which):
        return pl.BlockSpec((None, None, ATTN_TILE, HEAD_DIM),
                            lambda bi, hd, jt: (which, hd, bi * n_tile + jt, 0))

    tile_f32 = pltpu.VMEM((ATTN_BLOCK, TILE_LANES), F32)
    tile_bf16 = pltpu.VMEM((ATTN_BLOCK, TILE_LANES), BF16)
    tail_f32 = pltpu.VMEM((F32_SUBLANES, TILE_LANES), F32)
    stage = pltpu.VMEM((ATTN_BLOCK * STAGE_PITCH, HEAD_DIM), F32)
    return pl.pallas_call(
        functools.partial(_attn_body, group=group),
        grid=(batch, ATTN_HEADS, n_tile),
        in_specs=[spec(0), spec(1), spec(2)],
        out_specs=pl.BlockSpec((None, ATTN_TILE, HEAD_DIM), lambda bi, hd, jt: (hd, bi * n_tile + jt, 0)),
        out_shape=jax.ShapeDtypeStruct((ATTN_HEADS, batch * seq, HEAD_DIM), BF16),
        scratch_shapes=[tile_f32, tile_f32, tile_f32,
                        stage, stage, stage,
                        tile_bf16, tile_bf16, tile_bf16,
                        tile_f32, tile_f32, tile_f32,
                        tile_bf16, tile_bf16,
                        tail_f32, tail_f32],
        compiler_params=pltpu.CompilerParams(
            dimension_semantics=("arbitrary", "arbitrary", "arbitrary"),
            vmem_limit_bytes=VMEM_LIMIT_BYTES),
        name="dilated_attn",
    )(att, att, att)


CONV_HALO = 8


def _mlstm_body(gt_ref, q_ref, k_ref, qp_ref, kp_ref, v_ref, mo_ref, cw_ref, cb_ref, ng_ref,
                o_ref, cn_scr, m_scr, qext_scr, kext_scr):
    c = pl.program_id(1)
    L = MLSTM_CHUNK
    W = MLSTM_WIDTH

    @pl.when(c == 0)
    def _():
        cn_scr[...] = jnp.zeros_like(cn_scr)
        m_scr[...] = jnp.full_like(m_scr, M_INIT)

    has_prev = c > 0

    def conv_silu(cur_ref, prev_ref, ext, col0):
        ext[:CONV_HALO, :] = jnp.where(has_prev, prev_ref[...], 0.0)
        ext[CONV_HALO:, :] = cur_ref[...]
        y = cb_ref[:, col0:col0 + W]
        for s in range(CONV_WIDTH):
            tap = cw_ref[CONV_WIDTH - 1 - s:CONV_WIDTH - s, col0:col0 + W]
            y = y + ext[CONV_HALO - s:CONV_HALO - s + L, :] * tap
        return y * jax.nn.sigmoid(y)

    qf = conv_silu(q_ref, qp_ref, qext_scr, 0)
    kf = conv_silu(k_ref, kp_ref, kext_scr, W) * (HEAD_DIM ** -0.5)
    qb = qf.astype(BF16)

    gt = gt_ref[...]
    ig = gt[:MLSTM_HEADS]
    fg = gt[MLSTM_HEADS:]
    lf = jnp.minimum(fg, 0.0) - jnp.log1p(jnp.exp(-jnp.abs(fg)))
    lane = lax.broadcasted_iota(jnp.int32, (MLSTM_HEADS, L), 1)
    b = lf
    s = 1
    while s < L:
        b = b + jnp.where(lane >= s, pltpu.roll(b, s, 1), 0.0)
        s *= 2
    c_row = ig - b
    b_last = b[:, L - 1:L]
    c_max = jnp.max(c_row, axis=1, keepdims=True)
    m_prev = m_scr[:, :1]
    m_chunk = b_last + c_max
    m_new = jnp.maximum(b_last + m_prev, m_chunk)
    decay = jnp.exp(b_last + m_prev - m_new)
    scale = jnp.exp(m_chunk - m_new)
    wa = jnp.exp(c_row - c_max)
    m_scr[...] = jnp.broadcast_to(m_new, m_scr.shape)

    row = lax.broadcasted_iota(jnp.int32, (L, L), 0)
    col = lax.broadcasted_iota(jnp.int32, (L, L), 1)
    causal = col <= row
    ones_v = jnp.ones((L, HEAD_DIM), BF16)

    heads = range(MLSTM_HEADS)

    def hs(hd):
        return slice(hd * HEAD_DIM, (hd + 1) * HEAD_DIM)

    def one(hd):
        return slice(hd, hd + 1)

    k_ts, states, s_qks, inters = [], [], [], []
    for hd in heads:
        q_h = qb[:, hs(hd)]
        k_t = kf[:, hs(hd)].T
        state = cn_scr[hd]
        k_ts.append(k_t)
        states.append(state)
        s_qks.append(jnp.dot(q_h, k_t.astype(BF16), preferred_element_type=F32))
        inters.append(jnp.dot(q_h, state.astype(BF16), preferred_element_type=F32))

    ws, mus, gs = [], [], []
    for hd in heads:
        c_b = jnp.where(causal, jnp.broadcast_to(c_row[one(hd), :], (L, L)), MASK_VALUE)
        mu = jnp.maximum(jnp.max(c_b, axis=-1, keepdims=True), m_prev[one(hd), :])
        ws.append(jnp.exp(c_b - mu) * s_qks[hd])
        mus.append(mu)
        gs.append(jnp.exp(m_prev[one(hd), :] - mu))

    ones_sq = jnp.ones((L, L), BF16)

    def split(x):
        hi = x.astype(BF16)
        return hi, (x - hi.astype(F32)).astype(BF16)

    lf_rows = jnp.concatenate([jnp.broadcast_to(lf[one(hd), :], (HEAD_DIM, L)) for hd in heads], axis=0)
    causal_b = jnp.where(causal, 1.0, 0.0).astype(BF16)
    nt = (((1,), (1,)), ((), ()))
    lf_hi, lf_lo = split(lf_rows)
    b_all = (lax.dot_general(causal_b, lf_hi, nt, preferred_element_type=F32)
             + lax.dot_general(causal_b, lf_lo, nt, preferred_element_type=F32))

    intras, kvns, sum_los = [], [], []
    for hd in heads:
        v_aug = jnp.concatenate([v_ref[:, hs(hd)], ones_v], axis=1)
        w_hi, w_lo = split(ws[hd])
        intras.append(jnp.dot(w_hi, v_aug, preferred_element_type=F32))
        sum_los.append(jnp.dot(w_lo, ones_sq, preferred_element_type=F32))
        a_t = (k_ts[hd] * jnp.broadcast_to(wa[one(hd), :], (HEAD_DIM, L))).astype(BF16)
        kvns.append(jnp.dot(a_t, v_aug, preferred_element_type=F32))

    cells = []
    for hd in heads:
        g, mu = gs[hd], mus[hd]
        num = g * inters[hd][:, :HEAD_DIM] + intras[hd][:, :HEAD_DIM]
        den = g * inters[hd][:, HEAD_DIM:] + intras[hd][:, HEAD_DIM:] + sum_los[hd]
        cells.append(num / jnp.maximum(jnp.abs(den), jnp.exp(-(b_all[:, hs(hd)] + mu))))

    sq_sums = []
    for hd in heads:
        sq = (cells[hd] * cells[hd]).astype(BF16)
        sq_sums.append(jnp.dot(sq, ones_sq, preferred_element_type=F32))
    for hd in heads:
        y = cells[hd] * lax.rsqrt(sq_sums[hd] * (1.0 / HEAD_DIM) + EPS) * ng_ref[:, hs(hd)]
        o_ref[:, hs(hd)] = (jax.nn.sigmoid(mo_ref[:, hs(hd)]) * y).astype(BF16)
        cn_scr[hd] = decay[one(hd), :] * states[hd] + scale[one(hd), :] * kvns[hd]


def _mlstm(gates_t, m_f32, m_v, conv_w, conv_b, norm_g, batch, seq):
    t = batch * seq
    L = MLSTM_CHUNK
    W = MLSTM_WIDTH
    nc = seq // L
    halo_per_chunk = L // CONV_HALO

    def rows(bi, c):
        return bi * nc + c

    def halo(bi, c):
        return jnp.maximum(rows(bi, c) * halo_per_chunk - 1, 0)

    return pl.pallas_call(
        _mlstm_body,
        grid=(batch, nc),
        in_specs=[
            pl.BlockSpec((N_GATE, L), lambda bi, c: (0, rows(bi, c))),
            pl.BlockSpec((L, W), lambda bi, c: (rows(bi, c), 0)),
            pl.BlockSpec((L, W), lambda bi, c: (rows(bi, c), 1)),
            pl.BlockSpec((CONV_HALO, W), lambda bi, c: (halo(bi, c), 0)),
            pl.BlockSpec((CONV_HALO, W), lambda bi, c: (halo(bi, c), 1)),
            pl.BlockSpec((L, W), lambda bi, c: (rows(bi, c), 0)),
            pl.BlockSpec((L, W), lambda bi, c: (rows(bi, c), 2)),
            pl.BlockSpec((CONV_WIDTH, 2 * W), lambda bi, c: (0, 0)),
            pl.BlockSpec((1, 2 * W), lambda bi, c: (0, 0)),
            pl.BlockSpec((1, W), lambda bi, c: (0, 0)),
        ],
        out_specs=pl.BlockSpec((L, W), lambda bi, c: (rows(bi, c), 0)),
        out_shape=jax.ShapeDtypeStruct((t, W), BF16),
        scratch_shapes=[pltpu.VMEM((MLSTM_HEADS, HEAD_DIM, 2 * HEAD_DIM), F32),
                        pltpu.VMEM((MLSTM_HEADS, LANES), F32),
                        pltpu.VMEM((CONV_HALO + L, W), F32),
                        pltpu.VMEM((CONV_HALO + L, W), F32)],
        compiler_params=pltpu.CompilerParams(
            dimension_semantics=("arbitrary", "arbitrary"),
            vmem_limit_bytes=VMEM_LIMIT_BYTES),
        name="mlstm",
    )(gates_t, m_f32, m_f32, m_f32, m_f32, m_v, m_f32, conv_w, conv_b, norm_g)


def kernel(x, norm_mix_g, w_in, conv_w, conv_b, gate_b, q_norm_g, k_norm_g, mlstm_norm_g,
           w_out, norm_ffn_g, w_gate, w_up, w_down):
    B, S, _ = x.shape
    t = B * S
    layer = 0
    x2 = x.reshape(t, D_MODEL)

    w = w_in[layer]
    a3 = 3 * ATTN_WIDTH
    mw = MLSTM_WIDTH
    w_main = w[:, :a3 + 4 * mw].astype(BF16)
    w_gate_t = w[:, a3 + 4 * mw:].T
    b_gate = gate_b[layer].reshape(N_GATE, 1)
    head_gain = jnp.stack([
        jnp.tile(q_norm_g[layer] * (HEAD_DIM ** -0.5 * LOG2_E), ATTN_HEADS),
        jnp.tile(k_norm_g[layer], ATTN_HEADS)]).reshape(2, 1, ATTN_WIDTH)

    att, m_f32, m_v, gates_t, wo_b, wg_b, wu_b, wd_b = _in_proj(
        x2, norm_mix_g[layer].reshape(1, D_MODEL), w_main, w_gate_t, b_gate, head_gain,
        (w_out[layer], w_gate[layer], w_up[layer], w_down[layer]), tm=1024)

    attn_out = _dilated_attention(att, B, S)

    mlstm_out = _mlstm(gates_t, m_f32, m_v, conv_w[layer], conv_b[layer].reshape(1, 2 * mw),
                       mlstm_norm_g[layer].reshape(1, mw), B, S)

    x1, h2 = _out_proj(attn_out, mlstm_out, x2, wo_b, norm_ffn_g[layer].reshape(1, D_MODEL), tm=512)

    out = _ffn(h2, wg_b, wu_b, wd_b, x1, tm=1024, th=512)
    return out.reshape(B, S, D_MODEL)
```

```python
import functools

import jax
import jax.numpy as jnp
from jax import lax
from jax.experimental import pallas as pl
from jax.experimental.pallas import tpu as pltpu

D_MODEL = 2048
HEAD_DIM = 128
ATTN_HEADS = 8
MLSTM_HEADS = 8
ATTN_WIDTH = ATTN_HEADS * HEAD_DIM
MLSTM_WIDTH = MLSTM_HEADS * HEAD_DIM
ATTN_PATTERNS = ((128, 1), (512, 4), (2048, 16))
ATTN_BLOCK = 128
MLSTM_CHUNK = 128
CONV_WIDTH = 4
FFN_HIDDEN = 5632
EPS = 1e-6
MASK_VALUE = -1e30
M_INIT = -1e30

LOG2_E = 1.4426950408889634
N_GATE = 2 * MLSTM_HEADS
LANES = 128
V7X_VMEM_BYTES = 64 * 1024 * 1024
VMEM_LIMIT_BYTES = V7X_VMEM_BYTES - 6 * 1024 * 1024

F32 = jnp.float32
BF16 = jnp.bfloat16


def _in_proj_body(x_ref, g_ref, w_ref, wg_ref, bg_ref, hg_ref, *rest, n_cast):
    cast_in, rest = rest[:n_cast], rest[n_cast:]
    oatt_ref, omf_ref, omv_ref, ogate_ref = rest[:4]
    cast_out, h_scr = rest[4:4 + n_cast], rest[4 + n_cast]
    j = pl.program_id(1)

    for src, dst in zip(cast_in, cast_out):
        dst[...] = src[...].astype(BF16)

    def project(h=None):
        h = h_scr[...] if h is None else h
        return jnp.dot(h, w_ref[...], preferred_element_type=F32)

    def store_head_normed(acc):
        for hd in range(ATTN_HEADS):
            sl = slice(hd * HEAD_DIM, (hd + 1) * HEAD_DIM)
            a = acc[:, sl]
            ms = jnp.mean(a * a, axis=-1, keepdims=True)
            oatt_ref[hd] = (a * lax.rsqrt(ms + EPS) * hg_ref[:, sl]).astype(BF16)

    @pl.when(j == 0)
    def _():
        x = x_ref[...]
        ms = jnp.mean(x * x, axis=-1, keepdims=True)
        hb = (x * lax.rsqrt(ms + EPS) * g_ref[...]).astype(BF16)
        h_scr[...] = hb
        wg_t = wg_ref[...].T[:N_GATE, :].astype(BF16)
        ogate_ref[...] = lax.dot_general(wg_t, hb, (((1,), (1,)), ((), ())),
                                         preferred_element_type=F32) + bg_ref[...]
        store_head_normed(project(hb))

    @pl.when(j == 1)
    def _():
        store_head_normed(project())

    @pl.when(j == 2)
    def _():
        acc = project()
        for hd in range(ATTN_HEADS):
            oatt_ref[hd] = acc[:, hd * HEAD_DIM:(hd + 1) * HEAD_DIM].astype(BF16)

    @pl.when(jnp.logical_and(j >= 3, j < 6))
    def _():
        omf_ref[...] = project()

    @pl.when(j == 6)
    def _():
        omv_ref[...] = project().astype(BF16)


def _in_proj(x2, g_mix, w_main, w_gate_t, b_gate, head_gain, side_weights, tm):
    t = x2.shape[0]
    tn = ATTN_WIDTH
    n_col = w_main.shape[1] // tn
    n_step = (t // tm) * n_col

    def w_col(j):
        return jnp.where(j == 5, 6, jnp.where(j == 6, 5, j))

    cast_specs, cast_shapes = [], []
    for wt in side_weights:
        rows, cols = wt.shape
        n_blk = max(n for n in range(1, n_step + 1)
                    if rows % n == 0 and (rows // n) % BF16_SUBLANES == 0)
        cast_specs.append(pl.BlockSpec(
            (rows // n_blk, cols), lambda i, j, n_blk=n_blk: (jnp.minimum(i * n_col + j, n_blk - 1), 0)))
        cast_shapes.append(jax.ShapeDtypeStruct(wt.shape, BF16))

    return pl.pallas_call(
        functools.partial(_in_proj_body, n_cast=len(side_weights)),
        grid=(t // tm, n_col),
        in_specs=[
            pl.BlockSpec((tm, D_MODEL), lambda i, j: (i, 0)),
            pl.BlockSpec((1, D_MODEL), lambda i, j: (0, 0)),
            pl.BlockSpec((D_MODEL, tn), lambda i, j: (0, w_col(j))),
            pl.BlockSpec((D_MODEL, LANES), lambda i, j: (0, 0), pipeline_mode=pl.Buffered(1)),
            pl.BlockSpec((N_GATE, 1), lambda i, j: (0, 0)),
            pl.BlockSpec((None, 1, tn), lambda i, j: (jnp.minimum(j, 1), 0, 0)),
        ] + cast_specs,
        out_specs=[
            pl.BlockSpec((None, ATTN_HEADS, tm, HEAD_DIM), lambda i, j: (jnp.minimum(j, 2), 0, i, 0)),
            pl.BlockSpec((tm, tn), lambda i, j: (i, jnp.clip(j - 3, 0, 2))),
            pl.BlockSpec((tm, tn), lambda i, j: (i, 0)),
            pl.BlockSpec((N_GATE, tm), lambda i, j: (0, i)),
        ] + cast_specs,
        out_shape=[
            jax.ShapeDtypeStruct((3, ATTN_HEADS, t, HEAD_DIM), BF16),
            jax.ShapeDtypeStruct((t, 3 * tn), F32),
            jax.ShapeDtypeStruct((t, tn), BF16),
            jax.ShapeDtypeStruct((N_GATE, t), F32),
        ] + cast_shapes,
        scratch_shapes=[pltpu.VMEM((tm, D_MODEL), BF16)],
        compiler_params=pltpu.CompilerParams(
            dimension_semantics=("arbitrary", "arbitrary"),
            vmem_limit_bytes=VMEM_LIMIT_BYTES),
        name="in_proj",
    )(x2, g_mix, w_main, w_gate_t, b_gate, head_gain, *side_weights)


FFN_DOWN_CHUNK = 512


def _ffn_body(h_ref, wg_ref, wu_ref, wd_ref, x1_ref, o_ref, *, n_res):
    c = pl.program_id(1)

    @pl.when(c == 0)
    def _():
        o_ref[...] = jnp.zeros_like(o_ref)

    @pl.when(c < n_res)
    def _():
        slab = x1_ref.shape[0]
        rows = pl.ds(pl.multiple_of(c * slab, slab), slab)
        o_ref[rows, :] += x1_ref[...]

    h = h_ref[...]
    g = jnp.dot(h, wg_ref[...], preferred_element_type=F32)
    u = jnp.dot(h, wu_ref[...], preferred_element_type=F32)
    a = (g * jax.nn.sigmoid(g) * u).astype(BF16)
    for n0 in range(0, D_MODEL, FFN_DOWN_CHUNK):
        cols = slice(n0, n0 + FFN_DOWN_CHUNK)
        o_ref[:, cols] += jnp.dot(a, wd_ref[:, cols], preferred_element_type=F32)


def _ffn(h2, w_g, w_u, w_d, x1, tm, th):
    t = h2.shape[0]
    n_chunk = FFN_HIDDEN // th
    n_res = 8
    assert n_res <= n_chunk and tm % n_res == 0
    return pl.pallas_call(
        functools.partial(_ffn_body, n_res=n_res),
        grid=(t // tm, n_chunk),
        in_specs=[
            pl.BlockSpec((tm, D_MODEL), lambda i, c: (i, 0)),
            pl.BlockSpec((D_MODEL, th), lambda i, c: (0, c)),
            pl.BlockSpec((D_MODEL, th), lambda i, c: (0, c)),
            pl.BlockSpec((th, D_MODEL), lambda i, c: (c, 0)),
            pl.BlockSpec((tm // n_res, D_MODEL), lambda i, c: (i * n_res + jnp.minimum(c, n_res - 1), 0)),
        ],
        out_specs=pl.BlockSpec((tm, D_MODEL), lambda i, c: (i, 0)),
        out_shape=jax.ShapeDtypeStruct((t, D_MODEL), F32),
        compiler_params=pltpu.CompilerParams(
            dimension_semantics=("arbitrary", "arbitrary"),
            vmem_limit_bytes=VMEM_LIMIT_BYTES),
        name="ffn",
    )(h2, w_g, w_u, w_d, x1)


ATTN_GROUPS = 16
ATTN_TILE = ATTN_BLOCK * ATTN_GROUPS
TILE_LANES = ATTN_GROUPS * HEAD_DIM
F32_SUBLANES = 8
BF16_SUBLANES = 16
STAGE_PITCH = 24


def _band_mask(slab, n_slab, span):
    blk = ATTN_BLOCK
    shift = slab.bit_length() - 1
    row = lax.broadcasted_iota(jnp.int32, (blk, 2 * blk), 0)
    col = lax.broadcasted_iota(jnp.int32, (blk, 2 * blk), 1)

    def pos(r):
        return (r & (slab - 1)) * n_slab + lax.shift_right_logical(r, shift)

    k_rel = pos(col & (blk - 1)) - jnp.where(col < blk, blk, 0)
    dist = pos(row) - k_rel
    return jnp.logical_and(dist >= 0, dist <= span), col


def _attn_body(q_in, k_in, v_in, o_ref,
               acc_scr, m_scr, l_scr, stage_q, stage_k, stage_v,
               q_ref, k_ref, v_ref, qf_scr, kf_scr, vf_scr,
               kp_ref, vp_ref, kpf_scr, vpf_scr, *, group):
    jt = pl.program_id(2)
    blk = ATTN_BLOCK
    ones_v = jnp.ones((2 * blk, HEAD_DIM), BF16)

    def lanes(g):
        return slice(g * HEAD_DIM, (g + 1) * HEAD_DIM)

    @pl.when(jt == 0)
    def _():
        kp_ref[...] = jnp.zeros_like(kp_ref)
        vp_ref[...] = jnp.zeros_like(vp_ref)
        kpf_scr[...] = jnp.zeros_like(kpf_scr)
        vpf_scr[...] = jnp.zeros_like(vpf_scr)

    for src, stage, dst_bf16, dst_f32 in ((q_in, stage_q, q_ref, qf_scr),
                                          (k_in, stage_k, k_ref, kf_scr),
                                          (v_in, stage_v, v_ref, vf_scr)):
        for i in range(blk):
            stage[STAGE_PITCH * i:STAGE_PITCH * i + ATTN_GROUPS, :] = (
                src[ATTN_GROUPS * i:ATTN_GROUPS * (i + 1), :].astype(F32))
        for g in range(ATTN_GROUPS):
            cls = stage[pl.ds(g, blk, stride=STAGE_PITCH), :]
            dst_f32[:, lanes(g)] = cls
            dst_bf16[:, lanes(g)] = cls.astype(BF16)

    n_pat = len(ATTN_PATTERNS)
    for pi, (window, dil) in enumerate(reversed(ATTN_PATTERNS)):
        span = window // dil
        n_slab = ATTN_GROUPS // dil
        slab = blk // n_slab
        use_f32 = slab % BF16_SUBLANES != 0
        band, col = _band_mask(slab, n_slab, span)
        band_first = jnp.logical_and(band, jnp.logical_or(col >= blk, jt > 0))

        def gather(cur_ref, cur_f32, prev_ref, prev_f32, res, kb, prev):
            if not prev:
                src, rows = (cur_f32 if use_f32 else cur_ref), slice(slab * kb, slab * (kb + 1))
            elif kb > 0:
                src, rows = (cur_f32 if use_f32 else cur_ref), slice(slab * (kb - 1), slab * kb)
            elif use_f32:
                src, rows = prev_f32, slice(F32_SUBLANES - slab, F32_SUBLANES)
            else:
                src, rows = prev_ref, slice(blk - slab, blk)
            parts = [src[rows, lanes(u * dil + res)] for u in range(n_slab)]
            out = parts[0] if n_slab == 1 else jnp.concatenate(parts, axis=0)
            return out.astype(BF16)

        def st_load(scr, res, kb):
            rows = slice(slab * kb, slab * (kb + 1))
            parts = [scr[rows, lanes(u * dil + res)] for u in range(n_slab)]
            return parts[0] if n_slab == 1 else jnp.concatenate(parts, axis=0)

        def st_store(scr, res, kb, val):
            rows = slice(slab * kb, slab * (kb + 1))
            for u in range(n_slab):
                scr[rows, lanes(u * dil + res)] = val[u * slab:(u + 1) * slab, :]

        blocks = [(res, kb) for res in range(dil) for kb in range(n_slab)]
        for g0 in range(0, len(blocks), group):
            grp = blocks[g0:g0 + group]
            scores = []
            for res, kb in grp:
                q = gather(q_ref, qf_scr, None, None, res, kb, False)
                kw = jnp.concatenate([gather(k_ref, kf_scr, kp_ref, kpf_scr, res, kb, True),
                                      gather(k_ref, kf_scr, kp_ref, kpf_scr, res, kb, False)], axis=0)
                s = lax.dot_general(q, kw, (((1,), (1,)), ((), ())), preferred_element_type=F32)
                scores.append(jnp.where(band_first if kb == 0 else band, s, MASK_VALUE))
            probs = []
            for s in scores:
                m = jnp.max(s, axis=-1, keepdims=True)
                probs.append((jnp.exp2(s - m).astype(BF16), m))
            pvs = []
            for (res, kb), (p, _) in zip(grp, probs):
                vw = jnp.concatenate([gather(v_ref, vf_scr, vp_ref, vpf_scr, res, kb, True),
                                      gather(v_ref, vf_scr, vp_ref, vpf_scr, res, kb, False)], axis=0)
                pvs.append(jnp.dot(p, jnp.concatenate([vw, ones_v], axis=1), preferred_element_type=F32))
            for (res, kb), (_, m), pv in zip(grp, probs, pvs):
                num, den = pv[:, :HEAD_DIM], pv[:, HEAD_DIM:]
                m_new = jnp.broadcast_to(m, (blk, HEAD_DIM))
                if pi > 0:
                    m_old = st_load(m_scr, res, kb)
                    m_new = jnp.maximum(m_old, m_new)
                    w_old = jnp.exp2(m_old - m_new)
                    w_new = jnp.exp2(m - m_new)
                    num = st_load(acc_scr, res, kb) * w_old + num * w_new
                    den = st_load(l_scr, res, kb) * w_old + den * w_new
                if pi < n_pat - 1:
                    st_store(acc_scr, res, kb, num)
                    st_store(m_scr, res, kb, m_new)
                    st_store(l_scr, res, kb, den)
                else:
                    st_store(acc_scr, res, kb, num / den)

    for g in range(ATTN_GROUPS):
        stage_q[pl.ds(g, blk, stride=STAGE_PITCH), :] = acc_scr[:, lanes(g)]
    for i in range(blk):
        o_ref[ATTN_GROUPS * i:ATTN_GROUPS * (i + 1), :] = (
            stage_q[STAGE_PITCH * i:STAGE_PITCH * i + ATTN_GROUPS, :].astype(BF16))

    kp_ref[...] = k_ref[...]
    vp_ref[...] = v_ref[...]
    kpf_scr[...] = kf_scr[blk - F32_SUBLANES:, :]
    vpf_scr[...] = vf_scr[blk - F32_SUBLANES:, :]


def _dilated_attention(att, batch, seq, group=8):
    assert seq % ATTN_TILE == 0
    n_tile = seq // ATTN_TILE

    def spec(which):
        return pl.BlockSpec((None, None, ATTN_TILE, HEAD_DIM),
                            lambda bi, hd, jt: (which, hd, bi * n_tile + jt, 0))

    tile_f32 = pltpu.VMEM((ATTN_BLOCK, TILE_LANES), F32)
    tile_bf16 = pltpu.VMEM((ATTN_BLOCK, TILE_LANES), BF16)
    tail_f32 = pltpu.VMEM((F32_SUBLANES, TILE_LANES), F32)
    stage = pltpu.VMEM((ATTN_BLOCK * STAGE_PITCH, HEAD_DIM), F32)
    return pl.pallas_call(
        functools.partial(_attn_body, group=group),
        grid=(batch, ATTN_HEADS, n_tile),
        in_specs=[spec(0), spec(1), spec(2)],
        out_specs=pl.BlockSpec((None, ATTN_TILE, HEAD_DIM), lambda bi, hd, jt: (hd, bi * n_tile + jt, 0)),
        out_shape=jax.ShapeDtypeStruct((ATTN_HEADS, batch * seq, HEAD_DIM), BF16),
        scratch_shapes=[tile_f32, tile_f32, tile_f32,
                        stage, stage, stage,
                        tile_bf16, tile_bf16, tile_bf16,
                        tile_f32, tile_f32, tile_f32,
                        tile_bf16, tile_bf16,
                        tail_f32, tail_f32],
        compiler_params=pltpu.CompilerParams(
            dimension_semantics=("arbitrary", "arbitrary", "arbitrary"),
            vmem_limit_bytes=VMEM_LIMIT_BYTES),
        name="dilated_attn",
    )(att, att, att)


CONV_HALO = 8


def _mlstm_chunk(gt, qb, kf, v_ref, mo_ref, ng_ref, o_ref, rows, cn_scr, m_scr):
    L = MLSTM_CHUNK

    ig = gt[:MLSTM_HEADS]
    fg = gt[MLSTM_HEADS:]
    lf = jnp.minimum(fg, 0.0) - jnp.log1p(jnp.exp(-jnp.abs(fg)))
    lane = lax.broadcasted_iota(jnp.int32, (MLSTM_HEADS, L), 1)
    b = lf
    s = 1
    while s < L:
        b = b + jnp.where(lane >= s, pltpu.roll(b, s, 1), 0.0)
        s *= 2
    c_row = ig - b
    b_last = b[:, L - 1:L]
    c_max = jnp.max(c_row, axis=1, keepdims=True)
    m_prev = m_scr[:, :1]
    m_chunk = b_last + c_max
    m_new = jnp.maximum(b_last + m_prev, m_chunk)
    decay = jnp.exp(b_last + m_prev - m_new)
    scale = jnp.exp(m_chunk - m_new)
    wa = jnp.exp(c_row - c_max)
    m_scr[...] = jnp.broadcast_to(m_new, m_scr.shape)

    row = lax.broadcasted_iota(jnp.int32, (L, L), 0)
    col = lax.broadcasted_iota(jnp.int32, (L, L), 1)
    causal = col <= row
    ones_v = jnp.ones((L, HEAD_DIM), BF16)

    heads = range(MLSTM_HEADS)

    def hs(hd):
        return slice(hd * HEAD_DIM, (hd + 1) * HEAD_DIM)

    def one(hd):
        return slice(hd, hd + 1)

    k_ts, states, s_qks, inters = [], [], [], []
    for hd in heads:
        q_h = qb[:, hs(hd)]
        k_t = kf[:, hs(hd)].T
        state = cn_scr[hd]
        k_ts.append(k_t)
        states.append(state)
        s_qks.append(jnp.dot(q_h, k_t.astype(BF16), preferred_element_type=F32))
        inters.append(jnp.dot(q_h, state.astype(BF16), preferred_element_type=F32))

    ws, mus, gs = [], [], []
    for hd in heads:
        c_b = jnp.where(causal, jnp.broadcast_to(c_row[one(hd), :], (L, L)), MASK_VALUE)
        mu = jnp.maximum(jnp.max(c_b, axis=-1, keepdims=True), m_prev[one(hd), :])
        ws.append(jnp.exp(c_b - mu) * s_qks[hd])
        mus.append(mu)
        gs.append(jnp.exp(m_prev[one(hd), :] - mu))

    ones_sq = jnp.ones((L, L), BF16)

    def split(x):
        hi = x.astype(BF16)
        return hi, (x - hi.astype(F32)).astype(BF16)

    lf_rows = jnp.concatenate([jnp.broadcast_to(lf[one(hd), :], (HEAD_DIM, L)) for hd in heads], axis=0)
    causal_b = jnp.where(causal, 1.0, 0.0).astype(BF16)
    nt = (((1,), (1,)), ((), ()))
    lf_hi, lf_lo = split(lf_rows)
    b_all = (lax.dot_general(causal_b, lf_hi, nt, preferred_element_type=F32)
             + lax.dot_general(causal_b, lf_lo, nt, preferred_element_type=F32))

    intras, kvns, sum_los = [], [], []
    for hd in heads:
        v_aug = jnp.concatenate([v_ref[rows, hs(hd)], ones_v], axis=1)
        w_hi, w_lo = split(ws[hd])
        intras.append(jnp.dot(w_hi, v_aug, preferred_element_type=F32))
        sum_los.append(jnp.dot(w_lo, ones_sq, preferred_element_type=F32))
        a_t = (k_ts[hd] * jnp.broadcast_to(wa[one(hd), :], (HEAD_DIM, L))).astype(BF16)
        kvns.append(jnp.dot(a_t, v_aug, preferred_element_type=F32))

    cells = []
    for hd in heads:
        g, mu = gs[hd], mus[hd]
        num = g * inters[hd][:, :HEAD_DIM] + intras[hd][:, :HEAD_DIM]
        den = g * inters[hd][:, HEAD_DIM:] + intras[hd][:, HEAD_DIM:] + sum_los[hd]
        cells.append(num / jnp.maximum(jnp.abs(den), jnp.exp(-(b_all[:, hs(hd)] + mu))))

    sq_sums = []
    for hd in heads:
        sq = (cells[hd] * cells[hd]).astype(BF16)
        sq_sums.append(jnp.dot(sq, ones_sq, preferred_element_type=F32))
    for hd in heads:
        y = cells[hd] * lax.rsqrt(sq_sums[hd] * (1.0 / HEAD_DIM) + EPS) * ng_ref[:, hs(hd)]
        o_ref[rows, hs(hd)] = (jax.nn.sigmoid(mo_ref[rows, hs(hd)]) * y).astype(BF16)
        cn_scr[hd] = decay[one(hd), :] * states[hd] + scale[one(hd), :] * kvns[hd]


MIX_TILE = 2 * MLSTM_CHUNK


def _mix_out_body(gt_ref, q_ref, k_ref, qp_ref, kp_ref, v_ref, mo_ref, cw_ref, cb_ref, ng_ref,
                  a_ref, x_ref, wa_ref, wm_ref, gf_ref, x1_ref, h2_ref,
                  cn_scr, m_scr, qext_scr, kext_scr, ml_cur, ml_prev, y_scr, *, tiles_per_seq, n_tiles):
    s = pl.program_id(0)
    L = MLSTM_CHUNK
    W = MLSTM_WIDTH
    n_chunk = MIX_TILE // L
    tile = jnp.minimum(s, n_tiles - 1)
    seq_start = (tile % tiles_per_seq) == 0

    @pl.when(s == 0)
    def _():
        ml_prev[...] = jnp.zeros_like(ml_prev)

    @pl.when(seq_start)
    def _():
        cn_scr[...] = jnp.zeros_like(cn_scr)
        m_scr[...] = jnp.full_like(m_scr, M_INIT)

    has_prev = jnp.logical_not(seq_start)

    def conv_silu(cur_ref, prev_ref, ext, col0):
        ext[:CONV_HALO, :] = jnp.where(has_prev, prev_ref[...], 0.0)
        ext[CONV_HALO:, :] = cur_ref[...]
        y = cb_ref[:, col0:col0 + W]
        for sh in range(CONV_WIDTH):
            tap = cw_ref[CONV_WIDTH - 1 - sh:CONV_WIDTH - sh, col0:col0 + W]
            y = y + ext[CONV_HALO - sh:CONV_HALO - sh + MIX_TILE, :] * tap
        return y * jax.nn.sigmoid(y)

    qb = conv_silu(q_ref, qp_ref, qext_scr, 0).astype(BF16)
    kf = conv_silu(k_ref, kp_ref, kext_scr, W) * (HEAD_DIM ** -0.5)

    a_prev = jnp.concatenate([a_ref[hd] for hd in range(ATTN_HEADS)], axis=1)
    m_prev_out = ml_prev[...]
    n_slab = D_MODEL // n_chunk
    for ck in range(n_chunk):
        cols = slice(ck * n_slab, (ck + 1) * n_slab)
        y_scr[:, cols] = (jnp.dot(a_prev, wa_ref[:, cols], preferred_element_type=F32)
                          + jnp.dot(m_prev_out, wm_ref[:, cols], preferred_element_type=F32)
                          + x_ref[:, cols])
        rows = slice(ck * L, (ck + 1) * L)
        _mlstm_chunk(gt_ref[:, rows], qb[rows, :], kf[rows, :], v_ref, mo_ref, ng_ref, ml_cur, rows,
                     cn_scr, m_scr)

    y = y_scr[...]
    x1_ref[...] = y
    ms = jnp.mean(y * y, axis=-1, keepdims=True)
    h2_ref[...] = (y * lax.rsqrt(ms + EPS) * gf_ref[...]).astype(BF16)
    ml_prev[...] = ml_cur[...]


def _mix_out(gates_t, m_f32, m_v, conv_w, conv_b, norm_g, attn, x2, w_o, g_ffn, batch, seq):
    t = batch * seq
    W = MLSTM_WIDTH
    assert ATTN_WIDTH == MLSTM_WIDTH and seq % MIX_TILE == 0
    tiles_per_seq = seq // MIX_TILE
    n_tiles = t // MIX_TILE
    halo_per_tile = MIX_TILE // CONV_HALO

    def cur(s):
        return jnp.minimum(s, n_tiles - 1)

    def prev(s):
        return jnp.maximum(s - 1, 0)

    def halo(s):
        return jnp.maximum(cur(s) * halo_per_tile - 1, 0)

    const = lambda s: (0, 0)
    return pl.pallas_call(
        functools.partial(_mix_out_body, tiles_per_seq=tiles_per_seq, n_tiles=n_tiles),
        grid=(n_tiles + 1,),
        in_specs=[
            pl.BlockSpec((N_GATE, MIX_TILE), lambda s: (0, cur(s))),
            pl.BlockSpec((MIX_TILE, W), lambda s: (cur(s), 0)),
            pl.BlockSpec((MIX_TILE, W), lambda s: (cur(s), 1)),
            pl.BlockSpec((CONV_HALO, W), lambda s: (halo(s), 0)),
            pl.BlockSpec((CONV_HALO, W), lambda s: (halo(s), 1)),
            pl.BlockSpec((MIX_TILE, W), lambda s: (cur(s), 0)),
            pl.BlockSpec((MIX_TILE, W), lambda s: (cur(s), 2)),
            pl.BlockSpec((CONV_WIDTH, 2 * W), const),
            pl.BlockSpec((1, 2 * W), const),
            pl.BlockSpec((1, W), const),
            pl.BlockSpec((ATTN_HEADS, MIX_TILE, HEAD_DIM), lambda s: (0, prev(s), 0)),
            pl.BlockSpec((MIX_TILE, D_MODEL), lambda s: (prev(s), 0)),
            pl.BlockSpec((ATTN_WIDTH, D_MODEL), lambda s: (0, 0), pipeline_mode=pl.Buffered(1)),
            pl.BlockSpec((MLSTM_WIDTH, D_MODEL), lambda s: (1, 0), pipeline_mode=pl.Buffered(1)),
            pl.BlockSpec((1, D_MODEL), const),
        ],
        out_specs=[
            pl.BlockSpec((MIX_TILE, D_MODEL), lambda s: (prev(s), 0)),
            pl.BlockSpec((MIX_TILE, D_MODEL), lambda s: (prev(s), 0)),
        ],
        out_shape=[
            jax.ShapeDtypeStruct((t, D_MODEL), F32),
            jax.ShapeDtypeStruct((t, D_MODEL), BF16),
        ],
        scratch_shapes=[pltpu.VMEM((MLSTM_HEADS, HEAD_DIM, 2 * HEAD_DIM), F32),
                        pltpu.VMEM((MLSTM_HEADS, LANES), F32),
                        pltpu.VMEM((CONV_HALO + MIX_TILE, W), F32),
                        pltpu.VMEM((CONV_HALO + MIX_TILE, W), F32),
                        pltpu.VMEM((MIX_TILE, W), BF16),
                        pltpu.VMEM((MIX_TILE, W), BF16),
                        pltpu.VMEM((MIX_TILE, D_MODEL), F32)],
        compiler_params=pltpu.CompilerParams(
            dimension_semantics=("arbitrary",),
            vmem_limit_bytes=VMEM_LIMIT_BYTES),
        name="mlstm_out_proj",
    )(gates_t, m_f32, m_f32, m_f32, m_f32, m_v, m_f32, conv_w, conv_b, norm_g,
      attn, x2, w_o, w_o, g_ffn)


def kernel(x, norm_mix_g, w_in, conv_w, conv_b, gate_b, q_norm_g, k_norm_g, mlstm_norm_g,
           w_out, norm_ffn_g, w_gate, w_up, w_down):
    B, S, _ = x.shape
    t = B * S
    layer = 0
    x2 = x.reshape(t, D_MODEL)

    w = w_in[layer]
    a3 = 3 * ATTN_WIDTH
    mw = MLSTM_WIDTH
    w_main = w[:, :a3 + 4 * mw].astype(BF16)
    w_gate_t = jnp.pad(w[:, a3 + 4 * mw:], ((0, 0), (0, LANES - N_GATE)))
    b_gate = gate_b[layer].reshape(N_GATE, 1)
    head_gain = jnp.stack([
        jnp.tile(q_norm_g[layer] * (HEAD_DIM ** -0.5 * LOG2_E), ATTN_HEADS),
        jnp.tile(k_norm_g[layer], ATTN_HEADS)]).reshape(2, 1, ATTN_WIDTH)

    att, m_f32, m_v, gates_t, wo_b, wg_b, wu_b, wd_b = _in_proj(
        x2, norm_mix_g[layer].reshape(1, D_MODEL), w_main, w_gate_t, b_gate, head_gain,
        (w_out[layer], w_gate[layer], w_up[layer], w_down[layer]), tm=1024)

    attn_out = _dilated_attention(att, B, S)

    x1, h2 = _mix_out(gates_t, m_f32, m_v, conv_w[layer], conv_b[layer].reshape(1, 2 * mw),
                      mlstm_norm_g[layer].reshape(1, mw), attn_out, x2, wo_b,
                      norm_ffn_g[layer].reshape(1, D_MODEL), B, S)

    out = _ffn(h2, wg_b, wu_b, wd_b, x1, tm=1024, th=512)
    return out.reshape(B, S, D_MODEL)
```

```python
import functools

import jax
import jax.numpy as jnp
from jax import lax
from jax.experimental import pallas as pl
from jax.experimental.pallas import tpu as pltpu

D_MODEL = 2048
HEAD_DIM = 128
ATTN_HEADS = 8
MLSTM_HEADS = 8
ATTN_WIDTH = ATTN_HEADS * HEAD_DIM
MLSTM_WIDTH = MLSTM_HEADS * HEAD_DIM
ATTN_PATTERNS = ((128, 1), (512, 4), (2048, 16))
ATTN_BLOCK = 128
MLSTM_CHUNK = 128
CONV_WIDTH = 4
FFN_HIDDEN = 5632
EPS = 1e-6
MASK_VALUE = -1e30
M_INIT = -1e30

LOG2_E = 1.4426950408889634
N_GATE = 2 * MLSTM_HEADS
LANES = 128
V7X_VMEM_BYTES = 64 * 1024 * 1024
VMEM_LIMIT_BYTES = V7X_VMEM_BYTES - 6 * 1024 * 1024

F32 = jnp.float32
BF16 = jnp.bfloat16


def _cast_body(w_ref, o_ref):
    o_ref[...] = w_ref[...].astype(BF16)


def _cast_cols_bf16(w, n_cols, tn):
    rows = w.shape[0]
    return pl.pallas_call(
        _cast_body,
        grid=(n_cols // tn,),
        in_specs=[pl.BlockSpec((rows, tn), lambda j: (0, j))],
        out_specs=pl.BlockSpec((rows, tn), lambda j: (0, j)),
        out_shape=jax.ShapeDtypeStruct((rows, n_cols), BF16),
        compiler_params=pltpu.CompilerParams(
            dimension_semantics=("arbitrary",),
            vmem_limit_bytes=VMEM_LIMIT_BYTES),
        name="cast_w_in",
    )(w)


def _in_proj_body(x_ref, g_ref, w_ref, wg_ref, bg_ref, hg_ref, *rest, n_cast):
    cast_in, rest = rest[:n_cast], rest[n_cast:]
    oatt_ref, omf_ref, omv_ref, ogate_ref = rest[:4]
    cast_out, h_scr = rest[4:4 + n_cast], rest[4 + n_cast]
    j = pl.program_id(1)

    for src, dst in zip(cast_in, cast_out):
        dst[...] = src[...].astype(BF16)

    def project(h=None):
        h = h_scr[...] if h is None else h
        return jnp.dot(h, w_ref[...], preferred_element_type=F32)

    def store_head_normed(acc):
        for hd in range(ATTN_HEADS):
            sl = slice(hd * HEAD_DIM, (hd + 1) * HEAD_DIM)
            a = acc[:, sl]
            ms = jnp.mean(a * a, axis=-1, keepdims=True)
            oatt_ref[hd] = (a * lax.rsqrt(ms + EPS) * hg_ref[:, sl]).astype(BF16)

    @pl.when(j == 0)
    def _():
        x = x_ref[...]
        ms = jnp.mean(x * x, axis=-1, keepdims=True)
        hb = (x * lax.rsqrt(ms + EPS) * g_ref[...]).astype(BF16)
        h_scr[...] = hb
        wg_t = wg_ref[...].T[:N_GATE, :].astype(BF16)
        ogate_ref[...] = lax.dot_general(wg_t, hb, (((1,), (1,)), ((), ())),
                                         preferred_element_type=F32) + bg_ref[...]
        store_head_normed(project(hb))

    @pl.when(j == 1)
    def _():
        store_head_normed(project())

    @pl.when(j == 2)
    def _():
        acc = project()
        for hd in range(ATTN_HEADS):
            oatt_ref[hd] = acc[:, hd * HEAD_DIM:(hd + 1) * HEAD_DIM].astype(BF16)

    @pl.when(jnp.logical_and(j >= 3, j < 6))
    def _():
        omf_ref[...] = project()

    @pl.when(j == 6)
    def _():
        omv_ref[...] = project().astype(BF16)


def _in_proj(x2, g_mix, w_main, w_gate_t, b_gate, head_gain, side_weights, tm):
    t = x2.shape[0]
    tn = ATTN_WIDTH
    n_col = w_main.shape[1] // tn
    n_step = (t // tm) * n_col

    def w_col(j):
        return jnp.where(j == 5, 6, jnp.where(j == 6, 5, j))

    cast_specs, cast_shapes = [], []
    for wt in side_weights:
        rows, cols = wt.shape
        n_blk = max(n for n in range(1, n_step + 1)
                    if rows % n == 0 and (rows // n) % BF16_SUBLANES == 0)
        cast_specs.append(pl.BlockSpec(
            (rows // n_blk, cols), lambda i, j, n_blk=n_blk: (jnp.minimum(i * n_col + j, n_blk - 1), 0)))
        cast_shapes.append(jax.ShapeDtypeStruct(wt.shape, BF16))

    return pl.pallas_call(
        functools.partial(_in_proj_body, n_cast=len(side_weights)),
        grid=(t // tm, n_col),
        in_specs=[
            pl.BlockSpec((tm, D_MODEL), lambda i, j: (i, 0)),
            pl.BlockSpec((1, D_MODEL), lambda i, j: (0, 0)),
            pl.BlockSpec((D_MODEL, tn), lambda i, j: (0, w_col(j))),
            pl.BlockSpec((D_MODEL, LANES), lambda i, j: (0, 0), pipeline_mode=pl.Buffered(1)),
            pl.BlockSpec((N_GATE, 1), lambda i, j: (0, 0)),
            pl.BlockSpec((None, 1, tn), lambda i, j: (jnp.minimum(j, 1), 0, 0)),
        ] + cast_specs,
        out_specs=[
            pl.BlockSpec((None, ATTN_HEADS, tm, HEAD_DIM), lambda i, j: (jnp.minimum(j, 2), 0, i, 0)),
            pl.BlockSpec((tm, tn), lambda i, j: (i, jnp.clip(j - 3, 0, 2))),
            pl.BlockSpec((tm, tn), lambda i, j: (i, 0)),
            pl.BlockSpec((N_GATE, tm), lambda i, j: (0, i)),
        ] + cast_specs,
        out_shape=[
            jax.ShapeDtypeStruct((3, ATTN_HEADS, t, HEAD_DIM), BF16),
            jax.ShapeDtypeStruct((t, 3 * tn), F32),
            jax.ShapeDtypeStruct((t, tn), BF16),
            jax.ShapeDtypeStruct((N_GATE, t), F32),
        ] + cast_shapes,
        scratch_shapes=[pltpu.VMEM((tm, D_MODEL), BF16)],
        compiler_params=pltpu.CompilerParams(
            dimension_semantics=("arbitrary", "arbitrary"),
            vmem_limit_bytes=VMEM_LIMIT_BYTES),
        name="in_proj",
    )(x2, g_mix, w_main, w_gate_t, b_gate, head_gain, *side_weights)


FFN_DOWN_CHUNK = 512


def _ffn_body(h_ref, wg_ref, wu_ref, wd_ref, x1_ref, o_ref, *, n_res):
    c = pl.program_id(1)

    @pl.when(c == 0)
    def _():
        o_ref[...] = jnp.zeros_like(o_ref)

    @pl.when(c < n_res)
    def _():
        slab = x1_ref.shape[0]
        rows = pl.ds(pl.multiple_of(c * slab, slab), slab)
        o_ref[rows, :] += x1_ref[...]

    h = h_ref[...]
    g = jnp.dot(h, wg_ref[...], preferred_element_type=F32)
    u = jnp.dot(h, wu_ref[...], preferred_element_type=F32)
    a = (g * jax.nn.sigmoid(g) * u).astype(BF16)
    for n0 in range(0, D_MODEL, FFN_DOWN_CHUNK):
        cols = slice(n0, n0 + FFN_DOWN_CHUNK)
        o_ref[:, cols] += jnp.dot(a, wd_ref[:, cols], preferred_element_type=F32)


def _ffn(h2, w_g, w_u, w_d, x1, tm, th):
    t = h2.shape[0]
    n_chunk = FFN_HIDDEN // th
    n_res = 8
    assert n_res <= n_chunk and tm % n_res == 0
    return pl.pallas_call(
        functools.partial(_ffn_body, n_res=n_res),
        grid=(t // tm, n_chunk),
        in_specs=[
            pl.BlockSpec((tm, D_MODEL), lambda i, c: (i, 0)),
            pl.BlockSpec((D_MODEL, th), lambda i, c: (0, c)),
            pl.BlockSpec((D_MODEL, th), lambda i, c: (0, c)),
            pl.BlockSpec((th, D_MODEL), lambda i, c: (c, 0)),
            pl.BlockSpec((tm // n_res, D_MODEL), lambda i, c: (i * n_res + jnp.minimum(c, n_res - 1), 0)),
        ],
        out_specs=pl.BlockSpec((tm, D_MODEL), lambda i, c: (i, 0)),
        out_shape=jax.ShapeDtypeStruct((t, D_MODEL), F32),
        compiler_params=pltpu.CompilerParams(
            dimension_semantics=("arbitrary", "arbitrary"),
            vmem_limit_bytes=VMEM_LIMIT_BYTES),
        name="ffn",
    )(h2, w_g, w_u, w_d, x1)


ATTN_GROUPS = 16
ATTN_TILE = ATTN_BLOCK * ATTN_GROUPS
TILE_LANES = ATTN_GROUPS * HEAD_DIM
F32_SUBLANES = 8
BF16_SUBLANES = 16
STAGE_PITCH = 24


def _band_mask(slab, n_slab, span):
    blk = ATTN_BLOCK
    shift = slab.bit_length() - 1
    row = lax.broadcasted_iota(jnp.int32, (blk, 2 * blk), 0)
    col = lax.broadcasted_iota(jnp.int32, (blk, 2 * blk), 1)

    def pos(r):
        return (r & (slab - 1)) * n_slab + lax.shift_right_logical(r, shift)

    k_rel = pos(col & (blk - 1)) - jnp.where(col < blk, blk, 0)
    dist = pos(row) - k_rel
    return jnp.logical_and(dist >= 0, dist <= span), col


def _attn_body(q_in, k_in, v_in, o_ref,
               acc_scr, m_scr, l_scr, stage_q, stage_k, stage_v,
               q_ref, k_ref, v_ref, qf_scr, kf_scr, vf_scr,
               kp_ref, vp_ref, kpf_scr, vpf_scr, *, group):
    jt = pl.program_id(2)
    blk = ATTN_BLOCK
    ones_v = jnp.ones((2 * blk, HEAD_DIM), BF16)

    def lanes(g):
        return slice(g * HEAD_DIM, (g + 1) * HEAD_DIM)

    @pl.when(jt == 0)
    def _():
        kp_ref[...] = jnp.zeros_like(kp_ref)
        vp_ref[...] = jnp.zeros_like(vp_ref)
        kpf_scr[...] = jnp.zeros_like(kpf_scr)
        vpf_scr[...] = jnp.zeros_like(vpf_scr)

    for src, stage, dst_bf16, dst_f32 in ((q_in, stage_q, q_ref, qf_scr),
                                          (k_in, stage_k, k_ref, kf_scr),
                                          (v_in, stage_v, v_ref, vf_scr)):
        for i in range(blk):
            stage[STAGE_PITCH * i:STAGE_PITCH * i + ATTN_GROUPS, :] = (
                src[ATTN_GROUPS * i:ATTN_GROUPS * (i + 1), :].astype(F32))
        for g in range(ATTN_GROUPS):
            cls = stage[pl.ds(g, blk, stride=STAGE_PITCH), :]
            dst_f32[:, lanes(g)] = cls
            dst_bf16[:, lanes(g)] = cls.astype(BF16)

    n_pat = len(ATTN_PATTERNS)
    for pi, (window, dil) in enumerate(reversed(ATTN_PATTERNS)):
        span = window // dil
        n_slab = ATTN_GROUPS // dil
        slab = blk // n_slab
        use_f32 = slab % BF16_SUBLANES != 0
        band, col = _band_mask(slab, n_slab, span)
        band_first = jnp.logical_and(band, jnp.logical_or(col >= blk, jt > 0))

        def gather(cur_ref, cur_f32, prev_ref, prev_f32, res, kb, prev):
            if not prev:
                src, rows = (cur_f32 if use_f32 else cur_ref), slice(slab * kb, slab * (kb + 1))
            elif kb > 0:
                src, rows = (cur_f32 if use_f32 else cur_ref), slice(slab * (kb - 1), slab * kb)
            elif use_f32:
                src, rows = prev_f32, slice(F32_SUBLANES - slab, F32_SUBLANES)
            else:
                src, rows = prev_ref, slice(blk - slab, blk)
            parts = [src[rows, lanes(u * dil + res)] for u in range(n_slab)]
            out = parts[0] if n_slab == 1 else jnp.concatenate(parts, axis=0)
            return out.astype(BF16)

        def st_load(scr, res, kb):
            rows = slice(slab * kb, slab * (kb + 1))
            parts = [scr[rows, lanes(u * dil + res)] for u in range(n_slab)]
            return parts[0] if n_slab == 1 else jnp.concatenate(parts, axis=0)

        def st_store(scr, res, kb, val):
            rows = slice(slab * kb, slab * (kb + 1))
            for u in range(n_slab):
                scr[rows, lanes(u * dil + res)] = val[u * slab:(u + 1) * slab, :]

        blocks = [(res, kb) for res in range(dil) for kb in range(n_slab)]
        for g0 in range(0, len(blocks), group):
            grp = blocks[g0:g0 + group]
            scores = []
            for res, kb in grp:
                q = gather(q_ref, qf_scr, None, None, res, kb, False)
                kw = jnp.concatenate([gather(k_ref, kf_scr, kp_ref, kpf_scr, res, kb, True),
                                      gather(k_ref, kf_scr, kp_ref, kpf_scr, res, kb, False)], axis=0)
                s = lax.dot_general(q, kw, (((1,), (1,)), ((), ())), preferred_element_type=F32)
                scores.append(jnp.where(band_first if kb == 0 else band, s, MASK_VALUE))
            probs = []
            for s in scores:
                m = jnp.max(s, axis=-1, keepdims=True)
                probs.append((jnp.exp2(s - m).astype(BF16), m))
            pvs = []
            for (res, kb), (p, _) in zip(grp, probs):
                vw = jnp.concatenate([gather(v_ref, vf_scr, vp_ref, vpf_scr, res, kb, True),
                                      gather(v_ref, vf_scr, vp_ref, vpf_scr, res, kb, False)], axis=0)
                pvs.append(jnp.dot(p, jnp.concatenate([vw, ones_v], axis=1), preferred_element_type=F32))
            for (res, kb), (_, m), pv in zip(grp, probs, pvs):
                num, den = pv[:, :HEAD_DIM], pv[:, HEAD_DIM:]
                m_new = jnp.broadcast_to(m, (blk, HEAD_DIM))
                if pi > 0:
                    m_old = st_load(m_scr, res, kb)
                    m_new = jnp.maximum(m_old, m_new)
                    w_old = jnp.exp2(m_old - m_new)
                    w_new = jnp.exp2(m - m_new)
                    num = st_load(acc_scr, res, kb) * w_old + num * w_new
                    den = st_load(l_scr, res, kb) * w_old + den * w_new
                if pi < n_pat - 1:
                    st_store(acc_scr, res, kb, num)
                    st_store(m_scr, res, kb, m_new)
                    st_store(l_scr, res, kb, den)
                else:
                    st_store(acc_scr, res, kb, num / den)

    for g in range(ATTN_GROUPS):
        stage_q[pl.ds(g, blk, stride=STAGE_PITCH), :] = acc_scr[:, lanes(g)]
    for i in range(blk):
        o_ref[ATTN_GROUPS * i:ATTN_GROUPS * (i + 1), :] = (
            stage_q[STAGE_PITCH * i:STAGE_PITCH * i + ATTN_GROUPS, :].astype(BF16))

    kp_ref[...] = k_ref[...]
    vp_ref[...] = v_ref[...]
    kpf_scr[...] = kf_scr[blk - F32_SUBLANES:, :]
    vpf_scr[...] = vf_scr[blk - F32_SUBLANES:, :]


def _dilated_attention(att, batch, seq, group=8):
    assert seq % ATTN_TILE == 0
    n_tile = seq // ATTN_TILE

    def spec(which):
        return pl.BlockSpec((None, None, ATTN_TILE, HEAD_DIM),
                            lambda bi, hd, jt: (which, hd, bi * n_tile + jt, 0))

    tile_f32 = pltpu.VMEM((ATTN_BLOCK, TILE_LANES), F32)
    tile_bf16 = pltpu.VMEM((ATTN_BLOCK, TILE_LANES), BF16)
    tail_f32 = pltpu.VMEM((F32_SUBLANES, TILE_LANES), F32)
    stage = pltpu.VMEM((ATTN_BLOCK * STAGE_PITCH, HEAD_DIM), F32)
    return pl.pallas_call(
        functools.partial(_attn_body, group=group),
        grid=(batch, ATTN_HEADS, n_tile),
        in_specs=[spec(0), spec(1), spec(2)],
        out_specs=pl.BlockSpec((None, ATTN_TILE, HEAD_DIM), lambda bi, hd, jt: (hd, bi * n_tile + jt, 0)),
        out_shape=jax.ShapeDtypeStruct((ATTN_HEADS, batch * seq, HEAD_DIM), BF16),
        scratch_shapes=[tile_f32, tile_f32, tile_f32,
                        stage, stage, stage,
                        tile_bf16, tile_bf16, tile_bf16,
                        tile_f32, tile_f32, tile_f32,
                        tile_bf16, tile_bf16,
                        tail_f32, tail_f32],
        compiler_params=pltpu.CompilerParams(
            dimension_semantics=("arbitrary", "arbitrary", "arbitrary"),
            vmem_limit_bytes=VMEM_LIMIT_BYTES),
        name="dilated_attn",
    )(att, att, att)


CONV_HALO = 8


def _mlstm_chunk(gt, qb, kf, v_ref, mo_ref, ng_ref, o_ref, rows, cn_scr, m_scr):
    L = MLSTM_CHUNK

    ig = gt[:MLSTM_HEADS]
    fg = gt[MLSTM_HEADS:]
    lf = jnp.minimum(fg, 0.0) - jnp.log1p(jnp.exp(-jnp.abs(fg)))
    lane = lax.broadcasted_iota(jnp.int32, (MLSTM_HEADS, L), 1)
    b = lf
    s = 1
    while s < L:
        b = b + jnp.where(lane >= s, pltpu.roll(b, s, 1), 0.0)
        s *= 2
    c_row = ig - b
    b_last = b[:, L - 1:L]
    c_max = jnp.max(c_row, axis=1, keepdims=True)
    m_prev = m_scr[:, :1]
    m_chunk = b_last + c_max
    m_new = jnp.maximum(b_last + m_prev, m_chunk)
    decay = jnp.exp(b_last + m_prev - m_new)
    scale = jnp.exp(m_chunk - m_new)
    wa = jnp.exp(c_row - c_max)
    m_scr[...] = jnp.broadcast_to(m_new, m_scr.shape)

    row = lax.broadcasted_iota(jnp.int32, (L, L), 0)
    col = lax.broadcasted_iota(jnp.int32, (L, L), 1)
    causal = col <= row
    ones_v = jnp.ones((L, HEAD_DIM), BF16)

    heads = range(MLSTM_HEADS)

    def hs(hd):
        return slice(hd * HEAD_DIM, (hd + 1) * HEAD_DIM)

    def one(hd):
        return slice(hd, hd + 1)

    k_ts, states, s_qks, inters = [], [], [], []
    for hd in heads:
        q_h = qb[:, hs(hd)]
        k_t = kf[:, hs(hd)].T
        state = cn_scr[hd]
        k_ts.append(k_t)
        states.append(state)
        s_qks.append(jnp.dot(q_h, k_t.astype(BF16), preferred_element_type=F32))
        inters.append(jnp.dot(q_h, state.astype(BF16), preferred_element_type=F32))

    ws, mus, gs = [], [], []
    for hd in heads:
        c_b = jnp.where(causal, jnp.broadcast_to(c_row[one(hd), :], (L, L)), MASK_VALUE)
        mu = jnp.maximum(jnp.max(c_b, axis=-1, keepdims=True), m_prev[one(hd), :])
        ws.append(jnp.exp(c_b - mu) * s_qks[hd])
        mus.append(mu)
        gs.append(jnp.exp(m_prev[one(hd), :] - mu))

    ones_sq = jnp.ones((L, L), BF16)

    def split(x):
        hi = x.astype(BF16)
        return hi, (x - hi.astype(F32)).astype(BF16)

    lf_rows = jnp.concatenate([jnp.broadcast_to(lf[one(hd), :], (HEAD_DIM, L)) for hd in heads], axis=0)
    causal_b = jnp.where(causal, 1.0, 0.0).astype(BF16)
    nt = (((1,), (1,)), ((), ()))
    lf_hi, lf_lo = split(lf_rows)
    b_all = (lax.dot_general(causal_b, lf_hi, nt, preferred_element_type=F32)
             + lax.dot_general(causal_b, lf_lo, nt, preferred_element_type=F32))

    intras, kvns, sum_los = [], [], []
    for hd in heads:
        v_aug = jnp.concatenate([v_ref[rows, hs(hd)], ones_v], axis=1)
        w_hi, w_lo = split(ws[hd])
        intras.append(jnp.dot(w_hi, v_aug, preferred_element_type=F32))
        sum_los.append(jnp.dot(w_lo, ones_sq, preferred_element_type=F32))
        a_t = (k_ts[hd] * jnp.broadcast_to(wa[one(hd), :], (HEAD_DIM, L))).astype(BF16)
        kvns.append(jnp.dot(a_t, v_aug, preferred_element_type=F32))

    cells = []
    for hd in heads:
        g, mu = gs[hd], mus[hd]
        num = g * inters[hd][:, :HEAD_DIM] + intras[hd][:, :HEAD_DIM]
        den = g * inters[hd][:, HEAD_DIM:] + intras[hd][:, HEAD_DIM:] + sum_los[hd]
        cells.append(num / jnp.maximum(jnp.abs(den), jnp.exp(-(b_all[:, hs(hd)] + mu))))

    sq_sums = []
    for hd in heads:
        sq = (cells[hd] * cells[hd]).astype(BF16)
        sq_sums.append(jnp.dot(sq, ones_sq, preferred_element_type=F32))
    for hd in heads:
        y = cells[hd] * lax.rsqrt(sq_sums[hd] * (1.0 / HEAD_DIM) + EPS) * ng_ref[:, hs(hd)]
        o_ref[rows, hs(hd)] = (jax.nn.sigmoid(mo_ref[rows, hs(hd)]) * y).astype(BF16)
        cn_scr[hd] = decay[one(hd), :] * states[hd] + scale[one(hd), :] * kvns[hd]


MIX_TILE = 2 * MLSTM_CHUNK


def _mix_out_body(gt_ref, q_ref, k_ref, qp_ref, kp_ref, v_ref, mo_ref, cw_ref, cb_ref, ng_ref,
                  a_ref, x_ref, wa_ref, wm_ref, gf_ref, x1_ref, h2_ref,
                  cn_scr, m_scr, qext_scr, kext_scr, ml_cur, ml_prev, y_scr, *, tiles_per_seq, n_tiles):
    s = pl.program_id(0)
    L = MLSTM_CHUNK
    W = MLSTM_WIDTH
    n_chunk = MIX_TILE // L
    tile = jnp.minimum(s, n_tiles - 1)
    seq_start = (tile % tiles_per_seq) == 0

    @pl.when(s == 0)
    def _():
        ml_prev[...] = jnp.zeros_like(ml_prev)

    @pl.when(seq_start)
    def _():
        cn_scr[...] = jnp.zeros_like(cn_scr)
        m_scr[...] = jnp.full_like(m_scr, M_INIT)

    has_prev = jnp.logical_not(seq_start)

    def conv_silu(cur_ref, prev_ref, ext, col0):
        ext[:CONV_HALO, :] = jnp.where(has_prev, prev_ref[...], 0.0)
        ext[CONV_HALO:, :] = cur_ref[...]
        y = cb_ref[:, col0:col0 + W]
        for sh in range(CONV_WIDTH):
            tap = cw_ref[CONV_WIDTH - 1 - sh:CONV_WIDTH - sh, col0:col0 + W]
            y = y + ext[CONV_HALO - sh:CONV_HALO - sh + MIX_TILE, :] * tap
        return y * jax.nn.sigmoid(y)

    qb = conv_silu(q_ref, qp_ref, qext_scr, 0).astype(BF16)
    kf = conv_silu(k_ref, kp_ref, kext_scr, W) * (HEAD_DIM ** -0.5)

    a_prev = jnp.concatenate([a_ref[hd] for hd in range(ATTN_HEADS)], axis=1)
    m_prev_out = ml_prev[...]
    n_slab = D_MODEL // n_chunk
    for ck in range(n_chunk):
        cols = slice(ck * n_slab, (ck + 1) * n_slab)
        y_scr[:, cols] = (jnp.dot(a_prev, wa_ref[:, cols], preferred_element_type=F32)
                          + jnp.dot(m_prev_out, wm_ref[:, cols], preferred_element_type=F32)
                          + x_ref[:, cols])
        rows = slice(ck * L, (ck + 1) * L)
        _mlstm_chunk(gt_ref[:, rows], qb[rows, :], kf[rows, :], v_ref, mo_ref, ng_ref, ml_cur, rows,
                     cn_scr, m_scr)

    y = y_scr[...]
    x1_ref[...] = y
    ms = jnp.mean(y * y, axis=-1, keepdims=True)
    h2_ref[...] = (y * lax.rsqrt(ms + EPS) * gf_ref[...]).astype(BF16)
    ml_prev[...] = ml_cur[...]


def _mix_out(gates_t, m_f32, m_v, conv_w, conv_b, norm_g, attn, x2, w_o, g_ffn, batch, seq):
    t = batch * seq
    W = MLSTM_WIDTH
    assert ATTN_WIDTH == MLSTM_WIDTH and seq % MIX_TILE == 0
    tiles_per_seq = seq // MIX_TILE
    n_tiles = t // MIX_TILE
    halo_per_tile = MIX_TILE // CONV_HALO

    def cur(s):
        return jnp.minimum(s, n_tiles - 1)

    def prev(s):
        return jnp.maximum(s - 1, 0)

    def halo(s):
        return jnp.maximum(cur(s) * halo_per_tile - 1, 0)

    const = lambda s: (0, 0)
    return pl.pallas_call(
        functools.partial(_mix_out_body, tiles_per_seq=tiles_per_seq, n_tiles=n_tiles),
        grid=(n_tiles + 1,),
        in_specs=[
            pl.BlockSpec((N_GATE, MIX_TILE), lambda s: (0, cur(s))),
            pl.BlockSpec((MIX_TILE, W), lambda s: (cur(s), 0)),
            pl.BlockSpec((MIX_TILE, W), lambda s: (cur(s), 1)),
            pl.BlockSpec((CONV_HALO, W), lambda s: (halo(s), 0)),
            pl.BlockSpec((CONV_HALO, W), lambda s: (halo(s), 1)),
            pl.BlockSpec((MIX_TILE, W), lambda s: (cur(s), 0)),
            pl.BlockSpec((MIX_TILE, W), lambda s: (cur(s), 2)),
            pl.BlockSpec((CONV_WIDTH, 2 * W), const),
            pl.BlockSpec((1, 2 * W), const),
            pl.BlockSpec((1, W), const),
            pl.BlockSpec((ATTN_HEADS, MIX_TILE, HEAD_DIM), lambda s: (0, prev(s), 0)),
            pl.BlockSpec((MIX_TILE, D_MODEL), lambda s: (prev(s), 0)),
            pl.BlockSpec((ATTN_WIDTH, D_MODEL), lambda s: (0, 0), pipeline_mode=pl.Buffered(1)),
            pl.BlockSpec((MLSTM_WIDTH, D_MODEL), lambda s: (1, 0), pipeline_mode=pl.Buffered(1)),
            pl.BlockSpec((1, D_MODEL), const),
        ],
        out_specs=[
            pl.BlockSpec((MIX_TILE, D_MODEL), lambda s: (prev(s), 0)),
            pl.BlockSpec((MIX_TILE, D_MODEL), lambda s: (prev(s), 0)),
        ],
        out_shape=[
            jax.ShapeDtypeStruct((t, D_MODEL), F32),
            jax.ShapeDtypeStruct((t, D_MODEL), BF16),
        ],
        scratch_shapes=[pltpu.VMEM((MLSTM_HEADS, HEAD_DIM, 2 * HEAD_DIM), F32),
                        pltpu.VMEM((MLSTM_HEADS, LANES), F32),
                        pltpu.VMEM((CONV_HALO + MIX_TILE, W), F32),
                        pltpu.VMEM((CONV_HALO + MIX_TILE, W), F32),
                        pltpu.VMEM((MIX_TILE, W), BF16),
                        pltpu.VMEM((MIX_TILE, W), BF16),
                        pltpu.VMEM((MIX_TILE, D_MODEL), F32)],
        compiler_params=pltpu.CompilerParams(
            dimension_semantics=("arbitrary",),
            vmem_limit_bytes=VMEM_LIMIT_BYTES),
        name="mlstm_out_proj",
    )(gates_t, m_f32, m_f32, m_f32, m_f32, m_v, m_f32, conv_w, conv_b, norm_g,
      attn, x2, w_o, w_o, g_ffn)


def kernel(x, norm_mix_g, w_in, conv_w, conv_b, gate_b, q_norm_g, k_norm_g, mlstm_norm_g,
           w_out, norm_ffn_g, w_gate, w_up, w_down):
    B, S, _ = x.shape
    t = B * S
    layer = 0
    x2 = x.reshape(t, D_MODEL)

    w = w_in[layer]
    a3 = 3 * ATTN_WIDTH
    mw = MLSTM_WIDTH
    w_main = _cast_cols_bf16(w, a3 + 4 * mw, 512)
    w_gate_t = jnp.pad(w[:, a3 + 4 * mw:], ((0, 0), (0, LANES - N_GATE)))
    b_gate = gate_b[layer].reshape(N_GATE, 1)
    head_gain = jnp.stack([
        jnp.tile(q_norm_g[layer] * (HEAD_DIM ** -0.5 * LOG2_E), ATTN_HEADS),
        jnp.tile(k_norm_g[layer], ATTN_HEADS)]).reshape(2, 1, ATTN_WIDTH)

    att, m_f32, m_v, gates_t, wo_b, wg_b, wu_b, wd_b = _in_proj(
        x2, norm_mix_g[layer].reshape(1, D_MODEL), w_main, w_gate_t, b_gate, head_gain,
        (w_out[layer], w_gate[layer], w_up[layer], w_down[layer]), tm=1024)

    attn_out = _dilated_attention(att, B, S)

    x1, h2 = _mix_out(gates_t, m_f32, m_v, conv_w[layer], conv_b[layer].reshape(1, 2 * mw),
                      mlstm_norm_g[layer].reshape(1, mw), attn_out, x2, wo_b,
                      norm_ffn_g[layer].reshape(1, D_MODEL), B, S)

    out = _ffn(h2, wg_b, wu_b, wd_b, x1, tm=1024, th=512)
    return out.reshape(B, S, D_MODEL)
```

```python
import functools

import jax
import jax.numpy as jnp
from jax import lax
from jax.experimental import pallas as pl
from jax.experimental.pallas import tpu as pltpu

D_MODEL = 2048
HEAD_DIM = 128
ATTN_HEADS = 8
MLSTM_HEADS = 8
ATTN_WIDTH = ATTN_HEADS * HEAD_DIM
MLSTM_WIDTH = MLSTM_HEADS * HEAD_DIM
ATTN_PATTERNS = ((128, 1), (512, 4), (2048, 16))
ATTN_BLOCK = 128
MLSTM_CHUNK = 128
CONV_WIDTH = 4
FFN_HIDDEN = 5632
EPS = 1e-6
MASK_VALUE = -1e30
M_INIT = -1e30

LOG2_E = 1.4426950408889634
N_GATE = 2 * MLSTM_HEADS
LANES = 128
V7X_VMEM_BYTES = 64 * 1024 * 1024
VMEM_LIMIT_BYTES = V7X_VMEM_BYTES - 6 * 1024 * 1024

F32 = jnp.float32
BF16 = jnp.bfloat16


def _cast_body(w_ref, o_ref):
    o_ref[...] = w_ref[...].astype(BF16)


def _cast_rows_bf16(w, n_rows, tr):
    cols = w.shape[1]
    return pl.pallas_call(
        _cast_body,
        grid=(n_rows // tr,),
        in_specs=[pl.BlockSpec((tr, cols), lambda j: (j, 0))],
        out_specs=pl.BlockSpec((tr, cols), lambda j: (j, 0)),
        out_shape=jax.ShapeDtypeStruct((n_rows, cols), BF16),
        compiler_params=pltpu.CompilerParams(
            dimension_semantics=("arbitrary",),
            vmem_limit_bytes=VMEM_LIMIT_BYTES),
        name="cast_w_in",
    )(w)


def _in_proj_body(x_ref, g_ref, w_ref, wg_ref, bg_ref, hg_ref, *rest, n_cast):
    cast_in, rest = rest[:n_cast], rest[n_cast:]
    oatt_ref, omf_ref, omv_ref, ogate_ref = rest[:4]
    cast_out, h_scr = rest[4:4 + n_cast], rest[4 + n_cast]
    j = pl.program_id(1)

    for src, dst in zip(cast_in, cast_out):
        dst[...] = src[...].astype(BF16)

    nt = (((1,), (1,)), ((), ()))

    def project(h=None):
        h = h_scr[...] if h is None else h
        return lax.dot_general(h, w_ref[...], nt, preferred_element_type=F32)

    def store_head_normed(acc):
        for hd in range(ATTN_HEADS):
            sl = slice(hd * HEAD_DIM, (hd + 1) * HEAD_DIM)
            a = acc[:, sl]
            ms = jnp.mean(a * a, axis=-1, keepdims=True)
            oatt_ref[hd] = (a * lax.rsqrt(ms + EPS) * hg_ref[:, sl]).astype(BF16)

    @pl.when(j == 0)
    def _():
        x = x_ref[...]
        ms = jnp.mean(x * x, axis=-1, keepdims=True)
        hb = (x * lax.rsqrt(ms + EPS) * g_ref[...]).astype(BF16)
        h_scr[...] = hb
        ogate_ref[...] = lax.dot_general(wg_ref[...].astype(BF16), hb, nt,
                                         preferred_element_type=F32) + bg_ref[...]
        store_head_normed(project(hb))

    @pl.when(j == 1)
    def _():
        store_head_normed(project())

    @pl.when(j == 2)
    def _():
        acc = project()
        for hd in range(ATTN_HEADS):
            oatt_ref[hd] = acc[:, hd * HEAD_DIM:(hd + 1) * HEAD_DIM].astype(BF16)

    @pl.when(jnp.logical_and(j >= 3, j < 6))
    def _():
        omf_ref[...] = project()

    @pl.when(j == 6)
    def _():
        omv_ref[...] = project().astype(BF16)


def _in_proj(x2, g_mix, w_main, w_gate_t, b_gate, head_gain, side_weights, tm):
    t = x2.shape[0]
    tn = ATTN_WIDTH
    n_col = w_main.shape[0] // tn
    n_step = (t // tm) * n_col

    def w_col(j):
        return jnp.where(j == 5, 6, jnp.where(j == 6, 5, j))

    cast_specs, cast_shapes = [], []
    for wt in side_weights:
        rows, cols = wt.shape
        n_blk = max(n for n in range(1, n_step + 1)
                    if rows % n == 0 and (rows // n) % BF16_SUBLANES == 0)
        cast_specs.append(pl.BlockSpec(
            (rows // n_blk, cols), lambda i, j, n_blk=n_blk: (jnp.minimum(i * n_col + j, n_blk - 1), 0)))
        cast_shapes.append(jax.ShapeDtypeStruct(wt.shape, BF16))

    return pl.pallas_call(
        functools.partial(_in_proj_body, n_cast=len(side_weights)),
        grid=(t // tm, n_col),
        in_specs=[
            pl.BlockSpec((tm, D_MODEL), lambda i, j: (i, 0)),
            pl.BlockSpec((1, D_MODEL), lambda i, j: (0, 0)),
            pl.BlockSpec((tn, D_MODEL), lambda i, j: (w_col(j), 0)),
            pl.BlockSpec((N_GATE, D_MODEL), lambda i, j: (0, 0)),
            pl.BlockSpec((N_GATE, 1), lambda i, j: (0, 0)),
            pl.BlockSpec((None, 1, tn), lambda i, j: (jnp.minimum(j, 1), 0, 0)),
        ] + cast_specs,
        out_specs=[
            pl.BlockSpec((None, ATTN_HEADS, tm, HEAD_DIM), lambda i, j: (jnp.minimum(j, 2), 0, i, 0)),
            pl.BlockSpec((tm, tn), lambda i, j: (i, jnp.clip(j - 3, 0, 2))),
            pl.BlockSpec((tm, tn), lambda i, j: (i, 0)),
            pl.BlockSpec((N_GATE, tm), lambda i, j: (0, i)),
        ] + cast_specs,
        out_shape=[
            jax.ShapeDtypeStruct((3, ATTN_HEADS, t, HEAD_DIM), BF16),
            jax.ShapeDtypeStruct((t, 3 * tn), F32),
            jax.ShapeDtypeStruct((t, tn), BF16),
            jax.ShapeDtypeStruct((N_GATE, t), F32),
        ] + cast_shapes,
        scratch_shapes=[pltpu.VMEM((tm, D_MODEL), BF16)],
        compiler_params=pltpu.CompilerParams(
            dimension_semantics=("arbitrary", "arbitrary"),
            vmem_limit_bytes=VMEM_LIMIT_BYTES),
        name="in_proj",
    )(x2, g_mix, w_main, w_gate_t, b_gate, head_gain, *side_weights)


FFN_DOWN_CHUNK = 512


def _ffn_body(h_ref, wg_ref, wu_ref, wd_ref, x1_ref, o_ref, *, n_res):
    c = pl.program_id(1)

    @pl.when(c == 0)
    def _():
        o_ref[...] = jnp.zeros_like(o_ref)

    @pl.when(c < n_res)
    def _():
        slab = x1_ref.shape[0]
        rows = pl.ds(pl.multiple_of(c * slab, slab), slab)
        o_ref[rows, :] += x1_ref[...]

    h = h_ref[...]
    g = jnp.dot(h, wg_ref[...], preferred_element_type=F32)
    u = jnp.dot(h, wu_ref[...], preferred_element_type=F32)
    a = (g * jax.nn.sigmoid(g) * u).astype(BF16)
    for n0 in range(0, D_MODEL, FFN_DOWN_CHUNK):
        cols = slice(n0, n0 + FFN_DOWN_CHUNK)
        o_ref[:, cols] += jnp.dot(a, wd_ref[:, cols], preferred_element_type=F32)


def _ffn(h2, w_g, w_u, w_d, x1, tm, th):
    t = h2.shape[0]
    n_chunk = FFN_HIDDEN // th
    n_res = 8
    assert n_res <= n_chunk and tm % n_res == 0
    return pl.pallas_call(
        functools.partial(_ffn_body, n_res=n_res),
        grid=(t // tm, n_chunk),
        in_specs=[
            pl.BlockSpec((tm, D_MODEL), lambda i, c: (i, 0)),
            pl.BlockSpec((D_MODEL, th), lambda i, c: (0, c)),
            pl.BlockSpec((D_MODEL, th), lambda i, c: (0, c)),
            pl.BlockSpec((th, D_MODEL), lambda i, c: (c, 0)),
            pl.BlockSpec((tm // n_res, D_MODEL), lambda i, c: (i * n_res + jnp.minimum(c, n_res - 1), 0)),
        ],
        out_specs=pl.BlockSpec((tm, D_MODEL), lambda i, c: (i, 0)),
        out_shape=jax.ShapeDtypeStruct((t, D_MODEL), F32),
        compiler_params=pltpu.CompilerParams(
            dimension_semantics=("arbitrary", "arbitrary"),
            vmem_limit_bytes=VMEM_LIMIT_BYTES),
        name="ffn",
    )(h2, w_g, w_u, w_d, x1)


ATTN_GROUPS = 16
ATTN_TILE = ATTN_BLOCK * ATTN_GROUPS
TILE_LANES = ATTN_GROUPS * HEAD_DIM
F32_SUBLANES = 8
BF16_SUBLANES = 16
STAGE_PITCH = 24


def _band_mask(slab, n_slab, span):
    blk = ATTN_BLOCK
    shift = slab.bit_length() - 1
    row = lax.broadcasted_iota(jnp.int32, (blk, 2 * blk), 0)
    col = lax.broadcasted_iota(jnp.int32, (blk, 2 * blk), 1)

    def pos(r):
        return (r & (slab - 1)) * n_slab + lax.shift_right_logical(r, shift)

    k_rel = pos(col & (blk - 1)) - jnp.where(col < blk, blk, 0)
    dist = pos(row) - k_rel
    return jnp.logical_and(dist >= 0, dist <= span), col


def _attn_body(q_in, k_in, v_in, o_ref,
               acc_scr, m_scr, l_scr, stage_q, stage_k, stage_v,
               q_ref, k_ref, v_ref, qf_scr, kf_scr, vf_scr,
               kp_ref, vp_ref, kpf_scr, vpf_scr, *, group):
    jt = pl.program_id(2)
    blk = ATTN_BLOCK
    ones_v = jnp.ones((2 * blk, HEAD_DIM), BF16)

    def lanes(g):
        return slice(g * HEAD_DIM, (g + 1) * HEAD_DIM)

    @pl.when(jt == 0)
    def _():
        kp_ref[...] = jnp.zeros_like(kp_ref)
        vp_ref[...] = jnp.zeros_like(vp_ref)
        kpf_scr[...] = jnp.zeros_like(kpf_scr)
        vpf_scr[...] = jnp.zeros_like(vpf_scr)

    for src, stage, dst_bf16, dst_f32 in ((q_in, stage_q, q_ref, qf_scr),
                                          (k_in, stage_k, k_ref, kf_scr),
                                          (v_in, stage_v, v_ref, vf_scr)):
        for i in range(blk):
            stage[STAGE_PITCH * i:STAGE_PITCH * i + ATTN_GROUPS, :] = (
                src[ATTN_GROUPS * i:ATTN_GROUPS * (i + 1), :].astype(F32))
        for g in range(ATTN_GROUPS):
            cls = stage[pl.ds(g, blk, stride=STAGE_PITCH), :]
            dst_f32[:, lanes(g)] = cls
            dst_bf16[:, lanes(g)] = cls.astype(BF16)

    n_pat = len(ATTN_PATTERNS)
    for pi, (window, dil) in enumerate(reversed(ATTN_PATTERNS)):
        span = window // dil
        n_slab = ATTN_GROUPS // dil
        slab = blk // n_slab
        use_f32 = slab % BF16_SUBLANES != 0
        band, col = _band_mask(slab, n_slab, span)
        band_first = jnp.logical_and(band, jnp.logical_or(col >= blk, jt > 0))

        def gather(cur_ref, cur_f32, prev_ref, prev_f32, res, kb, prev):
            if not prev:
                src, rows = (cur_f32 if use_f32 else cur_ref), slice(slab * kb, slab * (kb + 1))
            elif kb > 0:
                src, rows = (cur_f32 if use_f32 else cur_ref), slice(slab * (kb - 1), slab * kb)
            elif use_f32:
                src, rows = prev_f32, slice(F32_SUBLANES - slab, F32_SUBLANES)
            else:
                src, rows = prev_ref, slice(blk - slab, blk)
            parts = [src[rows, lanes(u * dil + res)] for u in range(n_slab)]
            out = parts[0] if n_slab == 1 else jnp.concatenate(parts, axis=0)
            return out.astype(BF16)

        def st_load(scr, res, kb):
            rows = slice(slab * kb, slab * (kb + 1))
            parts = [scr[rows, lanes(u * dil + res)] for u in range(n_slab)]
            return parts[0] if n_slab == 1 else jnp.concatenate(parts, axis=0)

        def st_store(scr, res, kb, val):
            rows = slice(slab * kb, slab * (kb + 1))
            for u in range(n_slab):
                scr[rows, lanes(u * dil + res)] = val[u * slab:(u + 1) * slab, :]

        blocks = [(res, kb) for res in range(dil) for kb in range(n_slab)]
        for g0 in range(0, len(blocks), group):
            grp = blocks[g0:g0 + group]
            scores = []
            for res, kb in grp:
                q = gather(q_ref, qf_scr, None, None, res, kb, False)
                kw = jnp.concatenate([gather(k_ref, kf_scr, kp_ref, kpf_scr, res, kb, True),
                                      gather(k_ref, kf_scr, kp_ref, kpf_scr, res, kb, False)], axis=0)
                s = lax.dot_general(q, kw, (((1,), (1,)), ((), ())), preferred_element_type=F32)
                scores.append(jnp.where(band_first if kb == 0 else band, s, MASK_VALUE))
            probs = []
            for s in scores:
                m = jnp.max(s, axis=-1, keepdims=True)
                probs.append((jnp.exp2(s - m).astype(BF16), m))
            pvs = []
            for (res, kb), (p, _) in zip(grp, probs):
                vw = jnp.concatenate([gather(v_ref, vf_scr, vp_ref, vpf_scr, res, kb, True),
                                      gather(v_ref, vf_scr, vp_ref, vpf_scr, res, kb, False)], axis=0)
                pvs.append(jnp.dot(p, jnp.concatenate([vw, ones_v], axis=1), preferred_element_type=F32))
            for (res, kb), (_, m), pv in zip(grp, probs, pvs):
                num, den = pv[:, :HEAD_DIM], pv[:, HEAD_DIM:]
                m_new = jnp.broadcast_to(m, (blk, HEAD_DIM))
                if pi > 0:
                    m_old = st_load(m_scr, res, kb)
                    m_new = jnp.maximum(m_old, m_new)
                    w_old = jnp.exp2(m_old - m_new)
                    w_new = jnp.exp2(m - m_new)
                    num = st_load(acc_scr, res, kb) * w_old + num * w_new
                    den = st_load(l_scr, res, kb) * w_old + den * w_new
                if pi < n_pat - 1:
                    st_store(acc_scr, res, kb, num)
                    st_store(m_scr, res, kb, m_new)
                    st_store(l_scr, res, kb, den)
                else:
                    st_store(acc_scr, res, kb, num / den)

    for g in range(ATTN_GROUPS):
        stage_q[pl.ds(g, blk, stride=STAGE_PITCH), :] = acc_scr[:, lanes(g)]
    for i in range(blk):
        o_ref[ATTN_GROUPS * i:ATTN_GROUPS * (i + 1), :] = (
            stage_q[STAGE_PITCH * i:STAGE_PITCH * i + ATTN_GROUPS, :].astype(BF16))

    kp_ref[...] = k_ref[...]
    vp_ref[...] = v_ref[...]
    kpf_scr[...] = kf_scr[blk - F32_SUBLANES:, :]
    vpf_scr[...] = vf_scr[blk - F32_SUBLANES:, :]


def _dilated_attention(att, batch, seq, group=8):
    assert seq % ATTN_TILE == 0
    n_tile = seq // ATTN_TILE

    def spec(which):
        return pl.BlockSpec((None, None, ATTN_TILE, HEAD_DIM),
                            lambda bi, hd, jt: (which, hd, bi * n_tile + jt, 0))

    tile_f32 = pltpu.VMEM((ATTN_BLOCK, TILE_LANES), F32)
    tile_bf16 = pltpu.VMEM((ATTN_BLOCK, TILE_LANES), BF16)
    tail_f32 = pltpu.VMEM((F32_SUBLANES, TILE_LANES), F32)
    stage = pltpu.VMEM((ATTN_BLOCK * STAGE_PITCH, HEAD_DIM), F32)
    return pl.pallas_call(
        functools.partial(_attn_body, group=group),
        grid=(batch, ATTN_HEADS, n_tile),
        in_specs=[spec(0), spec(1), spec(2)],
        out_specs=pl.BlockSpec((None, ATTN_TILE, HEAD_DIM), lambda bi, hd, jt: (hd, bi * n_tile + jt, 0)),
        out_shape=jax.ShapeDtypeStruct((ATTN_HEADS, batch * seq, HEAD_DIM), BF16),
        scratch_shapes=[tile_f32, tile_f32, tile_f32,
                        stage, stage, stage,
                        tile_bf16, tile_bf16, tile_bf16,
                        tile_f32, tile_f32, tile_f32,
                        tile_bf16, tile_bf16,
                        tail_f32, tail_f32],
        compiler_params=pltpu.CompilerParams(
            dimension_semantics=("arbitrary", "arbitrary", "arbitrary"),
            vmem_limit_bytes=VMEM_LIMIT_BYTES),
        name="dilated_attn",
    )(att, att, att)


CONV_HALO = 8


def _mlstm_chunk(gt, qb, kf, v_ref, mo_ref, ng_ref, o_ref, rows, cn_scr, m_scr):
    L = MLSTM_CHUNK

    ig = gt[:MLSTM_HEADS]
    fg = gt[MLSTM_HEADS:]
    lf = jnp.minimum(fg, 0.0) - jnp.log1p(jnp.exp(-jnp.abs(fg)))
    lane = lax.broadcasted_iota(jnp.int32, (MLSTM_HEADS, L), 1)
    b = lf
    s = 1
    while s < L:
        b = b + jnp.where(lane >= s, pltpu.roll(b, s, 1), 0.0)
        s *= 2
    c_row = ig - b
    b_last = b[:, L - 1:L]
    c_max = jnp.max(c_row, axis=1, keepdims=True)
    m_prev = m_scr[:, :1]
    m_chunk = b_last + c_max
    m_new = jnp.maximum(b_last + m_prev, m_chunk)
    decay = jnp.exp(b_last + m_prev - m_new)
    scale = jnp.exp(m_chunk - m_new)
    wa = jnp.exp(c_row - c_max)
    m_scr[...] = jnp.broadcast_to(m_new, m_scr.shape)

    row = lax.broadcasted_iota(jnp.int32, (L, L), 0)
    col = lax.broadcasted_iota(jnp.int32, (L, L), 1)
    causal = col <= row
    ones_v = jnp.ones((L, HEAD_DIM), BF16)

    heads = range(MLSTM_HEADS)

    def hs(hd):
        return slice(hd * HEAD_DIM, (hd + 1) * HEAD_DIM)

    def one(hd):
        return slice(hd, hd + 1)

    k_ts, states, s_qks, inters = [], [], [], []
    for hd in heads:
        q_h = qb[:, hs(hd)]
        k_t = kf[:, hs(hd)].T
        state = cn_scr[hd]
        k_ts.append(k_t)
        states.append(state)
        s_qks.append(jnp.dot(q_h, k_t.astype(BF16), preferred_element_type=F32))
        inters.append(jnp.dot(q_h, state.astype(BF16), preferred_element_type=F32))

    ws, mus, gs = [], [], []
    for hd in heads:
        c_b = jnp.where(causal, jnp.broadcast_to(c_row[one(hd), :], (L, L)), MASK_VALUE)
        mu = jnp.maximum(jnp.max(c_b, axis=-1, keepdims=True), m_prev[one(hd), :])
        ws.append(jnp.exp(c_b - mu) * s_qks[hd])
        mus.append(mu)
        gs.append(jnp.exp(m_prev[one(hd), :] - mu))

    ones_sq = jnp.ones((L, L), BF16)

    def split(x):
        hi = x.astype(BF16)
        return hi, (x - hi.astype(F32)).astype(BF16)

    lf_rows = jnp.concatenate([jnp.broadcast_to(lf[one(hd), :], (HEAD_DIM, L)) for hd in heads], axis=0)
    causal_b = jnp.where(causal, 1.0, 0.0).astype(BF16)
    nt = (((1,), (1,)), ((), ()))
    lf_hi, lf_lo = split(lf_rows)
    b_all = (lax.dot_general(causal_b, lf_hi, nt, preferred_element_type=F32)
             + lax.dot_general(causal_b, lf_lo, nt, preferred_element_type=F32))

    intras, kvns, sum_los = [], [], []
    for hd in heads:
        v_aug = jnp.concatenate([v_ref[rows, hs(hd)], ones_v], axis=1)
        w_hi, w_lo = split(ws[hd])
        intras.append(jnp.dot(w_hi, v_aug, preferred_element_type=F32))
        sum_los.append(jnp.dot(w_lo, ones_sq, preferred_element_type=F32))
        a_t = (k_ts[hd] * jnp.broadcast_to(wa[one(hd), :], (HEAD_DIM, L))).astype(BF16)
        kvns.append(jnp.dot(a_t, v_aug, preferred_element_type=F32))

    cells = []
    for hd in heads:
        g, mu = gs[hd], mus[hd]
        num = g * inters[hd][:, :HEAD_DIM] + intras[hd][:, :HEAD_DIM]
        den = g * inters[hd][:, HEAD_DIM:] + intras[hd][:, HEAD_DIM:] + sum_los[hd]
        cells.append(num / jnp.maximum(jnp.abs(den), jnp.exp(-(b_all[:, hs(hd)] + mu))))

    sq_sums = []
    for hd in heads:
        sq = (cells[hd] * cells[hd]).astype(BF16)
        sq_sums.append(jnp.dot(sq, ones_sq, preferred_element_type=F32))
    for hd in heads:
        y = cells[hd] * lax.rsqrt(sq_sums[hd] * (1.0 / HEAD_DIM) + EPS) * ng_ref[:, hs(hd)]
        o_ref[rows, hs(hd)] = (jax.nn.sigmoid(mo_ref[rows, hs(hd)]) * y).astype(BF16)
        cn_scr[hd] = decay[one(hd), :] * states[hd] + scale[one(hd), :] * kvns[hd]


MIX_TILE = 2 * MLSTM_CHUNK


def _mix_out_body(gt_ref, q_ref, k_ref, qp_ref, kp_ref, v_ref, mo_ref, cw_ref, cb_ref, ng_ref,
                  a_ref, x_ref, wa_ref, wm_ref, gf_ref, x1_ref, h2_ref,
                  cn_scr, m_scr, qext_scr, kext_scr, ml_cur, ml_prev, y_scr, *, tiles_per_seq, n_tiles):
    s = pl.program_id(0)
    L = MLSTM_CHUNK
    W = MLSTM_WIDTH
    n_chunk = MIX_TILE // L
    tile = jnp.minimum(s, n_tiles - 1)
    seq_start = (tile % tiles_per_seq) == 0

    @pl.when(s == 0)
    def _():
        ml_prev[...] = jnp.zeros_like(ml_prev)

    @pl.when(seq_start)
    def _():
        cn_scr[...] = jnp.zeros_like(cn_scr)
        m_scr[...] = jnp.full_like(m_scr, M_INIT)

    has_prev = jnp.logical_not(seq_start)

    def conv_silu(cur_ref, prev_ref, ext, col0):
        ext[:CONV_HALO, :] = jnp.where(has_prev, prev_ref[...], 0.0)
        ext[CONV_HALO:, :] = cur_ref[...]
        y = cb_ref[:, col0:col0 + W]
        for sh in range(CONV_WIDTH):
            tap = cw_ref[CONV_WIDTH - 1 - sh:CONV_WIDTH - sh, col0:col0 + W]
            y = y + ext[CONV_HALO - sh:CONV_HALO - sh + MIX_TILE, :] * tap
        return y * jax.nn.sigmoid(y)

    qb = conv_silu(q_ref, qp_ref, qext_scr, 0).astype(BF16)
    kf = conv_silu(k_ref, kp_ref, kext_scr, W) * (HEAD_DIM ** -0.5)

    a_prev = jnp.concatenate([a_ref[hd] for hd in range(ATTN_HEADS)], axis=1)
    m_prev_out = ml_prev[...]
    n_slab = D_MODEL // n_chunk
    for ck in range(n_chunk):
        cols = slice(ck * n_slab, (ck + 1) * n_slab)
        y_scr[:, cols] = (jnp.dot(a_prev, wa_ref[:, cols], preferred_element_type=F32)
                          + jnp.dot(m_prev_out, wm_ref[:, cols], preferred_element_type=F32)
                          + x_ref[:, cols])
        rows = slice(ck * L, (ck + 1) * L)
        _mlstm_chunk(gt_ref[:, rows], qb[rows, :], kf[rows, :], v_ref, mo_ref, ng_ref, ml_cur, rows,
                     cn_scr, m_scr)

    y = y_scr[...]
    x1_ref[...] = y
    ms = jnp.mean(y * y, axis=-1, keepdims=True)
    h2_ref[...] = (y * lax.rsqrt(ms + EPS) * gf_ref[...]).astype(BF16)
    ml_prev[...] = ml_cur[...]


def _mix_out(gates_t, m_f32, m_v, conv_w, conv_b, norm_g, attn, x2, w_o, g_ffn, batch, seq):
    t = batch * seq
    W = MLSTM_WIDTH
    assert ATTN_WIDTH == MLSTM_WIDTH and seq % MIX_TILE == 0
    tiles_per_seq = seq // MIX_TILE
    n_tiles = t // MIX_TILE
    halo_per_tile = MIX_TILE // CONV_HALO

    def cur(s):
        return jnp.minimum(s, n_tiles - 1)

    def prev(s):
        return jnp.maximum(s - 1, 0)

    def halo(s):
        return jnp.maximum(cur(s) * halo_per_tile - 1, 0)

    const = lambda s: (0, 0)
    return pl.pallas_call(
        functools.partial(_mix_out_body, tiles_per_seq=tiles_per_seq, n_tiles=n_tiles),
        grid=(n_tiles + 1,),
        in_specs=[
            pl.BlockSpec((N_GATE, MIX_TILE), lambda s: (0, cur(s))),
            pl.BlockSpec((MIX_TILE, W), lambda s: (cur(s), 0)),
            pl.BlockSpec((MIX_TILE, W), lambda s: (cur(s), 1)),
            pl.BlockSpec((CONV_HALO, W), lambda s: (halo(s), 0)),
            pl.BlockSpec((CONV_HALO, W), lambda s: (halo(s), 1)),
            pl.BlockSpec((MIX_TILE, W), lambda s: (cur(s), 0)),
            pl.BlockSpec((MIX_TILE, W), lambda s: (cur(s), 2)),
            pl.BlockSpec((CONV_WIDTH, 2 * W), const),
            pl.BlockSpec((1, 2 * W), const),
            pl.BlockSpec((1, W), const),
            pl.BlockSpec((ATTN_HEADS, MIX_TILE, HEAD_DIM), lambda s: (0, prev(s), 0)),
            pl.BlockSpec((MIX_TILE, D_MODEL), lambda s: (prev(s), 0)),
            pl.BlockSpec((ATTN_WIDTH, D_MODEL), lambda s: (0, 0), pipeline_mode=pl.Buffered(1)),
            pl.BlockSpec((MLSTM_WIDTH, D_MODEL), lambda s: (1, 0), pipeline_mode=pl.Buffered(1)),
            pl.BlockSpec((1, D_MODEL), const),
        ],
        out_specs=[
            pl.BlockSpec((MIX_TILE, D_MODEL), lambda s: (prev(s), 0)),
            pl.BlockSpec((MIX_TILE, D_MODEL), lambda s: (prev(s), 0)),
        ],
        out_shape=[
            jax.ShapeDtypeStruct((t, D_MODEL), F32),
            jax.ShapeDtypeStruct((t, D_MODEL), BF16),
        ],
        scratch_shapes=[pltpu.VMEM((MLSTM_HEADS, HEAD_DIM, 2 * HEAD_DIM), F32),
                        pltpu.VMEM((MLSTM_HEADS, LANES), F32),
                        pltpu.VMEM((CONV_HALO + MIX_TILE, W), F32),
                        pltpu.VMEM((CONV_HALO + MIX_TILE, W), F32),
                        pltpu.VMEM((MIX_TILE, W), BF16),
                        pltpu.VMEM((MIX_TILE, W), BF16),
                        pltpu.VMEM((MIX_TILE, D_MODEL), F32)],
        compiler_params=pltpu.CompilerParams(
            dimension_semantics=("arbitrary",),
            vmem_limit_bytes=VMEM_LIMIT_BYTES),
        name="mlstm_out_proj",
    )(gates_t, m_f32, m_f32, m_f32, m_f32, m_v, m_f32, conv_w, conv_b, norm_g,
      attn, x2, w_o, w_o, g_ffn)


def kernel(x, norm_mix_g, w_in, conv_w, conv_b, gate_b, q_norm_g, k_norm_g, mlstm_norm_g,
           w_out, norm_ffn_g, w_gate, w_up, w_down):
    B, S, _ = x.shape
    t = B * S
    layer = 0
    x2 = x.reshape(t, D_MODEL)

    w_t = w_in[layer].T
    a3 = 3 * ATTN_WIDTH
    mw = MLSTM_WIDTH
    n_main = a3 + 4 * mw
    w_main = _cast_rows_bf16(w_t, n_main, 512)
    w_gate_t = w_t[n_main:]
    b_gate = gate_b[layer].reshape(N_GATE, 1)
    head_gain = jnp.stack([
        jnp.tile(q_norm_g[layer] * (HEAD_DIM ** -0.5 * LOG2_E), ATTN_HEADS),
        jnp.tile(k_norm_g[layer], ATTN_HEADS)]).reshape(2, 1, ATTN_WIDTH)

    att, m_f32, m_v, gates_t, wo_b, wg_b, wu_b, wd_b = _in_proj(
        x2, norm_mix_g[layer].reshape(1, D_MODEL), w_main, w_gate_t, b_gate, head_gain,
        (w_out[layer], w_gate[layer], w_up[layer], w_down[layer]), tm=1024)

    attn_out = _dilated_attention(att, B, S)

    x1, h2 = _mix_out(gates_t, m_f32, m_v, conv_w[layer], conv_b[layer].reshape(1, 2 * mw),
                      mlstm_norm_g[layer].reshape(1, mw), attn_out, x2, wo_b,
                      norm_ffn_g[layer].reshape(1, D_MODEL), B, S)

    out = _ffn(h2, wg_b, wu_b, wd_b, x1, tm=1024, th=512)
    return out.reshape(B, S, D_MODEL)
```

```python
import functools

import jax
import jax.numpy as jnp
from jax import lax
from jax.experimental import pallas as pl
from jax.experimental.pallas import tpu as pltpu

D_MODEL = 2048
HEAD_DIM = 128
ATTN_HEADS = 8
MLSTM_HEADS = 8
ATTN_WIDTH = ATTN_HEADS * HEAD_DIM
MLSTM_WIDTH = MLSTM_HEADS * HEAD_DIM
ATTN_PATTERNS = ((128, 1), (512, 4), (2048, 16))
ATTN_BLOCK = 128
MLSTM_CHUNK = 128
CONV_WIDTH = 4
FFN_HIDDEN = 5632
EPS = 1e-6
MASK_VALUE = -1e30
M_INIT = -1e30

LOG2_E = 1.4426950408889634
N_GATE = 2 * MLSTM_HEADS
LANES = 128
V7X_VMEM_BYTES = 64 * 1024 * 1024
VMEM_LIMIT_BYTES = V7X_VMEM_BYTES - 5 * 1024 * 1024

F32 = jnp.float32
BF16 = jnp.bfloat16


def _cast_body(w_ref, o_ref):
    o_ref[...] = w_ref[...].astype(BF16)


def _cast_rows_bf16(w, n_rows, tr):
    cols = w.shape[1]
    return pl.pallas_call(
        _cast_body,
        grid=(n_rows // tr,),
        in_specs=[pl.BlockSpec((tr, cols), lambda j: (j, 0))],
        out_specs=pl.BlockSpec((tr, cols), lambda j: (j, 0)),
        out_shape=jax.ShapeDtypeStruct((n_rows, cols), BF16),
        compiler_params=pltpu.CompilerParams(
            dimension_semantics=("arbitrary",),
            vmem_limit_bytes=VMEM_LIMIT_BYTES),
        name="cast_w_in",
    )(w)


def _in_proj_body(x_ref, g_ref, w_ref, wg_ref, bg_ref, hg_ref, *rest, n_cast):
    cast_in, rest = rest[:n_cast], rest[n_cast:]
    oatt_ref, omf_ref, omv_ref, ogate_ref = rest[:4]
    cast_out, h_scr = rest[4:4 + n_cast], rest[4 + n_cast]
    j = pl.program_id(1)

    for src, dst in zip(cast_in, cast_out):
        dst[...] = src[...].astype(BF16)

    nt = (((1,), (1,)), ((), ()))

    def project(h=None):
        h = h_scr[...] if h is None else h
        return lax.dot_general(h, w_ref[...], nt, preferred_element_type=F32)

    def store_head_normed(acc):
        for hd in range(ATTN_HEADS):
            sl = slice(hd * HEAD_DIM, (hd + 1) * HEAD_DIM)
            a = acc[:, sl]
            ms = jnp.mean(a * a, axis=-1, keepdims=True)
            oatt_ref[hd] = (a * lax.rsqrt(ms + EPS) * hg_ref[:, sl]).astype(BF16)

    @pl.when(j == 0)
    def _():
        x = x_ref[...]
        ms = jnp.mean(x * x, axis=-1, keepdims=True)
        hb = (x * lax.rsqrt(ms + EPS) * g_ref[...]).astype(BF16)
        h_scr[...] = hb
        ogate_ref[...] = lax.dot_general(wg_ref[...].astype(BF16), hb, nt,
                                         preferred_element_type=F32) + bg_ref[...]
        store_head_normed(project(hb))

    @pl.when(j == 1)
    def _():
        store_head_normed(project())

    @pl.when(j == 2)
    def _():
        acc = project()
        for hd in range(ATTN_HEADS):
            oatt_ref[hd] = acc[:, hd * HEAD_DIM:(hd + 1) * HEAD_DIM].astype(BF16)

    @pl.when(jnp.logical_and(j >= 3, j < 6))
    def _():
        omf_ref[...] = project()

    @pl.when(j == 6)
    def _():
        omv_ref[...] = project().astype(BF16)


def _in_proj(x2, g_mix, w_main, w_gate_t, b_gate, head_gain, side_weights, tm):
    t = x2.shape[0]
    tn = ATTN_WIDTH
    n_col = w_main.shape[0] // tn
    n_step = (t // tm) * n_col

    def w_col(j):
        return jnp.where(j == 5, 6, jnp.where(j == 6, 5, j))

    cast_specs, cast_shapes = [], []
    for wt in side_weights:
        rows, cols = wt.shape
        n_blk = max(n for n in range(1, n_step + 1)
                    if rows % n == 0 and (rows // n) % BF16_SUBLANES == 0)
        cast_specs.append(pl.BlockSpec(
            (rows // n_blk, cols), lambda i, j, n_blk=n_blk: (jnp.minimum(i * n_col + j, n_blk - 1), 0)))
        cast_shapes.append(jax.ShapeDtypeStruct(wt.shape, BF16))

    return pl.pallas_call(
        functools.partial(_in_proj_body, n_cast=len(side_weights)),
        grid=(t // tm, n_col),
        in_specs=[
            pl.BlockSpec((tm, D_MODEL), lambda i, j: (i, 0)),
            pl.BlockSpec((1, D_MODEL), lambda i, j: (0, 0)),
            pl.BlockSpec((tn, D_MODEL), lambda i, j: (w_col(j), 0)),
            pl.BlockSpec((N_GATE, D_MODEL), lambda i, j: (0, 0)),
            pl.BlockSpec((N_GATE, 1), lambda i, j: (0, 0)),
            pl.BlockSpec((None, 1, tn), lambda i, j: (jnp.minimum(j, 1), 0, 0)),
        ] + cast_specs,
        out_specs=[
            pl.BlockSpec((None, ATTN_HEADS, tm, HEAD_DIM), lambda i, j: (jnp.minimum(j, 2), 0, i, 0)),
            pl.BlockSpec((tm, tn), lambda i, j: (i, jnp.clip(j - 3, 0, 2))),
            pl.BlockSpec((tm, tn), lambda i, j: (i, 0)),
            pl.BlockSpec((N_GATE, tm), lambda i, j: (0, i)),
        ] + cast_specs,
        out_shape=[
            jax.ShapeDtypeStruct((3, ATTN_HEADS, t, HEAD_DIM), BF16),
            jax.ShapeDtypeStruct((t, 3 * tn), F32),
            jax.ShapeDtypeStruct((t, tn), BF16),
            jax.ShapeDtypeStruct((N_GATE, t), F32),
        ] + cast_shapes,
        scratch_shapes=[pltpu.VMEM((tm, D_MODEL), BF16)],
        compiler_params=pltpu.CompilerParams(
            dimension_semantics=("arbitrary", "arbitrary"),
            vmem_limit_bytes=VMEM_LIMIT_BYTES),
        name="in_proj",
    )(x2, g_mix, w_main, w_gate_t, b_gate, head_gain, *side_weights)


FFN_DOWN_CHUNK = 512


def _ffn_body(h_ref, wg_ref, wu_ref, wd_ref, x1_ref, o_ref, *, n_res):
    c = pl.program_id(1)

    @pl.when(c == 0)
    def _():
        o_ref[...] = jnp.zeros_like(o_ref)

    @pl.when(c < n_res)
    def _():
        slab = x1_ref.shape[0]
        rows = pl.ds(pl.multiple_of(c * slab, slab), slab)
        o_ref[rows, :] += x1_ref[...]

    h = h_ref[...]
    g = jnp.dot(h, wg_ref[...], preferred_element_type=F32)
    u = jnp.dot(h, wu_ref[...], preferred_element_type=F32)
    a = (g * jax.nn.sigmoid(g) * u).astype(BF16)
    for n0 in range(0, D_MODEL, FFN_DOWN_CHUNK):
        cols = slice(n0, n0 + FFN_DOWN_CHUNK)
        o_ref[:, cols] += jnp.dot(a, wd_ref[:, cols], preferred_element_type=F32)


def _ffn(h2, w_g, w_u, w_d, x1, tm, th):
    t = h2.shape[0]
    n_chunk = FFN_HIDDEN // th
    n_res = 8
    assert n_res <= n_chunk and tm % n_res == 0
    return pl.pallas_call(
        functools.partial(_ffn_body, n_res=n_res),
        grid=(t // tm, n_chunk),
        in_specs=[
            pl.BlockSpec((tm, D_MODEL), lambda i, c: (i, 0)),
            pl.BlockSpec((D_MODEL, th), lambda i, c: (0, c)),
            pl.BlockSpec((D_MODEL, th), lambda i, c: (0, c)),
            pl.BlockSpec((th, D_MODEL), lambda i, c: (c, 0)),
            pl.BlockSpec((tm // n_res, D_MODEL), lambda i, c: (i * n_res + jnp.minimum(c, n_res - 1), 0)),
        ],
        out_specs=pl.BlockSpec((tm, D_MODEL), lambda i, c: (i, 0)),
        out_shape=jax.ShapeDtypeStruct((t, D_MODEL), F32),
        compiler_params=pltpu.CompilerParams(
            dimension_semantics=("arbitrary", "arbitrary"),
            vmem_limit_bytes=VMEM_LIMIT_BYTES),
        name="ffn",
    )(h2, w_g, w_u, w_d, x1)


ATTN_GROUPS = 16
ATTN_TILE = ATTN_BLOCK * ATTN_GROUPS
TILE_LANES = ATTN_GROUPS * HEAD_DIM
F32_SUBLANES = 8
BF16_SUBLANES = 16
STAGE_PITCH = 24


def _band_mask(slab, n_slab, span):
    blk = ATTN_BLOCK
    shift = slab.bit_length() - 1
    row = lax.broadcasted_iota(jnp.int32, (blk, 2 * blk), 0)
    col = lax.broadcasted_iota(jnp.int32, (blk, 2 * blk), 1)

    def pos(r):
        return (r & (slab - 1)) * n_slab + lax.shift_right_logical(r, shift)

    k_rel = pos(col & (blk - 1)) - jnp.where(col < blk, blk, 0)
    dist = pos(row) - k_rel
    return jnp.logical_and(dist >= 0, dist <= span), col


def _attn_body(q_in, k_in, v_in, o_ref,
               acc_scr, m_scr, l_scr, stage_q, stage_k, stage_v,
               q_ref, k_ref, v_ref, qf_scr, kf_scr, vf_scr,
               kp_ref, vp_ref, kpf_scr, vpf_scr, *, depth):
    jt = pl.program_id(2)
    blk = ATTN_BLOCK
    ones_v = jnp.ones((2 * blk, HEAD_DIM), BF16)

    def lanes(g):
        return slice(g * HEAD_DIM, (g + 1) * HEAD_DIM)

    @pl.when(jt == 0)
    def _():
        kp_ref[...] = jnp.zeros_like(kp_ref)
        vp_ref[...] = jnp.zeros_like(vp_ref)
        kpf_scr[...] = jnp.zeros_like(kpf_scr)
        vpf_scr[...] = jnp.zeros_like(vpf_scr)

    for src, stage, dst_bf16, dst_f32 in ((q_in, stage_q, q_ref, qf_scr),
                                          (k_in, stage_k, k_ref, kf_scr),
                                          (v_in, stage_v, v_ref, vf_scr)):
        for i in range(blk):
            stage[STAGE_PITCH * i:STAGE_PITCH * i + ATTN_GROUPS, :] = (
                src[ATTN_GROUPS * i:ATTN_GROUPS * (i + 1), :].astype(F32))
        for g in range(ATTN_GROUPS):
            cls = stage[pl.ds(g, blk, stride=STAGE_PITCH), :]
            dst_f32[:, lanes(g)] = cls
            dst_bf16[:, lanes(g)] = cls.astype(BF16)

    n_pat = len(ATTN_PATTERNS)
    for pi, (window, dil) in enumerate(reversed(ATTN_PATTERNS)):
        span = window // dil
        n_slab = ATTN_GROUPS // dil
        slab = blk // n_slab
        use_f32 = slab % BF16_SUBLANES != 0
        band, col = _band_mask(slab, n_slab, span)
        band_first = jnp.logical_and(band, jnp.logical_or(col >= blk, jt > 0))

        def gather(cur_ref, cur_f32, prev_ref, prev_f32, res, kb, prev):
            if not prev:
                src, rows = (cur_f32 if use_f32 else cur_ref), slice(slab * kb, slab * (kb + 1))
            elif kb > 0:
                src, rows = (cur_f32 if use_f32 else cur_ref), slice(slab * (kb - 1), slab * kb)
            elif use_f32:
                src, rows = prev_f32, slice(F32_SUBLANES - slab, F32_SUBLANES)
            else:
                src, rows = prev_ref, slice(blk - slab, blk)
            parts = [src[rows, lanes(u * dil + res)] for u in range(n_slab)]
            out = parts[0] if n_slab == 1 else jnp.concatenate(parts, axis=0)
            return out.astype(BF16)

        def st_load(scr, res, kb):
            rows = slice(slab * kb, slab * (kb + 1))
            parts = [scr[rows, lanes(u * dil + res)] for u in range(n_slab)]
            return parts[0] if n_slab == 1 else jnp.concatenate(parts, axis=0)

        def st_store(scr, res, kb, val):
            rows = slice(slab * kb, slab * (kb + 1))
            for u in range(n_slab):
                scr[rows, lanes(u * dil + res)] = val[u * slab:(u + 1) * slab, :]

        def scores_of(res, kb):
            q = gather(q_ref, qf_scr, None, None, res, kb, False)
            kw = jnp.concatenate([gather(k_ref, kf_scr, kp_ref, kpf_scr, res, kb, True),
                                  gather(k_ref, kf_scr, kp_ref, kpf_scr, res, kb, False)], axis=0)
            s = lax.dot_general(q, kw, (((1,), (1,)), ((), ())), preferred_element_type=F32)
            return jnp.where(band_first if kb == 0 else band, s, MASK_VALUE)

        def finish(res, kb, s):
            m = jnp.max(s, axis=-1, keepdims=True)
            p = jnp.exp2(s - m).astype(BF16)
            vw = jnp.concatenate([gather(v_ref, vf_scr, vp_ref, vpf_scr, res, kb, True),
                                  gather(v_ref, vf_scr, vp_ref, vpf_scr, res, kb, False)], axis=0)
            pv = jnp.dot(p, jnp.concatenate([vw, ones_v], axis=1), preferred_element_type=F32)
            num, den = pv[:, :HEAD_DIM], pv[:, HEAD_DIM:]
            m_new = jnp.broadcast_to(m, (blk, HEAD_DIM))
            if pi > 0:
                m_old = st_load(m_scr, res, kb)
                m_new = jnp.maximum(m_old, m_new)
                w_old = jnp.exp2(m_old - m_new)
                w_new = jnp.exp2(m - m_new)
                num = st_load(acc_scr, res, kb) * w_old + num * w_new
                den = st_load(l_scr, res, kb) * w_old + den * w_new
            if pi < n_pat - 1:
                st_store(acc_scr, res, kb, num)
                st_store(m_scr, res, kb, m_new)
                st_store(l_scr, res, kb, den)
            else:
                st_store(acc_scr, res, kb, num / den)

        blocks = [(res, kb) for res in range(dil) for kb in range(n_slab)]
        pending = []
        for t in range(len(blocks) + depth):
            if t < len(blocks):
                pending.append(scores_of(*blocks[t]))
            if t >= depth:
                finish(*blocks[t - depth], pending.pop(0))

    for g in range(ATTN_GROUPS):
        stage_q[pl.ds(g, blk, stride=STAGE_PITCH), :] = acc_scr[:, lanes(g)]
    for i in range(blk):
        o_ref[ATTN_GROUPS * i:ATTN_GROUPS * (i + 1), :] = (
            stage_q[STAGE_PITCH * i:STAGE_PITCH * i + ATTN_GROUPS, :].astype(BF16))

    kp_ref[...] = k_ref[...]
    vp_ref[...] = v_ref[...]
    kpf_scr[...] = kf_scr[blk - F32_SUBLANES:, :]
    vpf_scr[...] = vf_scr[blk - F32_SUBLANES:, :]


def _dilated_attention(att, batch, seq, depth=3):
    assert seq % ATTN_TILE == 0
    n_tile = seq // ATTN_TILE

    def spec(which):
        return pl.BlockSpec((None, None, ATTN_TILE, HEAD_DIM),
                            lambda bi, hd, jt: (which, hd, bi * n_tile + jt, 0))

    tile_f32 = pltpu.VMEM((ATTN_BLOCK, TILE_LANES), F32)
    tile_bf16 = pltpu.VMEM((ATTN_BLOCK, TILE_LANES), BF16)
    tail_f32 = pltpu.VMEM((F32_SUBLANES, TILE_LANES), F32)
    stage = pltpu.VMEM((ATTN_BLOCK * STAGE_PITCH, HEAD_DIM), F32)
    return pl.pallas_call(
        functools.partial(_attn_body, depth=depth),
        grid=(batch, ATTN_HEADS, n_tile),
        in_specs=[spec(0), spec(1), spec(2)],
        out_specs=pl.BlockSpec((None, ATTN_TILE, HEAD_DIM), lambda bi, hd, jt: (hd, bi * n_tile + jt, 0)),
        out_shape=jax.ShapeDtypeStruct((ATTN_HEADS, batch * seq, HEAD_DIM), BF16),
        scratch_shapes=[tile_f32, tile_f32, tile_f32,
                        stage, stage, stage,
                        tile_bf16, tile_bf16, tile_bf16,
                        tile_f32, tile_f32, tile_f32,
                        tile_bf16, tile_bf16,
                        tail_f32, tail_f32],
        compiler_params=pltpu.CompilerParams(
            dimension_semantics=("arbitrary", "arbitrary", "arbitrary"),
            vmem_limit_bytes=VMEM_LIMIT_BYTES),
        name="dilated_attn",
    )(att, att, att)


CONV_HALO = 8


def _mlstm_chunk(gt, qb, kf, v_ref, mo_ref, ng_ref, o_ref, rows, cn_scr, m_scr):
    L = MLSTM_CHUNK

    ig = gt[:MLSTM_HEADS]
    fg = gt[MLSTM_HEADS:]
    lf = jnp.minimum(fg, 0.0) - jnp.log1p(jnp.exp(-jnp.abs(fg)))
    lane = lax.broadcasted_iota(jnp.int32, (MLSTM_HEADS, L), 1)
    b = lf
    s = 1
    while s < L:
        b = b + jnp.where(lane >= s, pltpu.roll(b, s, 1), 0.0)
        s *= 2
    c_row = ig - b
    b_last = b[:, L - 1:L]
    c_max = jnp.max(c_row, axis=1, keepdims=True)
    m_prev = m_scr[:, :1]
    m_chunk = b_last + c_max
    m_new = jnp.maximum(b_last + m_prev, m_chunk)
    decay = jnp.exp(b_last + m_prev - m_new)
    scale = jnp.exp(m_chunk - m_new)
    wa = jnp.exp(c_row - c_max)
    m_scr[...] = jnp.broadcast_to(m_new, m_scr.shape)

    row = lax.broadcasted_iota(jnp.int32, (L, L), 0)
    col = lax.broadcasted_iota(jnp.int32, (L, L), 1)
    causal = col <= row
    ones_v = jnp.ones((L, HEAD_DIM), BF16)

    heads = range(MLSTM_HEADS)

    def hs(hd):
        return slice(hd * HEAD_DIM, (hd + 1) * HEAD_DIM)

    def one(hd):
        return slice(hd, hd + 1)

    k_ts, states, s_qks, inters = [], [], [], []
    for hd in heads:
        q_h = qb[:, hs(hd)]
        k_t = kf[:, hs(hd)].T
        state = cn_scr[hd]
        k_ts.append(k_t)
        states.append(state)
        s_qks.append(jnp.dot(q_h, k_t.astype(BF16), preferred_element_type=F32))
        inters.append(jnp.dot(q_h, state.astype(BF16), preferred_element_type=F32))

    ws, mus, gs = [], [], []
    for hd in heads:
        c_b = jnp.where(causal, jnp.broadcast_to(c_row[one(hd), :], (L, L)), MASK_VALUE)
        mu = jnp.maximum(jnp.max(c_b, axis=-1, keepdims=True), m_prev[one(hd), :])
        ws.append(jnp.exp(c_b - mu) * s_qks[hd])
        mus.append(mu)
        gs.append(jnp.exp(m_prev[one(hd), :] - mu))

    ones_sq = jnp.ones((L, L), BF16)

    def split(x):
        hi = x.astype(BF16)
        return hi, (x - hi.astype(F32)).astype(BF16)

    lf_rows = jnp.concatenate([jnp.broadcast_to(lf[one(hd), :], (HEAD_DIM, L)) for hd in heads], axis=0)
    causal_b = jnp.where(causal, 1.0, 0.0).astype(BF16)
    nt = (((1,), (1,)), ((), ()))
    lf_hi, lf_lo = split(lf_rows)
    b_all = (lax.dot_general(causal_b, lf_hi, nt, preferred_element_type=F32)
             + lax.dot_general(causal_b, lf_lo, nt, preferred_element_type=F32))

    intras, kvns, sum_los = [], [], []
    for hd in heads:
        v_aug = jnp.concatenate([v_ref[rows, hs(hd)], ones_v], axis=1)
        w_hi, w_lo = split(ws[hd])
        intras.append(jnp.dot(w_hi, v_aug, preferred_element_type=F32))
        sum_los.append(jnp.dot(w_lo, ones_sq, preferred_element_type=F32))
        a_t = (k_ts[hd] * jnp.broadcast_to(wa[one(hd), :], (HEAD_DIM, L))).astype(BF16)
        kvns.append(jnp.dot(a_t, v_aug, preferred_element_type=F32))

    cells = []
    for hd in heads:
        g, mu = gs[hd], mus[hd]
        num = g * inters[hd][:, :HEAD_DIM] + intras[hd][:, :HEAD_DIM]
        den = g * inters[hd][:, HEAD_DIM:] + intras[hd][:, HEAD_DIM:] + sum_los[hd]
        cells.append(num / jnp.maximum(jnp.abs(den), jnp.exp(-(b_all[:, hs(hd)] + mu))))

    sq_sums = []
    for hd in heads:
        sq = (cells[hd] * cells[hd]).astype(BF16)
        sq_sums.append(jnp.dot(sq, ones_sq, preferred_element_type=F32))
    for hd in heads:
        y = cells[hd] * lax.rsqrt(sq_sums[hd] * (1.0 / HEAD_DIM) + EPS) * ng_ref[:, hs(hd)]
        o_ref[rows, hs(hd)] = (jax.nn.sigmoid(mo_ref[rows, hs(hd)]) * y).astype(BF16)
        cn_scr[hd] = decay[one(hd), :] * states[hd] + scale[one(hd), :] * kvns[hd]


MIX_TILE = 4 * MLSTM_CHUNK


def _mix_out_body(gt_ref, q_ref, k_ref, qp_ref, kp_ref, v_ref, mo_ref, cw_ref, cb_ref, ng_ref,
                  a_ref, x_ref, wo_ref, gf_ref, x1_ref, h2_ref,
                  cn_scr, m_scr, qext_scr, kext_scr, ml_cur, ml_prev, *, tiles_per_seq, n_tiles):
    s = pl.program_id(0)
    L = MLSTM_CHUNK
    W = MLSTM_WIDTH
    n_chunk = MIX_TILE // L
    tile = jnp.minimum(s, n_tiles - 1)
    seq_start = (tile % tiles_per_seq) == 0

    @pl.when(s == 0)
    def _():
        ml_prev[...] = jnp.zeros_like(ml_prev)

    @pl.when(seq_start)
    def _():
        cn_scr[...] = jnp.zeros_like(cn_scr)
        m_scr[...] = jnp.full_like(m_scr, M_INIT)

    has_prev = jnp.logical_not(seq_start)

    def conv_silu(cur_ref, prev_ref, ext, col0):
        ext[:CONV_HALO, :] = jnp.where(has_prev, prev_ref[...], 0.0)
        ext[CONV_HALO:, :] = cur_ref[...]
        y = cb_ref[:, col0:col0 + W]
        for sh in range(CONV_WIDTH):
            tap = cw_ref[CONV_WIDTH - 1 - sh:CONV_WIDTH - sh, col0:col0 + W]
            y = y + ext[CONV_HALO - sh:CONV_HALO - sh + MIX_TILE, :] * tap
        return y * jax.nn.sigmoid(y)

    qb = conv_silu(q_ref, qp_ref, qext_scr, 0).astype(BF16)
    kf = conv_silu(k_ref, kp_ref, kext_scr, W) * (HEAD_DIM ** -0.5)

    mix_prev = jnp.concatenate([a_ref[hd] for hd in range(ATTN_HEADS)] + [ml_prev[...]], axis=1)
    n_slab = D_MODEL // n_chunk
    for ck in range(n_chunk):
        cols = slice(ck * n_slab, (ck + 1) * n_slab)
        x1_ref[:, cols] = (jnp.dot(mix_prev, wo_ref[:, cols], preferred_element_type=F32)
                           + x_ref[:, cols])
        rows = slice(ck * L, (ck + 1) * L)
        _mlstm_chunk(gt_ref[:, rows], qb[rows, :], kf[rows, :], v_ref, mo_ref, ng_ref, ml_cur, rows,
                     cn_scr, m_scr)

    y = x1_ref[...]
    ms = jnp.mean(y * y, axis=-1, keepdims=True)
    h2_ref[...] = (y * lax.rsqrt(ms + EPS) * gf_ref[...]).astype(BF16)
    ml_prev[...] = ml_cur[...]


def _mix_out(gates_t, m_f32, m_v, conv_w, conv_b, norm_g, attn, x2, w_o, g_ffn, batch, seq):
    t = batch * seq
    W = MLSTM_WIDTH
    assert seq % MIX_TILE == 0
    tiles_per_seq = seq // MIX_TILE
    n_tiles = t // MIX_TILE
    halo_per_tile = MIX_TILE // CONV_HALO

    def cur(s):
        return jnp.minimum(s, n_tiles - 1)

    def prev(s):
        return jnp.maximum(s - 1, 0)

    def halo(s):
        return jnp.maximum(cur(s) * halo_per_tile - 1, 0)

    const = lambda s: (0, 0)
    return pl.pallas_call(
        functools.partial(_mix_out_body, tiles_per_seq=tiles_per_seq, n_tiles=n_tiles),
        grid=(n_tiles + 1,),
        in_specs=[
            pl.BlockSpec((N_GATE, MIX_TILE), lambda s: (0, cur(s))),
            pl.BlockSpec((MIX_TILE, W), lambda s: (cur(s), 0)),
            pl.BlockSpec((MIX_TILE, W), lambda s: (cur(s), 1)),
            pl.BlockSpec((CONV_HALO, W), lambda s: (halo(s), 0)),
            pl.BlockSpec((CONV_HALO, W), lambda s: (halo(s), 1)),
            pl.BlockSpec((MIX_TILE, W), lambda s: (cur(s), 0)),
            pl.BlockSpec((MIX_TILE, W), lambda s: (cur(s), 2)),
            pl.BlockSpec((CONV_WIDTH, 2 * W), const),
            pl.BlockSpec((1, 2 * W), const),
            pl.BlockSpec((1, W), const),
            pl.BlockSpec((ATTN_HEADS, MIX_TILE, HEAD_DIM), lambda s: (0, prev(s), 0)),
            pl.BlockSpec((MIX_TILE, D_MODEL), lambda s: (prev(s), 0)),
            pl.BlockSpec((ATTN_WIDTH + MLSTM_WIDTH, D_MODEL), const, pipeline_mode=pl.Buffered(1)),
            pl.BlockSpec((1, D_MODEL), const),
        ],
        out_specs=[
            pl.BlockSpec((MIX_TILE, D_MODEL), lambda s: (prev(s), 0)),
            pl.BlockSpec((MIX_TILE, D_MODEL), lambda s: (prev(s), 0)),
        ],
        out_shape=[
            jax.ShapeDtypeStruct((t, D_MODEL), F32),
            jax.ShapeDtypeStruct((t, D_MODEL), BF16),
        ],
        scratch_shapes=[pltpu.VMEM((MLSTM_HEADS, HEAD_DIM, 2 * HEAD_DIM), F32),
                        pltpu.VMEM((MLSTM_HEADS, LANES), F32),
                        pltpu.VMEM((CONV_HALO + MIX_TILE, W), F32),
                        pltpu.VMEM((CONV_HALO + MIX_TILE, W), F32),
                        pltpu.VMEM((MIX_TILE, W), BF16),
                        pltpu.VMEM((MIX_TILE, W), BF16)],
        compiler_params=pltpu.CompilerParams(
            dimension_semantics=("arbitrary",),
            vmem_limit_bytes=VMEM_LIMIT_BYTES),
        name="mlstm_out_proj",
    )(gates_t, m_f32, m_f32, m_f32, m_f32, m_v, m_f32, conv_w, conv_b, norm_g,
      attn, x2, w_o, g_ffn)


def kernel(x, norm_mix_g, w_in, conv_w, conv_b, gate_b, q_norm_g, k_norm_g, mlstm_norm_g,
           w_out, norm_ffn_g, w_gate, w_up, w_down):
    B, S, _ = x.shape
    t = B * S
    layer = 0
    x2 = x.reshape(t, D_MODEL)

    w_t = w_in[layer].T
    a3 = 3 * ATTN_WIDTH
    mw = MLSTM_WIDTH
    n_main = a3 + 4 * mw
    w_main = _cast_rows_bf16(w_t, n_main, 512)
    w_gate_t = w_t[n_main:]
    b_gate = gate_b[layer].reshape(N_GATE, 1)
    head_gain = jnp.stack([
        jnp.tile(q_norm_g[layer] * (HEAD_DIM ** -0.5 * LOG2_E), ATTN_HEADS),
        jnp.tile(k_norm_g[layer], ATTN_HEADS)]).reshape(2, 1, ATTN_WIDTH)

    att, m_f32, m_v, gates_t, wo_b, wg_b, wu_b, wd_b = _in_proj(
        x2, norm_mix_g[layer].reshape(1, D_MODEL), w_main, w_gate_t, b_gate, head_gain,
        (w_out[layer], w_gate[layer], w_up[layer], w_down[layer]), tm=1024)

    attn_out = _dilated_attention(att, B, S)

    x1, h2 = _mix_out(gates_t, m_f32, m_v, conv_w[layer], conv_b[layer].reshape(1, 2 * mw),
                      mlstm_norm_g[layer].reshape(1, mw), attn_out, x2, wo_b,
                      norm_ffn_g[layer].reshape(1, D_MODEL), B, S)

    out = _ffn(h2, wg_b, wu_b, wd_b, x1, tm=1024, th=512)
    return out.reshape(B, S, D_MODEL)
```

```python
import functools

import jax
import jax.numpy as jnp
from jax import lax
from jax.experimental import pallas as pl
from jax.experimental.pallas import tpu as pltpu

D_MODEL = 2048
HEAD_DIM = 128
ATTN_HEADS = 8
MLSTM_HEADS = 8
ATTN_WIDTH = ATTN_HEADS * HEAD_DIM
MLSTM_WIDTH = MLSTM_HEADS * HEAD_DIM
ATTN_PATTERNS = ((128, 1), (512, 4), (2048, 16))
ATTN_BLOCK = 128
MLSTM_CHUNK = 128
CONV_WIDTH = 4
FFN_HIDDEN = 5632
EPS = 1e-6
MASK_VALUE = -1e30
M_INIT = -1e30

LOG2_E = 1.4426950408889634
N_GATE = 2 * MLSTM_HEADS
LANES = 128
V7X_VMEM_BYTES = 64 * 1024 * 1024
VMEM_LIMIT_BYTES = V7X_VMEM_BYTES - 5 * 1024 * 1024

F32 = jnp.float32
BF16 = jnp.bfloat16


def _cast_body(w_ref, o_ref):
    o_ref[...] = w_ref[...].astype(BF16)


def _cast_rows_bf16(w, n_rows, tr):
    cols = w.shape[1]
    return pl.pallas_call(
        _cast_body,
        grid=(n_rows // tr,),
        in_specs=[pl.BlockSpec((tr, cols), lambda j: (j, 0))],
        out_specs=pl.BlockSpec((tr, cols), lambda j: (j, 0)),
        out_shape=jax.ShapeDtypeStruct((n_rows, cols), BF16),
        compiler_params=pltpu.CompilerParams(
            dimension_semantics=("arbitrary",),
            vmem_limit_bytes=VMEM_LIMIT_BYTES),
        name="cast_w_in",
    )(w)


def _in_proj_body(x_ref, g_ref, w_ref, wg_ref, bg_ref, hg_ref, *rest, n_cast):
    cast_in, rest = rest[:n_cast], rest[n_cast:]
    oatt_ref, omf_ref, omv_ref, ogate_ref = rest[:4]
    cast_out, h_scr = rest[4:4 + n_cast], rest[4 + n_cast]
    j = pl.program_id(1)

    for src, dst in zip(cast_in, cast_out):
        dst[...] = src[...].astype(BF16)

    nt = (((1,), (1,)), ((), ()))

    def project(h=None):
        h = h_scr[...] if h is None else h
        return lax.dot_general(h, w_ref[...], nt, preferred_element_type=F32)

    def store_head_normed(acc):
        for hd in range(ATTN_HEADS):
            sl = slice(hd * HEAD_DIM, (hd + 1) * HEAD_DIM)
            a = acc[:, sl]
            ms = jnp.mean(a * a, axis=-1, keepdims=True)
            oatt_ref[hd] = (a * lax.rsqrt(ms + EPS) * hg_ref[:, sl]).astype(BF16)

    @pl.when(j == 0)
    def _():
        x = x_ref[...]
        ms = jnp.mean(x * x, axis=-1, keepdims=True)
        hb = (x * lax.rsqrt(ms + EPS) * g_ref[...]).astype(BF16)
        h_scr[...] = hb
        ogate_ref[...] = lax.dot_general(wg_ref[...].astype(BF16), hb, nt,
                                         preferred_element_type=F32) + bg_ref[...]
        store_head_normed(project(hb))

    @pl.when(j == 1)
    def _():
        store_head_normed(project())

    @pl.when(j == 2)
    def _():
        acc = project()
        for hd in range(ATTN_HEADS):
            oatt_ref[hd] = acc[:, hd * HEAD_DIM:(hd + 1) * HEAD_DIM].astype(BF16)

    @pl.when(jnp.logical_and(j >= 3, j < 6))
    def _():
        omf_ref[...] = project()

    @pl.when(j == 6)
    def _():
        omv_ref[...] = project().astype(BF16)


def _in_proj(x2, g_mix, w_main, w_gate_t, b_gate, head_gain, side_weights, tm):
    t = x2.shape[0]
    tn = ATTN_WIDTH
    n_col = w_main.shape[0] // tn
    n_step = (t // tm) * n_col

    def w_col(j):
        return jnp.where(j == 5, 6, jnp.where(j == 6, 5, j))

    cast_specs, cast_shapes = [], []
    for wt in side_weights:
        rows, cols = wt.shape
        n_blk = max(n for n in range(1, n_step + 1)
                    if rows % n == 0 and (rows // n) % BF16_SUBLANES == 0)
        cast_specs.append(pl.BlockSpec(
            (rows // n_blk, cols), lambda i, j, n_blk=n_blk: (jnp.minimum(i * n_col + j, n_blk - 1), 0)))
        cast_shapes.append(jax.ShapeDtypeStruct(wt.shape, BF16))

    return pl.pallas_call(
        functools.partial(_in_proj_body, n_cast=len(side_weights)),
        grid=(t // tm, n_col),
        in_specs=[
            pl.BlockSpec((tm, D_MODEL), lambda i, j: (i, 0)),
            pl.BlockSpec((1, D_MODEL), lambda i, j: (0, 0)),
            pl.BlockSpec((tn, D_MODEL), lambda i, j: (w_col(j), 0)),
            pl.BlockSpec((N_GATE, D_MODEL), lambda i, j: (0, 0)),
            pl.BlockSpec((N_GATE, 1), lambda i, j: (0, 0)),
            pl.BlockSpec((None, 1, tn), lambda i, j: (jnp.minimum(j, 1), 0, 0)),
        ] + cast_specs,
        out_specs=[
            pl.BlockSpec((None, ATTN_HEADS, tm, HEAD_DIM), lambda i, j: (jnp.minimum(j, 2), 0, i, 0)),
            pl.BlockSpec((tm, tn), lambda i, j: (i, jnp.clip(j - 3, 0, 2))),
            pl.BlockSpec((tm, tn), lambda i, j: (i, 0)),
            pl.BlockSpec((N_GATE, tm), lambda i, j: (0, i)),
        ] + cast_specs,
        out_shape=[
            jax.ShapeDtypeStruct((3, ATTN_HEADS, t, HEAD_DIM), BF16),
            jax.ShapeDtypeStruct((t, 3 * tn), F32),
            jax.ShapeDtypeStruct((t, tn), BF16),
            jax.ShapeDtypeStruct((N_GATE, t), F32),
        ] + cast_shapes,
        scratch_shapes=[pltpu.VMEM((tm, D_MODEL), BF16)],
        compiler_params=pltpu.CompilerParams(
            dimension_semantics=("arbitrary", "arbitrary"),
            vmem_limit_bytes=VMEM_LIMIT_BYTES),
        name="in_proj",
    )(x2, g_mix, w_main, w_gate_t, b_gate, head_gain, *side_weights)


FFN_DOWN_CHUNK = 512


def _ffn_body(h_ref, wg_ref, wu_ref, wd_ref, x1_ref, o_ref, *, n_res):
    c = pl.program_id(1)

    @pl.when(c == 0)
    def _():
        o_ref[...] = jnp.zeros_like(o_ref)

    @pl.when(c < n_res)
    def _():
        slab = x1_ref.shape[0]
        rows = pl.ds(pl.multiple_of(c * slab, slab), slab)
        o_ref[rows, :] += x1_ref[...]

    h = h_ref[...]
    th = wg_ref.shape[1]
    acts = []
    for k0 in range(0, th, th // 2):
        kc = slice(k0, k0 + th // 2)
        g = jnp.dot(h, wg_ref[:, kc], preferred_element_type=F32)
        u = jnp.dot(h, wu_ref[:, kc], preferred_element_type=F32)
        acts.append((g * jax.nn.sigmoid(g) * u).astype(BF16))
    a = jnp.concatenate(acts, axis=1)
    for n0 in range(0, D_MODEL, FFN_DOWN_CHUNK):
        cols = slice(n0, n0 + FFN_DOWN_CHUNK)
        o_ref[:, cols] += jnp.dot(a, wd_ref[:, cols], preferred_element_type=F32)


def _ffn(h2, w_g, w_u, w_d, x1, tm, th):
    t = h2.shape[0]
    n_chunk = FFN_HIDDEN // th
    n_res = 8
    assert n_res <= n_chunk and tm % n_res == 0
    return pl.pallas_call(
        functools.partial(_ffn_body, n_res=n_res),
        grid=(t // tm, n_chunk),
        in_specs=[
            pl.BlockSpec((tm, D_MODEL), lambda i, c: (i, 0)),
            pl.BlockSpec((D_MODEL, th), lambda i, c: (0, c)),
            pl.BlockSpec((D_MODEL, th), lambda i, c: (0, c)),
            pl.BlockSpec((th, D_MODEL), lambda i, c: (c, 0)),
            pl.BlockSpec((tm // n_res, D_MODEL), lambda i, c: (i * n_res + jnp.minimum(c, n_res - 1), 0)),
        ],
        out_specs=pl.BlockSpec((tm, D_MODEL), lambda i, c: (i, 0)),
        out_shape=jax.ShapeDtypeStruct((t, D_MODEL), F32),
        compiler_params=pltpu.CompilerParams(
            dimension_semantics=("arbitrary", "arbitrary"),
            vmem_limit_bytes=VMEM_LIMIT_BYTES),
        name="ffn",
    )(h2, w_g, w_u, w_d, x1)


ATTN_GROUPS = 16
ATTN_TILE = ATTN_BLOCK * ATTN_GROUPS
TILE_LANES = ATTN_GROUPS * HEAD_DIM
F32_SUBLANES = 8
BF16_SUBLANES = 16
STAGE_PITCH = 24


def _band_mask(slab, n_slab, span):
    blk = ATTN_BLOCK
    shift = slab.bit_length() - 1
    row = lax.broadcasted_iota(jnp.int32, (blk, 2 * blk), 0)
    col = lax.broadcasted_iota(jnp.int32, (blk, 2 * blk), 1)

    def pos(r):
        return (r & (slab - 1)) * n_slab + lax.shift_right_logical(r, shift)

    k_rel = pos(col & (blk - 1)) - jnp.where(col < blk, blk, 0)
    dist = pos(row) - k_rel
    return jnp.logical_and(dist >= 0, dist <= span), col


def _attn_body(q_in, k_in, v_in, o_ref,
               acc_scr, m_scr, l_scr, stage_q, stage_k, stage_v,
               q_ref, k_ref, v_ref, qf_scr, kf_scr, vf_scr,
               kp_ref, vp_ref, kpf_scr, vpf_scr, *, depth):
    jt = pl.program_id(2)
    blk = ATTN_BLOCK
    ones_v = jnp.ones((2 * blk, HEAD_DIM), BF16)

    def lanes(g):
        return slice(g * HEAD_DIM, (g + 1) * HEAD_DIM)

    @pl.when(jt == 0)
    def _():
        kp_ref[...] = jnp.zeros_like(kp_ref)
        vp_ref[...] = jnp.zeros_like(vp_ref)
        kpf_scr[...] = jnp.zeros_like(kpf_scr)
        vpf_scr[...] = jnp.zeros_like(vpf_scr)

    for src, stage, dst_bf16, dst_f32 in ((q_in, stage_q, q_ref, qf_scr),
                                          (k_in, stage_k, k_ref, kf_scr),
                                          (v_in, stage_v, v_ref, vf_scr)):
        for i in range(blk):
            stage[STAGE_PITCH * i:STAGE_PITCH * i + ATTN_GROUPS, :] = (
                src[ATTN_GROUPS * i:ATTN_GROUPS * (i + 1), :].astype(F32))
        for g in range(ATTN_GROUPS):
            cls = stage[pl.ds(g, blk, stride=STAGE_PITCH), :]
            dst_f32[:, lanes(g)] = cls
            dst_bf16[:, lanes(g)] = cls.astype(BF16)

    n_pat = len(ATTN_PATTERNS)
    for pi, (window, dil) in enumerate(reversed(ATTN_PATTERNS)):
        span = window // dil
        n_slab = ATTN_GROUPS // dil
        slab = blk // n_slab
        use_f32 = slab % BF16_SUBLANES != 0
        band, col = _band_mask(slab, n_slab, span)
        band_first = jnp.logical_and(band, jnp.logical_or(col >= blk, jt > 0))

        def gather(cur_ref, cur_f32, prev_ref, prev_f32, res, kb, prev):
            if not prev:
                src, rows = (cur_f32 if use_f32 else cur_ref), slice(slab * kb, slab * (kb + 1))
            elif kb > 0:
                src, rows = (cur_f32 if use_f32 else cur_ref), slice(slab * (kb - 1), slab * kb)
            elif use_f32:
                src, rows = prev_f32, slice(F32_SUBLANES - slab, F32_SUBLANES)
            else:
                src, rows = prev_ref, slice(blk - slab, blk)
            parts = [src[rows, lanes(u * dil + res)] for u in range(n_slab)]
            out = parts[0] if n_slab == 1 else jnp.concatenate(parts, axis=0)
            return out.astype(BF16)

        def st_load(scr, res, kb):
            rows = slice(slab * kb, slab * (kb + 1))
            parts = [scr[rows, lanes(u * dil + res)] for u in range(n_slab)]
            return parts[0] if n_slab == 1 else jnp.concatenate(parts, axis=0)

        def st_store(scr, res, kb, val):
            rows = slice(slab * kb, slab * (kb + 1))
            for u in range(n_slab):
                scr[rows, lanes(u * dil + res)] = val[u * slab:(u + 1) * slab, :]

        def scores_of(res, kb):
            q = gather(q_ref, qf_scr, None, None, res, kb, False)
            kw = jnp.concatenate([gather(k_ref, kf_scr, kp_ref, kpf_scr, res, kb, True),
                                  gather(k_ref, kf_scr, kp_ref, kpf_scr, res, kb, False)], axis=0)
            s = lax.dot_general(q, kw, (((1,), (1,)), ((), ())), preferred_element_type=F32)
            return jnp.where(band_first if kb == 0 else band, s, MASK_VALUE)

        def finish(res, kb, s):
            m = jnp.max(s, axis=-1, keepdims=True)
            p = jnp.exp2(s - m).astype(BF16)
            vw = jnp.concatenate([gather(v_ref, vf_scr, vp_ref, vpf_scr, res, kb, True),
                                  gather(v_ref, vf_scr, vp_ref, vpf_scr, res, kb, False)], axis=0)
            pv = jnp.dot(p, jnp.concatenate([vw, ones_v], axis=1), preferred_element_type=F32)
            num, den = pv[:, :HEAD_DIM], pv[:, HEAD_DIM:]
            m_new = jnp.broadcast_to(m, (blk, HEAD_DIM))
            if pi > 0:
                m_old = st_load(m_scr, res, kb)
                m_new = jnp.maximum(m_old, m_new)
                w_old = jnp.exp2(m_old - m_new)
                w_new = jnp.exp2(m - m_new)
                num = st_load(acc_scr, res, kb) * w_old + num * w_new
                den = st_load(l_scr, res, kb) * w_old + den * w_new
            if pi < n_pat - 1:
                st_store(acc_scr, res, kb, num)
                st_store(m_scr, res, kb, m_new)
                st_store(l_scr, res, kb, den)
            else:
                st_store(acc_scr, res, kb, num / den)

        blocks = [(res, kb) for res in range(dil) for kb in range(n_slab)]
        pending = []
        for t in range(len(blocks) + depth):
            if t < len(blocks):
                pending.append(scores_of(*blocks[t]))
            if t >= depth:
                finish(*blocks[t - depth], pending.pop(0))

    for g in range(ATTN_GROUPS):
        stage_q[pl.ds(g, blk, stride=STAGE_PITCH), :] = acc_scr[:, lanes(g)]
    for i in range(blk):
        o_ref[ATTN_GROUPS * i:ATTN_GROUPS * (i + 1), :] = (
            stage_q[STAGE_PITCH * i:STAGE_PITCH * i + ATTN_GROUPS, :].astype(BF16))

    kp_ref[...] = k_ref[...]
    vp_ref[...] = v_ref[...]
    kpf_scr[...] = kf_scr[blk - F32_SUBLANES:, :]
    vpf_scr[...] = vf_scr[blk - F32_SUBLANES:, :]


def _dilated_attention(att, batch, seq, depth=3):
    assert seq % ATTN_TILE == 0
    n_tile = seq // ATTN_TILE

    def spec(which):
        return pl.BlockSpec((None, None, ATTN_TILE, HEAD_DIM),
                            lambda bi, hd, jt: (which, hd, bi * n_tile + jt, 0))

    tile_f32 = pltpu.VMEM((ATTN_BLOCK, TILE_LANES), F32)
    tile_bf16 = pltpu.VMEM((ATTN_BLOCK, TILE_LANES), BF16)
    tail_f32 = pltpu.VMEM((F32_SUBLANES, TILE_LANES), F32)
    stage = pltpu.VMEM((ATTN_BLOCK * STAGE_PITCH, HEAD_DIM), F32)
    return pl.pallas_call(
        functools.partial(_attn_body, depth=depth),
        grid=(batch, ATTN_HEADS, n_tile),
        in_specs=[spec(0), spec(1), spec(2)],
        out_specs=pl.BlockSpec((None, ATTN_TILE, HEAD_DIM), lambda bi, hd, jt: (hd, bi * n_tile + jt, 0)),
        out_shape=jax.ShapeDtypeStruct((ATTN_HEADS, batch * seq, HEAD_DIM), BF16),
        scratch_shapes=[tile_f32, tile_f32, tile_f32,
                        stage, stage, stage,
                        tile_bf16, tile_bf16, tile_bf16,
                        tile_f32, tile_f32, tile_f32,
                        tile_bf16, tile_bf16,
                        tail_f32, tail_f32],
        compiler_params=pltpu.CompilerParams(
            dimension_semantics=("arbitrary", "arbitrary", "arbitrary"),
            vmem_limit_bytes=VMEM_LIMIT_BYTES),
        name="dilated_attn",
    )(att, att, att)


CONV_HALO = 8


def _mlstm_chunk(gt, qb, kf, v_ref, mo_ref, ng_ref, o_ref, rows, cn_scr, m_scr):
    L = MLSTM_CHUNK

    ig = gt[:MLSTM_HEADS]
    fg = gt[MLSTM_HEADS:]
    lf = jnp.minimum(fg, 0.0) - jnp.log1p(jnp.exp(-jnp.abs(fg)))
    lane = lax.broadcasted_iota(jnp.int32, (MLSTM_HEADS, L), 1)
    b = lf
    s = 1
    while s < L:
        b = b + jnp.where(lane >= s, pltpu.roll(b, s, 1), 0.0)
        s *= 2
    c_row = ig - b
    b_last = b[:, L - 1:L]
    c_max = jnp.max(c_row, axis=1, keepdims=True)
    m_prev = m_scr[:, :1]
    m_chunk = b_last + c_max
    m_new = jnp.maximum(b_last + m_prev, m_chunk)
    decay = jnp.exp(b_last + m_prev - m_new)
    scale = jnp.exp(m_chunk - m_new)
    wa = jnp.exp(c_row - c_max)
    m_scr[...] = jnp.broadcast_to(m_new, m_scr.shape)

    row = lax.broadcasted_iota(jnp.int32, (L, L), 0)
    col = lax.broadcasted_iota(jnp.int32, (L, L), 1)
    causal = col <= row
    ones_v = jnp.ones((L, HEAD_DIM), BF16)

    heads = range(MLSTM_HEADS)

    def hs(hd):
        return slice(hd * HEAD_DIM, (hd + 1) * HEAD_DIM)

    def one(hd):
        return slice(hd, hd + 1)

    k_ts, states, s_qks, inters = [], [], [], []
    for hd in heads:
        q_h = qb[:, hs(hd)]
        k_t = kf[:, hs(hd)].T
        state = cn_scr[hd]
        k_ts.append(k_t)
        states.append(state)
        s_qks.append(jnp.dot(q_h, k_t.astype(BF16), preferred_element_type=F32))
        inters.append(jnp.dot(q_h, state.astype(BF16), preferred_element_type=F32))

    ws, mus, gs = [], [], []
    for hd in heads:
        c_b = jnp.where(causal, jnp.broadcast_to(c_row[one(hd), :], (L, L)), MASK_VALUE)
        mu = jnp.maximum(jnp.max(c_b, axis=-1, keepdims=True), m_prev[one(hd), :])
        ws.append(jnp.exp(c_b - mu) * s_qks[hd])
        mus.append(mu)
        gs.append(jnp.exp(m_prev[one(hd), :] - mu))

    ones_sq = jnp.ones((L, L), BF16)

    def split(x):
        hi = x.astype(BF16)
        return hi, (x - hi.astype(F32)).astype(BF16)

    lf_rows = jnp.concatenate([jnp.broadcast_to(lf[one(hd), :], (HEAD_DIM, L)) for hd in heads], axis=0)
    causal_b = jnp.where(causal, 1.0, 0.0).astype(BF16)
    nt = (((1,), (1,)), ((), ()))
    lf_hi, lf_lo = split(lf_rows)
    b_all = (lax.dot_general(causal_b, lf_hi, nt, preferred_element_type=F32)
             + lax.dot_general(causal_b, lf_lo, nt, preferred_element_type=F32))

    intras, kvns, sum_los = [], [], []
    for hd in heads:
        v_aug = jnp.concatenate([v_ref[rows, hs(hd)], ones_v], axis=1)
        w_hi, w_lo = split(ws[hd])
        intras.append(jnp.dot(w_hi, v_aug, preferred_element_type=F32))
        sum_los.append(jnp.dot(w_lo, ones_sq, preferred_element_type=F32))
        a_t = (k_ts[hd] * jnp.broadcast_to(wa[one(hd), :], (HEAD_DIM, L))).astype(BF16)
        kvns.append(jnp.dot(a_t, v_aug, preferred_element_type=F32))

    cells = []
    for hd in heads:
        g, mu = gs[hd], mus[hd]
        num = g * inters[hd][:, :HEAD_DIM] + intras[hd][:, :HEAD_DIM]
        den = g * inters[hd][:, HEAD_DIM:] + intras[hd][:, HEAD_DIM:] + sum_los[hd]
        cells.append(num / jnp.maximum(jnp.abs(den), jnp.exp(-(b_all[:, hs(hd)] + mu))))

    sq_sums = []
    for hd in heads:
        sq = (cells[hd] * cells[hd]).astype(BF16)
        sq_sums.append(jnp.dot(sq, ones_sq, preferred_element_type=F32))
    for hd in heads:
        y = cells[hd] * lax.rsqrt(sq_sums[hd] * (1.0 / HEAD_DIM) + EPS) * ng_ref[:, hs(hd)]
        o_ref[rows, hs(hd)] = (jax.nn.sigmoid(mo_ref[rows, hs(hd)]) * y).astype(BF16)
        cn_scr[hd] = decay[one(hd), :] * states[hd] + scale[one(hd), :] * kvns[hd]


MIX_TILE = 4 * MLSTM_CHUNK


def _mix_out_body(gt_ref, q_ref, k_ref, qp_ref, kp_ref, v_ref, mo_ref, cw_ref, cb_ref, ng_ref,
                  a_ref, x_ref, wo_ref, gf_ref, x1_ref, h2_ref,
                  cn_scr, m_scr, qext_scr, kext_scr, ml_cur, ml_prev, *, tiles_per_seq, n_tiles):
    s = pl.program_id(0)
    L = MLSTM_CHUNK
    W = MLSTM_WIDTH
    n_chunk = MIX_TILE // L
    tile = jnp.minimum(s, n_tiles - 1)
    seq_start = (tile % tiles_per_seq) == 0

    @pl.when(s == 0)
    def _():
        ml_prev[...] = jnp.zeros_like(ml_prev)

    @pl.when(seq_start)
    def _():
        cn_scr[...] = jnp.zeros_like(cn_scr)
        m_scr[...] = jnp.full_like(m_scr, M_INIT)

    has_prev = jnp.logical_not(seq_start)

    def conv_silu(cur_ref, prev_ref, ext, col0):
        ext[:CONV_HALO, :] = jnp.where(has_prev, prev_ref[...], 0.0)
        ext[CONV_HALO:, :] = cur_ref[...]
        y = cb_ref[:, col0:col0 + W]
        for sh in range(CONV_WIDTH):
            tap = cw_ref[CONV_WIDTH - 1 - sh:CONV_WIDTH - sh, col0:col0 + W]
            y = y + ext[CONV_HALO - sh:CONV_HALO - sh + MIX_TILE, :] * tap
        return y * jax.nn.sigmoid(y)

    qb = conv_silu(q_ref, qp_ref, qext_scr, 0).astype(BF16)
    kf = conv_silu(k_ref, kp_ref, kext_scr, W) * (HEAD_DIM ** -0.5)

    mix_prev = jnp.concatenate([a_ref[hd] for hd in range(ATTN_HEADS)] + [ml_prev[...]], axis=1)
    n_slab = D_MODEL // n_chunk
    for ck in range(n_chunk):
        cols = slice(ck * n_slab, (ck + 1) * n_slab)
        x1_ref[:, cols] = (jnp.dot(mix_prev, wo_ref[:, cols], preferred_element_type=F32)
                           + x_ref[:, cols])
        rows = slice(ck * L, (ck + 1) * L)
        _mlstm_chunk(gt_ref[:, rows], qb[rows, :], kf[rows, :], v_ref, mo_ref, ng_ref, ml_cur, rows,
                     cn_scr, m_scr)

    y = x1_ref[...]
    ms = jnp.mean(y * y, axis=-1, keepdims=True)
    h2_ref[...] = (y * lax.rsqrt(ms + EPS) * gf_ref[...]).astype(BF16)
    ml_prev[...] = ml_cur[...]


def _mix_out(gates_t, m_f32, m_v, conv_w, conv_b, norm_g, attn, x2, w_o, g_ffn, batch, seq):
    t = batch * seq
    W = MLSTM_WIDTH
    assert seq % MIX_TILE == 0
    tiles_per_seq = seq // MIX_TILE
    n_tiles = t // MIX_TILE
    halo_per_tile = MIX_TILE // CONV_HALO

    def cur(s):
        return jnp.minimum(s, n_tiles - 1)

    def prev(s):
        return jnp.maximum(s - 1, 0)

    def halo(s):
        return jnp.maximum(cur(s) * halo_per_tile - 1, 0)

    const = lambda s: (0, 0)
    return pl.pallas_call(
        functools.partial(_mix_out_body, tiles_per_seq=tiles_per_seq, n_tiles=n_tiles),
        grid=(n_tiles + 1,),
        in_specs=[
            pl.BlockSpec((N_GATE, MIX_TILE), lambda s: (0, cur(s))),
            pl.BlockSpec((MIX_TILE, W), lambda s: (cur(s), 0)),
            pl.BlockSpec((MIX_TILE, W), lambda s: (cur(s), 1)),
            pl.BlockSpec((CONV_HALO, W), lambda s: (halo(s), 0)),
            pl.BlockSpec((CONV_HALO, W), lambda s: (halo(s), 1)),
            pl.BlockSpec((MIX_TILE, W), lambda s: (cur(s), 0)),
            pl.BlockSpec((MIX_TILE, W), lambda s: (cur(s), 2)),
            pl.BlockSpec((CONV_WIDTH, 2 * W), const),
            pl.BlockSpec((1, 2 * W), const),
            pl.BlockSpec((1, W), const),
            pl.BlockSpec((ATTN_HEADS, MIX_TILE, HEAD_DIM), lambda s: (0, prev(s), 0)),
            pl.BlockSpec((MIX_TILE, D_MODEL), lambda s: (prev(s), 0)),
            pl.BlockSpec((ATTN_WIDTH + MLSTM_WIDTH, D_MODEL), const, pipeline_mode=pl.Buffered(1)),
            pl.BlockSpec((1, D_MODEL), const),
        ],
        out_specs=[
            pl.BlockSpec((MIX_TILE, D_MODEL), lambda s: (prev(s), 0)),
            pl.BlockSpec((MIX_TILE, D_MODEL), lambda s: (prev(s), 0)),
        ],
        out_shape=[
            jax.ShapeDtypeStruct((t, D_MODEL), F32),
            jax.ShapeDtypeStruct((t, D_MODEL), BF16),
        ],
        scratch_shapes=[pltpu.VMEM((MLSTM_HEADS, HEAD_DIM, 2 * HEAD_DIM), F32),
                        pltpu.VMEM((MLSTM_HEADS, LANES), F32),
                        pltpu.VMEM((CONV_HALO + MIX_TILE, W), F32),
                        pltpu.VMEM((CONV_HALO + MIX_TILE, W), F32),
                        pltpu.VMEM((MIX_TILE, W), BF16),
                        pltpu.VMEM((MIX_TILE, W), BF16)],
        compiler_params=pltpu.CompilerParams(
            dimension_semantics=("arbitrary",),
            vmem_limit_bytes=VMEM_LIMIT_BYTES),
        name="mlstm_out_proj",
    )(gates_t, m_f32, m_f32, m_f32, m_f32, m_v, m_f32, conv_w, conv_b, norm_g,
      attn, x2, w_o, g_ffn)


def kernel(x, norm_mix_g, w_in, conv_w, conv_b, gate_b, q_norm_g, k_norm_g, mlstm_norm_g,
           w_out, norm_ffn_g, w_gate, w_up, w_down):
    B, S, _ = x.shape
    t = B * S
    layer = 0
    x2 = x.reshape(t, D_MODEL)

    w_t = w_in[layer].T
    a3 = 3 * ATTN_WIDTH
    mw = MLSTM_WIDTH
    n_main = a3 + 4 * mw
    w_main = _cast_rows_bf16(w_t, n_main, 512)
    w_gate_t = w_t[n_main:]
    b_gate = gate_b[layer].reshape(N_GATE, 1)
    head_gain = jnp.stack([
        jnp.tile(q_norm_g[layer] * (HEAD_DIM ** -0.5 * LOG2_E), ATTN_HEADS),
        jnp.tile(k_norm_g[layer], ATTN_HEADS)]).reshape(2, 1, ATTN_WIDTH)

    att, m_f32, m_v, gates_t, wo_b, wg_b, wu_b, wd_b = _in_proj(
        x2, norm_mix_g[layer].reshape(1, D_MODEL), w_main, w_gate_t, b_gate, head_gain,
        (w_out[layer], w_gate[layer], w_up[layer], w_down[layer]), tm=1024)

    attn_out = _dilated_attention(att, B, S)

    x1, h2 = _mix_out(gates_t, m_f32, m_v, conv_w[layer], conv_b[layer].reshape(1, 2 * mw),
                      mlstm_norm_g[layer].reshape(1, mw), attn_out, x2, wo_b,
                      norm_ffn_g[layer].reshape(1, D_MODEL), B, S)

    out = _ffn(h2, wg_b, wu_b, wd_b, x1, tm=1024, th=512)
    return out.reshape(B, S, D_MODEL)
```

```python
import functools

import jax
import jax.numpy as jnp
from jax import lax
from jax.experimental import pallas as pl
from jax.experimental.pallas import tpu as pltpu

D_MODEL = 2048
HEAD_DIM = 128
ATTN_HEADS = 8
MLSTM_HEADS = 8
ATTN_WIDTH = ATTN_HEADS * HEAD_DIM
MLSTM_WIDTH = MLSTM_HEADS * HEAD_DIM
ATTN_PATTERNS = ((128, 1), (512, 4), (2048, 16))
ATTN_BLOCK = 128
MLSTM_CHUNK = 128
CONV_WIDTH = 4
FFN_HIDDEN = 5632
EPS = 1e-6
MASK_VALUE = -1e30
M_INIT = -1e30

LOG2_E = 1.4426950408889634
N_GATE = 2 * MLSTM_HEADS
LANES = 128
V7X_VMEM_BYTES = 64 * 1024 * 1024
VMEM_LIMIT_BYTES = V7X_VMEM_BYTES - 5 * 1024 * 1024

F32 = jnp.float32
BF16 = jnp.bfloat16


def _cast_body(w_ref, o_ref):
    o_ref[...] = w_ref[...].astype(BF16)


def _cast_rows_bf16(w, n_rows, tr):
    cols = w.shape[1]
    return pl.pallas_call(
        _cast_body,
        grid=(n_rows // tr,),
        in_specs=[pl.BlockSpec((tr, cols), lambda j: (j, 0))],
        out_specs=pl.BlockSpec((tr, cols), lambda j: (j, 0)),
        out_shape=jax.ShapeDtypeStruct((n_rows, cols), BF16),
        compiler_params=pltpu.CompilerParams(
            dimension_semantics=("arbitrary",),
            vmem_limit_bytes=VMEM_LIMIT_BYTES),
        name="cast_w_in",
    )(w)


def _side_cast_specs(weights, n_step, step_of):
    specs, shapes = [], []
    for wt in weights:
        rows, cols = wt.shape
        n_blk = max(n for n in range(1, n_step + 1)
                    if rows % n == 0 and (rows // n) % BF16_SUBLANES == 0)
        specs.append(pl.BlockSpec(
            (rows // n_blk, cols),
            lambda *idx, n_blk=n_blk: (jnp.minimum(step_of(*idx), n_blk - 1), 0)))
        shapes.append(jax.ShapeDtypeStruct(wt.shape, BF16))
    return specs, shapes


def _side_cast(cast_in, cast_out):
    for src, dst in zip(cast_in, cast_out):
        dst[...] = src[...].astype(BF16)


def _in_proj_body(x_ref, g_ref, w_ref, wg_ref, bg_ref, hg_ref, *rest, n_cast):
    cast_in, rest = rest[:n_cast], rest[n_cast:]
    oatt_ref, omf_ref, omv_ref, ogate_ref = rest[:4]
    cast_out, h_scr = rest[4:4 + n_cast], rest[4 + n_cast]
    j = pl.program_id(1)

    _side_cast(cast_in, cast_out)

    nt = (((1,), (1,)), ((), ()))

    def project(h=None):
        h = h_scr[...] if h is None else h
        return lax.dot_general(h, w_ref[...], nt, preferred_element_type=F32)

    def store_head_normed(acc):
        for hd in range(ATTN_HEADS):
            sl = slice(hd * HEAD_DIM, (hd + 1) * HEAD_DIM)
            a = acc[:, sl]
            ms = jnp.mean(a * a, axis=-1, keepdims=True)
            oatt_ref[hd] = (a * lax.rsqrt(ms + EPS) * hg_ref[:, sl]).astype(BF16)

    @pl.when(j == 0)
    def _():
        x = x_ref[...]
        ms = jnp.mean(x * x, axis=-1, keepdims=True)
        hb = (x * lax.rsqrt(ms + EPS) * g_ref[...]).astype(BF16)
        h_scr[...] = hb
        ogate_ref[...] = lax.dot_general(wg_ref[...].astype(BF16), hb, nt,
                                         preferred_element_type=F32) + bg_ref[...]
        store_head_normed(project(hb))

    @pl.when(j == 1)
    def _():
        store_head_normed(project())

    @pl.when(j == 2)
    def _():
        acc = project()
        for hd in range(ATTN_HEADS):
            oatt_ref[hd] = acc[:, hd * HEAD_DIM:(hd + 1) * HEAD_DIM].astype(BF16)

    @pl.when(jnp.logical_and(j >= 3, j < 6))
    def _():
        omf_ref[...] = project()

    @pl.when(j == 6)
    def _():
        omv_ref[...] = project().astype(BF16)


def _in_proj(x2, g_mix, w_main, w_gate_t, b_gate, head_gain, side_weights, tm):
    t = x2.shape[0]
    tn = ATTN_WIDTH
    n_col = w_main.shape[0] // tn
    n_step = (t // tm) * n_col

    def w_col(j):
        return jnp.where(j == 5, 6, jnp.where(j == 6, 5, j))

    cast_specs, cast_shapes = _side_cast_specs(side_weights, n_step, lambda i, j: i * n_col + j)

    return pl.pallas_call(
        functools.partial(_in_proj_body, n_cast=len(side_weights)),
        grid=(t // tm, n_col),
        in_specs=[
            pl.BlockSpec((tm, D_MODEL), lambda i, j: (i, 0)),
            pl.BlockSpec((1, D_MODEL), lambda i, j: (0, 0)),
            pl.BlockSpec((tn, D_MODEL), lambda i, j: (w_col(j), 0)),
            pl.BlockSpec((N_GATE, D_MODEL), lambda i, j: (0, 0)),
            pl.BlockSpec((N_GATE, 1), lambda i, j: (0, 0)),
            pl.BlockSpec((None, 1, tn), lambda i, j: (jnp.minimum(j, 1), 0, 0)),
        ] + cast_specs,
        out_specs=[
            pl.BlockSpec((None, ATTN_HEADS, tm, HEAD_DIM), lambda i, j: (jnp.minimum(j, 2), 0, i, 0)),
            pl.BlockSpec((tm, tn), lambda i, j: (i, jnp.clip(j - 3, 0, 2))),
            pl.BlockSpec((tm, tn), lambda i, j: (i, 0)),
            pl.BlockSpec((N_GATE, tm), lambda i, j: (0, i)),
        ] + cast_specs,
        out_shape=[
            jax.ShapeDtypeStruct((3, ATTN_HEADS, t, HEAD_DIM), BF16),
            jax.ShapeDtypeStruct((t, 3 * tn), F32),
            jax.ShapeDtypeStruct((t, tn), BF16),
            jax.ShapeDtypeStruct((N_GATE, t), F32),
        ] + cast_shapes,
        scratch_shapes=[pltpu.VMEM((tm, D_MODEL), BF16)],
        compiler_params=pltpu.CompilerParams(
            dimension_semantics=("arbitrary", "arbitrary"),
            vmem_limit_bytes=VMEM_LIMIT_BYTES),
        name="in_proj",
    )(x2, g_mix, w_main, w_gate_t, b_gate, head_gain, *side_weights)


FFN_DOWN_CHUNK = 512


def _ffn_body(h_ref, wg_ref, wu_ref, wd_ref, x1_ref, o_ref, *, n_res):
    c = pl.program_id(1)

    @pl.when(c == 0)
    def _():
        o_ref[...] = jnp.zeros_like(o_ref)

    @pl.when(c < n_res)
    def _():
        slab = x1_ref.shape[0]
        rows = pl.ds(pl.multiple_of(c * slab, slab), slab)
        o_ref[rows, :] += x1_ref[...]

    h = h_ref[...]
    th = wg_ref.shape[1]
    acts = []
    for k0 in range(0, th, th // 2):
        kc = slice(k0, k0 + th // 2)
        g = jnp.dot(h, wg_ref[:, kc], preferred_element_type=F32)
        u = jnp.dot(h, wu_ref[:, kc], preferred_element_type=F32)
        acts.append((g * jax.nn.sigmoid(g) * u).astype(BF16))
    a = jnp.concatenate(acts, axis=1)
    for n0 in range(0, D_MODEL, FFN_DOWN_CHUNK):
        cols = slice(n0, n0 + FFN_DOWN_CHUNK)
        o_ref[:, cols] += jnp.dot(a, wd_ref[:, cols], preferred_element_type=F32)


def _ffn(h2, w_g, w_u, w_d, x1, tm, th):
    t = h2.shape[0]
    n_chunk = FFN_HIDDEN // th
    n_res = 8
    assert n_res <= n_chunk and tm % n_res == 0
    return pl.pallas_call(
        functools.partial(_ffn_body, n_res=n_res),
        grid=(t // tm, n_chunk),
        in_specs=[
            pl.BlockSpec((tm, D_MODEL), lambda i, c: (i, 0)),
            pl.BlockSpec((D_MODEL, th), lambda i, c: (0, c)),
            pl.BlockSpec((D_MODEL, th), lambda i, c: (0, c)),
            pl.BlockSpec((th, D_MODEL), lambda i, c: (c, 0)),
            pl.BlockSpec((tm // n_res, D_MODEL), lambda i, c: (i * n_res + jnp.minimum(c, n_res - 1), 0)),
        ],
        out_specs=pl.BlockSpec((tm, D_MODEL), lambda i, c: (i, 0)),
        out_shape=jax.ShapeDtypeStruct((t, D_MODEL), F32),
        compiler_params=pltpu.CompilerParams(
            dimension_semantics=("arbitrary", "arbitrary"),
            vmem_limit_bytes=VMEM_LIMIT_BYTES),
        name="ffn",
    )(h2, w_g, w_u, w_d, x1)


ATTN_GROUPS = 16
ATTN_TILE = ATTN_BLOCK * ATTN_GROUPS
TILE_LANES = ATTN_GROUPS * HEAD_DIM
F32_SUBLANES = 8
BF16_SUBLANES = 16
STAGE_PITCH = 24


def _band_mask(slab, n_slab, span):
    blk = ATTN_BLOCK
    shift = slab.bit_length() - 1
    row = lax.broadcasted_iota(jnp.int32, (blk, 2 * blk), 0)
    col = lax.broadcasted_iota(jnp.int32, (blk, 2 * blk), 1)

    def pos(r):
        return (r & (slab - 1)) * n_slab + lax.shift_right_logical(r, shift)

    k_rel = pos(col & (blk - 1)) - jnp.where(col < blk, blk, 0)
    dist = pos(row) - k_rel
    return jnp.logical_and(dist >= 0, dist <= span), col


def _attn_body(*refs, depth, n_cast):
    q_in, k_in, v_in = refs[:3]
    cast_in, o_ref, cast_out = refs[3:3 + n_cast], refs[3 + n_cast], refs[4 + n_cast:4 + 2 * n_cast]
    (acc_scr, m_scr, l_scr, stage_q, stage_k, stage_v, q_ref, k_ref, v_ref, qf_scr, kf_scr, vf_scr,
     kp_ref, vp_ref, kpf_scr, vpf_scr) = refs[4 + 2 * n_cast:]
    jt = pl.program_id(2)
    blk = ATTN_BLOCK
    ones_v = jnp.ones((2 * blk, HEAD_DIM), BF16)
    _side_cast(cast_in, cast_out)

    def lanes(g):
        return slice(g * HEAD_DIM, (g + 1) * HEAD_DIM)

    @pl.when(jt == 0)
    def _():
        kp_ref[...] = jnp.zeros_like(kp_ref)
        vp_ref[...] = jnp.zeros_like(vp_ref)
        kpf_scr[...] = jnp.zeros_like(kpf_scr)
        vpf_scr[...] = jnp.zeros_like(vpf_scr)

    for src, stage, dst_bf16, dst_f32 in ((q_in, stage_q, q_ref, qf_scr),
                                          (k_in, stage_k, k_ref, kf_scr),
                                          (v_in, stage_v, v_ref, vf_scr)):
        for i in range(blk):
            stage[STAGE_PITCH * i:STAGE_PITCH * i + ATTN_GROUPS, :] = (
                src[ATTN_GROUPS * i:ATTN_GROUPS * (i + 1), :].astype(F32))
        for g in range(ATTN_GROUPS):
            cls = stage[pl.ds(g, blk, stride=STAGE_PITCH), :]
            dst_f32[:, lanes(g)] = cls
            dst_bf16[:, lanes(g)] = cls.astype(BF16)

    n_pat = len(ATTN_PATTERNS)
    for pi, (window, dil) in enumerate(reversed(ATTN_PATTERNS)):
        span = window // dil
        n_slab = ATTN_GROUPS // dil
        slab = blk // n_slab
        use_f32 = slab % BF16_SUBLANES != 0
        band, col = _band_mask(slab, n_slab, span)
        band_first = jnp.logical_and(band, jnp.logical_or(col >= blk, jt > 0))

        def gather(cur_ref, cur_f32, prev_ref, prev_f32, res, kb, prev):
            if not prev:
                src, rows = (cur_f32 if use_f32 else cur_ref), slice(slab * kb, slab * (kb + 1))
            elif kb > 0:
                src, rows = (cur_f32 if use_f32 else cur_ref), slice(slab * (kb - 1), slab * kb)
            elif use_f32:
                src, rows = prev_f32, slice(F32_SUBLANES - slab, F32_SUBLANES)
            else:
                src, rows = prev_ref, slice(blk - slab, blk)
            parts = [src[rows, lanes(u * dil + res)] for u in range(n_slab)]
            out = parts[0] if n_slab == 1 else jnp.concatenate(parts, axis=0)
            return out.astype(BF16)

        def st_load(scr, res, kb):
            rows = slice(slab * kb, slab * (kb + 1))
            parts = [scr[rows, lanes(u * dil + res)] for u in range(n_slab)]
            return parts[0] if n_slab == 1 else jnp.concatenate(parts, axis=0)

        def st_store(scr, res, kb, val):
            rows = slice(slab * kb, slab * (kb + 1))
            for u in range(n_slab):
                scr[rows, lanes(u * dil + res)] = val[u * slab:(u + 1) * slab, :]

        def scores_of(res, kb):
            q = gather(q_ref, qf_scr, None, None, res, kb, False)
            kw = jnp.concatenate([gather(k_ref, kf_scr, kp_ref, kpf_scr, res, kb, True),
                                  gather(k_ref, kf_scr, kp_ref, kpf_scr, res, kb, False)], axis=0)
            s = lax.dot_general(q, kw, (((1,), (1,)), ((), ())), preferred_element_type=F32)
            return jnp.where(band_first if kb == 0 else band, s, MASK_VALUE)

        def finish(res, kb, s):
            m = jnp.max(s, axis=-1, keepdims=True)
            p = jnp.exp2(s - m).astype(BF16)
            vw = jnp.concatenate([gather(v_ref, vf_scr, vp_ref, vpf_scr, res, kb, True),
                                  gather(v_ref, vf_scr, vp_ref, vpf_scr, res, kb, False)], axis=0)
            pv = jnp.dot(p, jnp.concatenate([vw, ones_v], axis=1), preferred_element_type=F32)
            num, den = pv[:, :HEAD_DIM], pv[:, HEAD_DIM:]
            m_new = jnp.broadcast_to(m, (blk, HEAD_DIM))
            if pi > 0:
                m_old = st_load(m_scr, res, kb)
                m_new = jnp.maximum(m_old, m_new)
                w_old = jnp.exp2(m_old - m_new)
                w_new = jnp.exp2(m - m_new)
                num = st_load(acc_scr, res, kb) * w_old + num * w_new
                den = st_load(l_scr, res, kb) * w_old + den * w_new
            if pi < n_pat - 1:
                st_store(acc_scr, res, kb, num)
                st_store(m_scr, res, kb, m_new)
                st_store(l_scr, res, kb, den)
            else:
                st_store(acc_scr, res, kb, num / den)

        blocks = [(res, kb) for res in range(dil) for kb in range(n_slab)]
        pending = []
        for t in range(len(blocks) + depth):
            if t < len(blocks):
                pending.append(scores_of(*blocks[t]))
            if t >= depth:
                finish(*blocks[t - depth], pending.pop(0))

    for g in range(ATTN_GROUPS):
        stage_q[pl.ds(g, blk, stride=STAGE_PITCH), :] = acc_scr[:, lanes(g)]
    for i in range(blk):
        o_ref[ATTN_GROUPS * i:ATTN_GROUPS * (i + 1), :] = (
            stage_q[STAGE_PITCH * i:STAGE_PITCH * i + ATTN_GROUPS, :].astype(BF16))

    kp_ref[...] = k_ref[...]
    vp_ref[...] = v_ref[...]
    kpf_scr[...] = kf_scr[blk - F32_SUBLANES:, :]
    vpf_scr[...] = vf_scr[blk - F32_SUBLANES:, :]


def _dilated_attention(att, side_weights, batch, seq, depth=3):
    assert seq % ATTN_TILE == 0
    n_tile = seq // ATTN_TILE
    cast_specs, cast_shapes = _side_cast_specs(
        side_weights, batch * ATTN_HEADS * n_tile,
        lambda bi, hd, jt: (bi * ATTN_HEADS + hd) * n_tile + jt)

    def spec(which):
        return pl.BlockSpec((None, None, ATTN_TILE, HEAD_DIM),
                            lambda bi, hd, jt: (which, hd, bi * n_tile + jt, 0))

    tile_f32 = pltpu.VMEM((ATTN_BLOCK, TILE_LANES), F32)
    tile_bf16 = pltpu.VMEM((ATTN_BLOCK, TILE_LANES), BF16)
    tail_f32 = pltpu.VMEM((F32_SUBLANES, TILE_LANES), F32)
    stage = pltpu.VMEM((ATTN_BLOCK * STAGE_PITCH, HEAD_DIM), F32)
    return pl.pallas_call(
        functools.partial(_attn_body, depth=depth, n_cast=len(side_weights)),
        grid=(batch, ATTN_HEADS, n_tile),
        in_specs=[spec(0), spec(1), spec(2)] + cast_specs,
        out_specs=[pl.BlockSpec((None, ATTN_TILE, HEAD_DIM),
                                lambda bi, hd, jt: (hd, bi * n_tile + jt, 0))] + cast_specs,
        out_shape=[jax.ShapeDtypeStruct((ATTN_HEADS, batch * seq, HEAD_DIM), BF16)] + cast_shapes,
        scratch_shapes=[tile_f32, tile_f32, tile_f32,
                        stage, stage, stage,
                        tile_bf16, tile_bf16, tile_bf16,
                        tile_f32, tile_f32, tile_f32,
                        tile_bf16, tile_bf16,
                        tail_f32, tail_f32],
        compiler_params=pltpu.CompilerParams(
            dimension_semantics=("arbitrary", "arbitrary", "arbitrary"),
            vmem_limit_bytes=VMEM_LIMIT_BYTES),
        name="dilated_attn",
    )(att, att, att, *side_weights)


CONV_HALO = 8


def _mlstm_chunk(gt, qb, kf, v_ref, mo_ref, ng_ref, o_ref, rows, cn_scr, m_scr):
    L = MLSTM_CHUNK

    ig = gt[:MLSTM_HEADS]
    fg = gt[MLSTM_HEADS:]
    lf = jnp.minimum(fg, 0.0) - jnp.log1p(jnp.exp(-jnp.abs(fg)))
    lane = lax.broadcasted_iota(jnp.int32, (MLSTM_HEADS, L), 1)
    b = lf
    s = 1
    while s < L:
        b = b + jnp.where(lane >= s, pltpu.roll(b, s, 1), 0.0)
        s *= 2
    c_row = ig - b
    b_last = b[:, L - 1:L]
    c_max = jnp.max(c_row, axis=1, keepdims=True)
    m_prev = m_scr[:, :1]
    m_chunk = b_last + c_max
    m_new = jnp.maximum(b_last + m_prev, m_chunk)
    decay = jnp.exp(b_last + m_prev - m_new)
    scale = jnp.exp(m_chunk - m_new)
    wa = jnp.exp(c_row - c_max)
    m_scr[...] = jnp.broadcast_to(m_new, m_scr.shape)

    row = lax.broadcasted_iota(jnp.int32, (L, L), 0)
    col = lax.broadcasted_iota(jnp.int32, (L, L), 1)
    causal = col <= row
    ones_v = jnp.ones((L, HEAD_DIM), BF16)

    heads = range(MLSTM_HEADS)

    def hs(hd):
        return slice(hd * HEAD_DIM, (hd + 1) * HEAD_DIM)

    def one(hd):
        return slice(hd, hd + 1)

    k_ts, states, s_qks, inters = [], [], [], []
    for hd in heads:
        q_h = qb[:, hs(hd)]
        k_t = kf[:, hs(hd)].T
        state = cn_scr[hd]
        k_ts.append(k_t)
        states.append(state)
        s_qks.append(jnp.dot(q_h, k_t.astype(BF16), preferred_element_type=F32))
        inters.append(jnp.dot(q_h, state.astype(BF16), preferred_element_type=F32))

    ws, mus, gs = [], [], []
    for hd in heads:
        c_b = jnp.where(causal, jnp.broadcast_to(c_row[one(hd), :], (L, L)), MASK_VALUE)
        mu = jnp.maximum(jnp.max(c_b, axis=-1, keepdims=True), m_prev[one(hd), :])
        ws.append(jnp.exp(c_b - mu) * s_qks[hd])
        mus.append(mu)
        gs.append(jnp.exp(m_prev[one(hd), :] - mu))

    ones_sq = jnp.ones((L, L), BF16)

    def split(x):
        hi = x.astype(BF16)
        return hi, (x - hi.astype(F32)).astype(BF16)

    lf_rows = jnp.concatenate([jnp.broadcast_to(lf[one(hd), :], (HEAD_DIM, L)) for hd in heads], axis=0)
    causal_b = jnp.where(causal, 1.0, 0.0).astype(BF16)
    nt = (((1,), (1,)), ((), ()))
    lf_hi, lf_lo = split(lf_rows)
    b_all = (lax.dot_general(causal_b, lf_hi, nt, preferred_element_type=F32)
             + lax.dot_general(causal_b, lf_lo, nt, preferred_element_type=F32))

    intras, kvns, sum_los = [], [], []
    for hd in heads:
        v_aug = jnp.concatenate([v_ref[rows, hs(hd)], ones_v], axis=1)
        w_hi, w_lo = split(ws[hd])
        intras.append(jnp.dot(w_hi, v_aug, preferred_element_type=F32))
        sum_los.append(jnp.dot(w_lo, ones_sq, preferred_element_type=F32))
        a_t = (k_ts[hd] * jnp.broadcast_to(wa[one(hd), :], (HEAD_DIM, L))).astype(BF16)
        kvns.append(jnp.dot(a_t, v_aug, preferred_element_type=F32))

    cells = []
    for hd in heads:
        g, mu = gs[hd], mus[hd]
        num = g * inters[hd][:, :HEAD_DIM] + intras[hd][:, :HEAD_DIM]
        den = g * inters[hd][:, HEAD_DIM:] + intras[hd][:, HEAD_DIM:] + sum_los[hd]
        cells.append(num / jnp.maximum(jnp.abs(den), jnp.exp(-(b_all[:, hs(hd)] + mu))))

    sq_sums = []
    for hd in heads:
        sq = (cells[hd] * cells[hd]).astype(BF16)
        sq_sums.append(jnp.dot(sq, ones_sq, preferred_element_type=F32))
    for hd in heads:
        y = cells[hd] * lax.rsqrt(sq_sums[hd] * (1.0 / HEAD_DIM) + EPS) * ng_ref[:, hs(hd)]
        o_ref[rows, hs(hd)] = (jax.nn.sigmoid(mo_ref[rows, hs(hd)]) * y).astype(BF16)
        cn_scr[hd] = decay[one(hd), :] * states[hd] + scale[one(hd), :] * kvns[hd]


MIX_TILE = 4 * MLSTM_CHUNK


def _mix_out_body(gt_ref, q_ref, k_ref, qp_ref, kp_ref, v_ref, mo_ref, cw_ref, cb_ref, ng_ref,
                  a_ref, x_ref, wo_ref, gf_ref, x1_ref, h2_ref,
                  cn_scr, m_scr, qext_scr, kext_scr, ml_cur, ml_prev, *, tiles_per_seq, n_tiles):
    s = pl.program_id(0)
    L = MLSTM_CHUNK
    W = MLSTM_WIDTH
    n_chunk = MIX_TILE // L
    tile = jnp.minimum(s, n_tiles - 1)
    seq_start = (tile % tiles_per_seq) == 0

    @pl.when(s == 0)
    def _():
        ml_prev[...] = jnp.zeros_like(ml_prev)

    @pl.when(seq_start)
    def _():
        cn_scr[...] = jnp.zeros_like(cn_scr)
        m_scr[...] = jnp.full_like(m_scr, M_INIT)

    has_prev = jnp.logical_not(seq_start)

    def conv_silu(cur_ref, prev_ref, ext, col0):
        ext[:CONV_HALO, :] = jnp.where(has_prev, prev_ref[...], 0.0)
        ext[CONV_HALO:, :] = cur_ref[...]
        y = cb_ref[:, col0:col0 + W]
        for sh in range(CONV_WIDTH):
            tap = cw_ref[CONV_WIDTH - 1 - sh:CONV_WIDTH - sh, col0:col0 + W]
            y = y + ext[CONV_HALO - sh:CONV_HALO - sh + MIX_TILE, :] * tap
        return y * jax.nn.sigmoid(y)

    qb = conv_silu(q_ref, qp_ref, qext_scr, 0).astype(BF16)
    kf = conv_silu(k_ref, kp_ref, kext_scr, W) * (HEAD_DIM ** -0.5)

    mix_prev = jnp.concatenate([a_ref[hd] for hd in range(ATTN_HEADS)] + [ml_prev[...]], axis=1)
    n_slab = D_MODEL // n_chunk
    for ck in range(n_chunk):
        cols = slice(ck * n_slab, (ck + 1) * n_slab)
        x1_ref[:, cols] = (jnp.dot(mix_prev, wo_ref[:, cols], preferred_element_type=F32)
                           + x_ref[:, cols])
        rows = slice(ck * L, (ck + 1) * L)
        _mlstm_chunk(gt_ref[:, rows], qb[rows, :], kf[rows, :], v_ref, mo_ref, ng_ref, ml_cur, rows,
                     cn_scr, m_scr)

    y = x1_ref[...]
    ms = jnp.mean(y * y, axis=-1, keepdims=True)
    h2_ref[...] = (y * lax.rsqrt(ms + EPS) * gf_ref[...]).astype(BF16)
    ml_prev[...] = ml_cur[...]


def _mix_out(gates_t, m_f32, m_v, conv_w, conv_b, norm_g, attn, x2, w_o, g_ffn, batch, seq):
    t = batch * seq
    W = MLSTM_WIDTH
    assert seq % MIX_TILE == 0
    tiles_per_seq = seq // MIX_TILE
    n_tiles = t // MIX_TILE
    halo_per_tile = MIX_TILE // CONV_HALO

    def cur(s):
        return jnp.minimum(s, n_tiles - 1)

    def prev(s):
        return jnp.maximum(s - 1, 0)

    def halo(s):
        return jnp.maximum(cur(s) * halo_per_tile - 1, 0)

    const = lambda s: (0, 0)
    return pl.pallas_call(
        functools.partial(_mix_out_body, tiles_per_seq=tiles_per_seq, n_tiles=n_tiles),
        grid=(n_tiles + 1,),
        in_specs=[
            pl.BlockSpec((N_GATE, MIX_TILE), lambda s: (0, cur(s))),
            pl.BlockSpec((MIX_TILE, W), lambda s: (cur(s), 0)),
            pl.BlockSpec((MIX_TILE, W), lambda s: (cur(s), 1)),
            pl.BlockSpec((CONV_HALO, W), lambda s: (halo(s), 0)),
            pl.BlockSpec((CONV_HALO, W), lambda s: (halo(s), 1)),
            pl.BlockSpec((MIX_TILE, W), lambda s: (cur(s), 0)),
            pl.BlockSpec((MIX_TILE, W), lambda s: (cur(s), 2)),
            pl.BlockSpec((CONV_WIDTH, 2 * W), const),
            pl.BlockSpec((1, 2 * W), const),
            pl.BlockSpec((1, W), const),
            pl.BlockSpec((ATTN_HEADS, MIX_TILE, HEAD_DIM), lambda s: (0, prev(s), 0)),
            pl.BlockSpec((MIX_TILE, D_MODEL), lambda s: (prev(s), 0)),
            pl.BlockSpec((ATTN_WIDTH + MLSTM_WIDTH, D_MODEL), const, pipeline_mode=pl.Buffered(1)),
            pl.BlockSpec((1, D_MODEL), const),
        ],
        out_specs=[
            pl.BlockSpec((MIX_TILE, D_MODEL), lambda s: (prev(s), 0)),
            pl.BlockSpec((MIX_TILE, D_MODEL), lambda s: (prev(s), 0)),
        ],
        out_shape=[
            jax.ShapeDtypeStruct((t, D_MODEL), F32),
            jax.ShapeDtypeStruct((t, D_MODEL), BF16),
        ],
        scratch_shapes=[pltpu.VMEM((MLSTM_HEADS, HEAD_DIM, 2 * HEAD_DIM), F32),
                        pltpu.VMEM((MLSTM_HEADS, LANES), F32),
                        pltpu.VMEM((CONV_HALO + MIX_TILE, W), F32),
                        pltpu.VMEM((CONV_HALO + MIX_TILE, W), F32),
                        pltpu.VMEM((MIX_TILE, W), BF16),
                        pltpu.VMEM((MIX_TILE, W), BF16)],
        compiler_params=pltpu.CompilerParams(
            dimension_semantics=("arbitrary",),
            vmem_limit_bytes=VMEM_LIMIT_BYTES),
        name="mlstm_out_proj",
    )(gates_t, m_f32, m_f32, m_f32, m_f32, m_v, m_f32, conv_w, conv_b, norm_g,
      attn, x2, w_o, g_ffn)


def kernel(x, norm_mix_g, w_in, conv_w, conv_b, gate_b, q_norm_g, k_norm_g, mlstm_norm_g,
           w_out, norm_ffn_g, w_gate, w_up, w_down):
    B, S, _ = x.shape
    t = B * S
    layer = 0
    x2 = x.reshape(t, D_MODEL)

    w_t = w_in[layer].T
    a3 = 3 * ATTN_WIDTH
    mw = MLSTM_WIDTH
    n_main = a3 + 4 * mw
    w_main = _cast_rows_bf16(w_t, n_main, 512)
    w_gate_t = w_t[n_main:]
    b_gate = gate_b[layer].reshape(N_GATE, 1)
    head_gain = jnp.stack([
        jnp.tile(q_norm_g[layer] * (HEAD_DIM ** -0.5 * LOG2_E), ATTN_HEADS),
        jnp.tile(k_norm_g[layer], ATTN_HEADS)]).reshape(2, 1, ATTN_WIDTH)

    att, m_f32, m_v, gates_t = _in_proj(
        x2, norm_mix_g[layer].reshape(1, D_MODEL), w_main, w_gate_t, b_gate, head_gain, (), tm=1024)

    attn_out, wo_b, wg_b, wu_b, wd_b = _dilated_attention(
        att, (w_out[layer], w_gate[layer], w_up[layer], w_down[layer]), B, S)

    x1, h2 = _mix_out(gates_t, m_f32, m_v, conv_w[layer], conv_b[layer].reshape(1, 2 * mw),
                      mlstm_norm_g[layer].reshape(1, mw), attn_out, x2, wo_b,
                      norm_ffn_g[layer].reshape(1, D_MODEL), B, S)

    out = _ffn(h2, wg_b, wu_b, wd_b, x1, tm=1024, th=512)
    return out.reshape(B, S, D_MODEL)
```

```python
import functools

import jax
import jax.numpy as jnp
from jax import lax
from jax.experimental import pallas as pl
from jax.experimental.pallas import tpu as pltpu

D_MODEL = 2048
HEAD_DIM = 128
ATTN_HEADS = 8
MLSTM_HEADS = 8
ATTN_WIDTH = ATTN_HEADS * HEAD_DIM
MLSTM_WIDTH = MLSTM_HEADS * HEAD_DIM
ATTN_PATTERNS = ((128, 1), (512, 4), (2048, 16))
ATTN_BLOCK = 128
MLSTM_CHUNK = 128
CONV_WIDTH = 4
FFN_HIDDEN = 5632
EPS = 1e-6
MASK_VALUE = -1e30
M_INIT = -1e30

LOG2_E = 1.4426950408889634
N_GATE = 2 * MLSTM_HEADS
LANES = 128
V7X_VMEM_BYTES = 64 * 1024 * 1024
VMEM_LIMIT_BYTES = V7X_VMEM_BYTES - 5 * 1024 * 1024

F32 = jnp.float32
BF16 = jnp.bfloat16


def _cast_body(w_ref, o_ref):
    o_ref[...] = w_ref[...].astype(BF16)


def _cast_rows_bf16(w, n_rows, tr):
    cols = w.shape[1]
    return pl.pallas_call(
        _cast_body,
        grid=(n_rows // tr,),
        in_specs=[pl.BlockSpec((tr, cols), lambda j: (j, 0))],
        out_specs=pl.BlockSpec((tr, cols), lambda j: (j, 0)),
        out_shape=jax.ShapeDtypeStruct((n_rows, cols), BF16),
        compiler_params=pltpu.CompilerParams(
            dimension_semantics=("arbitrary",),
            vmem_limit_bytes=VMEM_LIMIT_BYTES),
        name="cast_w_in",
    )(w)


def _side_cast_specs(weights, n_step, step_of):
    specs, shapes = [], []
    for wt in weights:
        rows, cols = wt.shape
        n_blk = max(n for n in range(1, n_step + 1)
                    if rows % n == 0 and (rows // n) % BF16_SUBLANES == 0)
        specs.append(pl.BlockSpec(
            (rows // n_blk, cols),
            lambda *idx, n_blk=n_blk: (jnp.minimum(step_of(*idx), n_blk - 1), 0)))
        shapes.append(jax.ShapeDtypeStruct(wt.shape, BF16))
    return specs, shapes


def _side_cast(cast_in, cast_out):
    for src, dst in zip(cast_in, cast_out):
        dst[...] = src[...].astype(BF16)


def _in_proj_body(x_ref, g_ref, w_ref, wg_ref, bg_ref, hg_ref, *rest, n_cast):
    cast_in, rest = rest[:n_cast], rest[n_cast:]
    oatt_ref, omf_ref, omv_ref, ogate_ref = rest[:4]
    cast_out, h_scr = rest[4:4 + n_cast], rest[4 + n_cast]
    j = pl.program_id(1)

    _side_cast(cast_in, cast_out)

    nt = (((1,), (1,)), ((), ()))

    def project(h=None):
        h = h_scr[...] if h is None else h
        return lax.dot_general(h, w_ref[...], nt, preferred_element_type=F32)

    def store_head_normed(acc):
        for hd in range(ATTN_HEADS):
            sl = slice(hd * HEAD_DIM, (hd + 1) * HEAD_DIM)
            a = acc[:, sl]
            ms = jnp.mean(a * a, axis=-1, keepdims=True)
            oatt_ref[hd] = (a * lax.rsqrt(ms + EPS) * hg_ref[:, sl]).astype(BF16)

    @pl.when(j == 0)
    def _():
        x = x_ref[...]
        ms = jnp.mean(x * x, axis=-1, keepdims=True)
        hb = (x * lax.rsqrt(ms + EPS) * g_ref[...]).astype(BF16)
        h_scr[...] = hb
        ogate_ref[...] = lax.dot_general(wg_ref[...].astype(BF16), hb, nt,
                                         preferred_element_type=F32) + bg_ref[...]
        store_head_normed(project(hb))

    @pl.when(j == 1)
    def _():
        store_head_normed(project())

    @pl.when(j == 2)
    def _():
        acc = project()
        for hd in range(ATTN_HEADS):
            oatt_ref[hd] = acc[:, hd * HEAD_DIM:(hd + 1) * HEAD_DIM].astype(BF16)

    @pl.when(jnp.logical_and(j >= 3, j < 6))
    def _():
        omf_ref[...] = project()

    @pl.when(j == 6)
    def _():
        omv_ref[...] = project().astype(BF16)


def _in_proj(x2, g_mix, w_main, w_gate_t, b_gate, head_gain, side_weights, tm):
    t = x2.shape[0]
    tn = ATTN_WIDTH
    n_col = w_main.shape[0] // tn
    n_step = (t // tm) * n_col

    def w_col(j):
        return jnp.where(j == 5, 6, jnp.where(j == 6, 5, j))

    cast_specs, cast_shapes = _side_cast_specs(side_weights, n_step, lambda i, j: i * n_col + j)

    return pl.pallas_call(
        functools.partial(_in_proj_body, n_cast=len(side_weights)),
        grid=(t // tm, n_col),
        in_specs=[
            pl.BlockSpec((tm, D_MODEL), lambda i, j: (i, 0)),
            pl.BlockSpec((1, D_MODEL), lambda i, j: (0, 0)),
            pl.BlockSpec((tn, D_MODEL), lambda i, j: (w_col(j), 0)),
            pl.BlockSpec((N_GATE, D_MODEL), lambda i, j: (0, 0)),
            pl.BlockSpec((N_GATE, 1), lambda i, j: (0, 0)),
            pl.BlockSpec((None, 1, tn), lambda i, j: (jnp.minimum(j, 1), 0, 0)),
        ] + cast_specs,
        out_specs=[
            pl.BlockSpec((None, ATTN_HEADS, tm, HEAD_DIM), lambda i, j: (jnp.minimum(j, 2), 0, i, 0)),
            pl.BlockSpec((tm, tn), lambda i, j: (i, jnp.clip(j - 3, 0, 2))),
            pl.BlockSpec((tm, tn), lambda i, j: (i, 0)),
            pl.BlockSpec((N_GATE, tm), lambda i, j: (0, i)),
        ] + cast_specs,
        out_shape=[
            jax.ShapeDtypeStruct((3, ATTN_HEADS, t, HEAD_DIM), BF16),
            jax.ShapeDtypeStruct((t, 3 * tn), F32),
            jax.ShapeDtypeStruct((t, tn), BF16),
            jax.ShapeDtypeStruct((N_GATE, t), F32),
        ] + cast_shapes,
        scratch_shapes=[pltpu.VMEM((tm, D_MODEL), BF16)],
        compiler_params=pltpu.CompilerParams(
            dimension_semantics=("arbitrary", "arbitrary"),
            vmem_limit_bytes=VMEM_LIMIT_BYTES),
        name="in_proj",
    )(x2, g_mix, w_main, w_gate_t, b_gate, head_gain, *side_weights)


FFN_DOWN_CHUNK = 512


def _ffn_body(h_ref, wg_ref, wu_ref, wd_ref, x1_ref, o_ref, *, n_res):
    c = pl.program_id(1)

    def add_residual():
        slab = x1_ref.shape[0]
        rows = pl.ds(pl.multiple_of(c * slab, slab), slab)
        o_ref[rows, :] += x1_ref[...]

    def hidden_chunk(first):
        h = h_ref[...]
        th = wg_ref.shape[1]
        acts = []
        for k0 in range(0, th, th // 2):
            kc = slice(k0, k0 + th // 2)
            g = jnp.dot(h, wg_ref[:, kc], preferred_element_type=F32)
            u = jnp.dot(h, wu_ref[:, kc], preferred_element_type=F32)
            acts.append((g * jax.nn.sigmoid(g) * u).astype(BF16))
        a = jnp.concatenate(acts, axis=1)
        for n0 in range(0, D_MODEL, FFN_DOWN_CHUNK):
            cols = slice(n0, n0 + FFN_DOWN_CHUNK)
            d = jnp.dot(a, wd_ref[:, cols], preferred_element_type=F32)
            if first:
                o_ref[:, cols] = d
            else:
                o_ref[:, cols] += d

    @pl.when(c == 0)
    def _():
        hidden_chunk(True)
        add_residual()

    @pl.when(c > 0)
    def _():
        pl.when(c < n_res)(add_residual)
        hidden_chunk(False)


def _ffn(h2, w_g, w_u, w_d, x1, tm, th):
    t = h2.shape[0]
    n_chunk = FFN_HIDDEN // th
    n_res = 8
    assert n_res <= n_chunk and tm % n_res == 0
    return pl.pallas_call(
        functools.partial(_ffn_body, n_res=n_res),
        grid=(t // tm, n_chunk),
        in_specs=[
            pl.BlockSpec((tm, D_MODEL), lambda i, c: (i, 0)),
            pl.BlockSpec((D_MODEL, th), lambda i, c: (0, c)),
            pl.BlockSpec((D_MODEL, th), lambda i, c: (0, c)),
            pl.BlockSpec((th, D_MODEL), lambda i, c: (c, 0)),
            pl.BlockSpec((tm // n_res, D_MODEL), lambda i, c: (i * n_res + jnp.minimum(c, n_res - 1), 0)),
        ],
        out_specs=pl.BlockSpec((tm, D_MODEL), lambda i, c: (i, 0)),
        out_shape=jax.ShapeDtypeStruct((t, D_MODEL), F32),
        compiler_params=pltpu.CompilerParams(
            dimension_semantics=("arbitrary", "arbitrary"),
            vmem_limit_bytes=VMEM_LIMIT_BYTES),
        name="ffn",
    )(h2, w_g, w_u, w_d, x1)


ATTN_GROUPS = 16
ATTN_TILE = ATTN_BLOCK * ATTN_GROUPS
TILE_LANES = ATTN_GROUPS * HEAD_DIM
F32_SUBLANES = 8
BF16_SUBLANES = 16
STAGE_PITCH = 24


def _band_mask(slab, n_slab, span):
    blk = ATTN_BLOCK
    shift = slab.bit_length() - 1
    row = lax.broadcasted_iota(jnp.int32, (blk, 2 * blk), 0)
    col = lax.broadcasted_iota(jnp.int32, (blk, 2 * blk), 1)

    def pos(r):
        return (r & (slab - 1)) * n_slab + lax.shift_right_logical(r, shift)

    k_rel = pos(col & (blk - 1)) - jnp.where(col < blk, blk, 0)
    dist = pos(row) - k_rel
    return jnp.logical_and(dist >= 0, dist <= span), col


def _attn_body(*refs, depth, n_cast):
    q_in, k_in, v_in = refs[:3]
    cast_in, o_ref, cast_out = refs[3:3 + n_cast], refs[3 + n_cast], refs[4 + n_cast:4 + 2 * n_cast]
    (acc_scr, m_scr, l_scr, stage_q, stage_k, stage_v, q_ref, k_ref, v_ref, qf_scr, kf_scr, vf_scr,
     kp_ref, vp_ref, kpf_scr, vpf_scr) = refs[4 + 2 * n_cast:]
    jt = pl.program_id(2)
    blk = ATTN_BLOCK
    ones_v = jnp.ones((2 * blk, HEAD_DIM), BF16)
    _side_cast(cast_in, cast_out)

    def lanes(g):
        return slice(g * HEAD_DIM, (g + 1) * HEAD_DIM)

    @pl.when(jt == 0)
    def _():
        kp_ref[...] = jnp.zeros_like(kp_ref)
        vp_ref[...] = jnp.zeros_like(vp_ref)
        kpf_scr[...] = jnp.zeros_like(kpf_scr)
        vpf_scr[...] = jnp.zeros_like(vpf_scr)

    for src, stage, dst_bf16, dst_f32 in ((q_in, stage_q, q_ref, qf_scr),
                                          (k_in, stage_k, k_ref, kf_scr),
                                          (v_in, stage_v, v_ref, vf_scr)):
        for i in range(blk):
            stage[STAGE_PITCH * i:STAGE_PITCH * i + ATTN_GROUPS, :] = (
                src[ATTN_GROUPS * i:ATTN_GROUPS * (i + 1), :].astype(F32))
        for g in range(ATTN_GROUPS):
            cls = stage[pl.ds(g, blk, stride=STAGE_PITCH), :]
            dst_f32[:, lanes(g)] = cls
            dst_bf16[:, lanes(g)] = cls.astype(BF16)

    n_pat = len(ATTN_PATTERNS)
    for pi, (window, dil) in enumerate(reversed(ATTN_PATTERNS)):
        span = window // dil
        n_slab = ATTN_GROUPS // dil
        slab = blk // n_slab
        use_f32 = slab % BF16_SUBLANES != 0
        band, col = _band_mask(slab, n_slab, span)
        band_first = jnp.logical_and(band, jnp.logical_or(col >= blk, jt > 0))

        def gather(cur_ref, cur_f32, prev_ref, prev_f32, res, kb, prev):
            if not prev:
                src, rows = (cur_f32 if use_f32 else cur_ref), slice(slab * kb, slab * (kb + 1))
            elif kb > 0:
                src, rows = (cur_f32 if use_f32 else cur_ref), slice(slab * (kb - 1), slab * kb)
            elif use_f32:
                src, rows = prev_f32, slice(F32_SUBLANES - slab, F32_SUBLANES)
            else:
                src, rows = prev_ref, slice(blk - slab, blk)
            parts = [src[rows, lanes(u * dil + res)] for u in range(n_slab)]
            out = parts[0] if n_slab == 1 else jnp.concatenate(parts, axis=0)
            return out.astype(BF16)

        def st_load(scr, res, kb):
            rows = slice(slab * kb, slab * (kb + 1))
            parts = [scr[rows, lanes(u * dil + res)] for u in range(n_slab)]
            return parts[0] if n_slab == 1 else jnp.concatenate(parts, axis=0)

        def st_store(scr, res, kb, val):
            rows = slice(slab * kb, slab * (kb + 1))
            for u in range(n_slab):
                scr[rows, lanes(u * dil + res)] = val[u * slab:(u + 1) * slab, :]

        def scores_of(res, kb):
            q = gather(q_ref, qf_scr, None, None, res, kb, False)
            kw = jnp.concatenate([gather(k_ref, kf_scr, kp_ref, kpf_scr, res, kb, True),
                                  gather(k_ref, kf_scr, kp_ref, kpf_scr, res, kb, False)], axis=0)
            s = lax.dot_general(q, kw, (((1,), (1,)), ((), ())), preferred_element_type=F32)
            return jnp.where(band_first if kb == 0 else band, s, MASK_VALUE)

        def finish(res, kb, s):
            m = jnp.max(s, axis=-1, keepdims=True)
            p = jnp.exp2(s - m).astype(BF16)
            vw = jnp.concatenate([gather(v_ref, vf_scr, vp_ref, vpf_scr, res, kb, True),
                                  gather(v_ref, vf_scr, vp_ref, vpf_scr, res, kb, False)], axis=0)
            pv = jnp.dot(p, jnp.concatenate([vw, ones_v], axis=1), preferred_element_type=F32)
            num, den = pv[:, :HEAD_DIM], pv[:, HEAD_DIM:]
            m_new = jnp.broadcast_to(m, (blk, HEAD_DIM))
            if pi > 0:
                m_old = st_load(m_scr, res, kb)
                m_new = jnp.maximum(m_old, m_new)
                w_old = jnp.exp2(m_old - m_new)
                w_new = jnp.exp2(m - m_new)
                num = st_load(acc_scr, res, kb) * w_old + num * w_new
                den = st_load(l_scr, res, kb) * w_old + den * w_new
            if pi < n_pat - 1:
                st_store(acc_scr, res, kb, num)
                st_store(m_scr, res, kb, m_new)
                st_store(l_scr, res, kb, den)
            else:
                st_store(acc_scr, res, kb, num / den)

        blocks = [(res, kb) for res in range(dil) for kb in range(n_slab)]
        pending = []
        for t in range(len(blocks) + depth):
            if t < len(blocks):
                pending.append(scores_of(*blocks[t]))
            if t >= depth:
                finish(*blocks[t - depth], pending.pop(0))

    for g in range(ATTN_GROUPS):
        stage_q[pl.ds(g, blk, stride=STAGE_PITCH), :] = acc_scr[:, lanes(g)]
    for i in range(blk):
        o_ref[ATTN_GROUPS * i:ATTN_GROUPS * (i + 1), :] = (
            stage_q[STAGE_PITCH * i:STAGE_PITCH * i + ATTN_GROUPS, :].astype(BF16))

    kp_ref[...] = k_ref[...]
    vp_ref[...] = v_ref[...]
    kpf_scr[...] = kf_scr[blk - F32_SUBLANES:, :]
    vpf_scr[...] = vf_scr[blk - F32_SUBLANES:, :]


def _dilated_attention(att, side_weights, batch, seq, depth=3):
    assert seq % ATTN_TILE == 0
    n_tile = seq // ATTN_TILE
    cast_specs, cast_shapes = _side_cast_specs(
        side_weights, batch * ATTN_HEADS * n_tile,
        lambda bi, hd, jt: (bi * ATTN_HEADS + hd) * n_tile + jt)

    def spec(which):
        return pl.BlockSpec((None, None, ATTN_TILE, HEAD_DIM),
                            lambda bi, hd, jt: (which, hd, bi * n_tile + jt, 0))

    tile_f32 = pltpu.VMEM((ATTN_BLOCK, TILE_LANES), F32)
    tile_bf16 = pltpu.VMEM((ATTN_BLOCK, TILE_LANES), BF16)
    tail_f32 = pltpu.VMEM((F32_SUBLANES, TILE_LANES), F32)
    stage = pltpu.VMEM((ATTN_BLOCK * STAGE_PITCH, HEAD_DIM), F32)
    return pl.pallas_call(
        functools.partial(_attn_body, depth=depth, n_cast=len(side_weights)),
        grid=(batch, ATTN_HEADS, n_tile),
        in_specs=[spec(0), spec(1), spec(2)] + cast_specs,
        out_specs=[pl.BlockSpec((None, ATTN_TILE, HEAD_DIM),
                                lambda bi, hd, jt: (hd, bi * n_tile + jt, 0))] + cast_specs,
        out_shape=[jax.ShapeDtypeStruct((ATTN_HEADS, batch * seq, HEAD_DIM), BF16)] + cast_shapes,
        scratch_shapes=[tile_f32, tile_f32, tile_f32,
                        stage, stage, stage,
                        tile_bf16, tile_bf16, tile_bf16,
                        tile_f32, tile_f32, tile_f32,
                        tile_bf16, tile_bf16,
                        tail_f32, tail_f32],
        compiler_params=pltpu.CompilerParams(
            dimension_semantics=("arbitrary", "arbitrary", "arbitrary"),
            vmem_limit_bytes=VMEM_LIMIT_BYTES),
        name="dilated_attn",
    )(att, att, att, *side_weights)


CONV_HALO = 8


def _mlstm_chunk(gt, qb, kf, v_ref, mo_ref, ng_ref, o_ref, rows, cn_scr, m_scr):
    L = MLSTM_CHUNK

    ig = gt[:MLSTM_HEADS]
    fg = gt[MLSTM_HEADS:]
    lf = jnp.minimum(fg, 0.0) - jnp.log1p(jnp.exp(-jnp.abs(fg)))
    lane = lax.broadcasted_iota(jnp.int32, (MLSTM_HEADS, L), 1)
    b = lf
    s = 1
    while s < L:
        b = b + jnp.where(lane >= s, pltpu.roll(b, s, 1), 0.0)
        s *= 2
    c_row = ig - b
    b_last = b[:, L - 1:L]
    c_max = jnp.max(c_row, axis=1, keepdims=True)
    m_prev = m_scr[:, :1]
    m_chunk = b_last + c_max
    m_new = jnp.maximum(b_last + m_prev, m_chunk)
    decay = jnp.exp(b_last + m_prev - m_new)
    scale = jnp.exp(m_chunk - m_new)
    wa = jnp.exp(c_row - c_max)
    m_scr[...] = jnp.broadcast_to(m_new, m_scr.shape)

    row = lax.broadcasted_iota(jnp.int32, (L, L), 0)
    col = lax.broadcasted_iota(jnp.int32, (L, L), 1)
    causal = col <= row
    ones_v = jnp.ones((L, HEAD_DIM), BF16)

    heads = range(MLSTM_HEADS)

    def hs(hd):
        return slice(hd * HEAD_DIM, (hd + 1) * HEAD_DIM)

    def one(hd):
        return slice(hd, hd + 1)

    k_ts, states, s_qks, inters = [], [], [], []
    for hd in heads:
        q_h = qb[:, hs(hd)]
        k_t = kf[:, hs(hd)].T
        state = cn_scr[hd]
        k_ts.append(k_t)
        states.append(state)
        s_qks.append(jnp.dot(q_h, k_t.astype(BF16), preferred_element_type=F32))
        inters.append(jnp.dot(q_h, state.astype(BF16), preferred_element_type=F32))

    ws, mus, gs = [], [], []
    for hd in heads:
        c_b = jnp.where(causal, jnp.broadcast_to(c_row[one(hd), :], (L, L)), MASK_VALUE)
        mu = jnp.maximum(jnp.max(c_b, axis=-1, keepdims=True), m_prev[one(hd), :])
        ws.append(jnp.exp(c_b - mu) * s_qks[hd])
        mus.append(mu)
        gs.append(jnp.exp(m_prev[one(hd), :] - mu))

    ones_sq = jnp.ones((L, L), BF16)

    def split(x):
        hi = x.astype(BF16)
        return hi, (x - hi.astype(F32)).astype(BF16)

    lf_rows = jnp.concatenate([jnp.broadcast_to(lf[one(hd), :], (HEAD_DIM, L)) for hd in heads], axis=0)
    causal_b = jnp.where(causal, 1.0, 0.0).astype(BF16)
    nt = (((1,), (1,)), ((), ()))
    lf_hi, lf_lo = split(lf_rows)
    b_all = (lax.dot_general(causal_b, lf_hi, nt, preferred_element_type=F32)
             + lax.dot_general(causal_b, lf_lo, nt, preferred_element_type=F32))

    intras, kvns, sum_los = [], [], []
    for hd in heads:
        v_aug = jnp.concatenate([v_ref[rows, hs(hd)], ones_v], axis=1)
        w_hi, w_lo = split(ws[hd])
        intras.append(jnp.dot(w_hi, v_aug, preferred_element_type=F32))
        sum_los.append(jnp.dot(w_lo, ones_sq, preferred_element_type=F32))
        a_t = (k_ts[hd] * jnp.broadcast_to(wa[one(hd), :], (HEAD_DIM, L))).astype(BF16)
        kvns.append(jnp.dot(a_t, v_aug, preferred_element_type=F32))

    cells = []
    for hd in heads:
        g, mu = gs[hd], mus[hd]
        num = g * inters[hd][:, :HEAD_DIM] + intras[hd][:, :HEAD_DIM]
        den = g * inters[hd][:, HEAD_DIM:] + intras[hd][:, HEAD_DIM:] + sum_los[hd]
        cells.append(num / jnp.maximum(jnp.abs(den), jnp.exp(-(b_all[:, hs(hd)] + mu))))

    sq_sums = []
    for hd in heads:
        sq = (cells[hd] * cells[hd]).astype(BF16)
        sq_sums.append(jnp.dot(sq, ones_sq, preferred_element_type=F32))
    for hd in heads:
        y = cells[hd] * lax.rsqrt(sq_sums[hd] * (1.0 / HEAD_DIM) + EPS) * ng_ref[:, hs(hd)]
        o_ref[rows, hs(hd)] = (jax.nn.sigmoid(mo_ref[rows, hs(hd)]) * y).astype(BF16)
        cn_scr[hd] = decay[one(hd), :] * states[hd] + scale[one(hd), :] * kvns[hd]


MIX_TILE = 4 * MLSTM_CHUNK


def _mix_out_body(gt_ref, q_ref, k_ref, qp_ref, kp_ref, v_ref, mo_ref, cw_ref, cb_ref, ng_ref,
                  a_ref, x_ref, wo_ref, gf_ref, x1_ref, h2_ref,
                  cn_scr, m_scr, qext_scr, kext_scr, ml_cur, ml_prev, *, tiles_per_seq, n_tiles):
    s = pl.program_id(0)
    L = MLSTM_CHUNK
    W = MLSTM_WIDTH
    n_chunk = MIX_TILE // L
    tile = jnp.minimum(s, n_tiles - 1)
    seq_start = (tile % tiles_per_seq) == 0

    @pl.when(s == 0)
    def _():
        ml_prev[...] = jnp.zeros_like(ml_prev)

    @pl.when(seq_start)
    def _():
        cn_scr[...] = jnp.zeros_like(cn_scr)
        m_scr[...] = jnp.full_like(m_scr, M_INIT)

    has_prev = jnp.logical_not(seq_start)

    def conv_silu(cur_ref, prev_ref, ext, col0):
        ext[:CONV_HALO, :] = jnp.where(has_prev, prev_ref[...], 0.0)
        ext[CONV_HALO:, :] = cur_ref[...]
        y = cb_ref[:, col0:col0 + W]
        for sh in range(CONV_WIDTH):
            tap = cw_ref[CONV_WIDTH - 1 - sh:CONV_WIDTH - sh, col0:col0 + W]
            y = y + ext[CONV_HALO - sh:CONV_HALO - sh + MIX_TILE, :] * tap
        return y * jax.nn.sigmoid(y)

    qb = conv_silu(q_ref, qp_ref, qext_scr, 0).astype(BF16)
    kf = conv_silu(k_ref, kp_ref, kext_scr, W) * (HEAD_DIM ** -0.5)

    mix_prev = jnp.concatenate([a_ref[hd] for hd in range(ATTN_HEADS)] + [ml_prev[...]], axis=1)
    n_slab = D_MODEL // n_chunk
    for ck in range(n_chunk):
        cols = slice(ck * n_slab, (ck + 1) * n_slab)
        x1_ref[:, cols] = (jnp.dot(mix_prev, wo_ref[:, cols], preferred_element_type=F32)
                           + x_ref[:, cols])
        rows = slice(ck * L, (ck + 1) * L)
        _mlstm_chunk(gt_ref[:, rows], qb[rows, :], kf[rows, :], v_ref, mo_ref, ng_ref, ml_cur, rows,
                     cn_scr, m_scr)

    y = x1_ref[...]
    ms = jnp.mean(y * y, axis=-1, keepdims=True)
    h2_ref[...] = (y * lax.rsqrt(ms + EPS) * gf_ref[...]).astype(BF16)
    ml_prev[...] = ml_cur[...]


def _mix_out(gates_t, m_f32, m_v, conv_w, conv_b, norm_g, attn, x2, w_o, g_ffn, batch, seq):
    t = batch * seq
    W = MLSTM_WIDTH
    assert seq % MIX_TILE == 0
    tiles_per_seq = seq // MIX_TILE
    n_tiles = t // MIX_TILE
    halo_per_tile = MIX_TILE // CONV_HALO

    def cur(s):
        return jnp.minimum(s, n_tiles - 1)

    def prev(s):
        return jnp.maximum(s - 1, 0)

    def halo(s):
        return jnp.maximum(cur(s) * halo_per_tile - 1, 0)

    const = lambda s: (0, 0)
    return pl.pallas_call(
        functools.partial(_mix_out_body, tiles_per_seq=tiles_per_seq, n_tiles=n_tiles),
        grid=(n_tiles + 1,),
        in_specs=[
            pl.BlockSpec((N_GATE, MIX_TILE), lambda s: (0, cur(s))),
            pl.BlockSpec((MIX_TILE, W), lambda s: (cur(s), 0)),
            pl.BlockSpec((MIX_TILE, W), lambda s: (cur(s), 1)),
            pl.BlockSpec((CONV_HALO, W), lambda s: (halo(s), 0)),
            pl.BlockSpec((CONV_HALO, W), lambda s: (halo(s), 1)),
            pl.BlockSpec((MIX_TILE, W), lambda s: (cur(s), 0)),
            pl.BlockSpec((MIX_TILE, W), lambda s: (cur(s), 2)),
            pl.BlockSpec((CONV_WIDTH, 2 * W), const),
            pl.BlockSpec((1, 2 * W), const),
            pl.BlockSpec((1, W), const),
            pl.BlockSpec((ATTN_HEADS, MIX_TILE, HEAD_DIM), lambda s: (0, prev(s), 0)),
            pl.BlockSpec((MIX_TILE, D_MODEL), lambda s: (prev(s), 0)),
            pl.BlockSpec((ATTN_WIDTH + MLSTM_WIDTH, D_MODEL), const, pipeline_mode=pl.Buffered(1)),
            pl.BlockSpec((1, D_MODEL), const),
        ],
        out_specs=[
            pl.BlockSpec((MIX_TILE, D_MODEL), lambda s: (prev(s), 0)),
            pl.BlockSpec((MIX_TILE, D_MODEL), lambda s: (prev(s), 0)),
        ],
        out_shape=[
            jax.ShapeDtypeStruct((t, D_MODEL), F32),
            jax.ShapeDtypeStruct((t, D_MODEL), BF16),
        ],
        scratch_shapes=[pltpu.VMEM((MLSTM_HEADS, HEAD_DIM, 2 * HEAD_DIM), F32),
                        pltpu.VMEM((MLSTM_HEADS, LANES), F32),
                        pltpu.VMEM((CONV_HALO + MIX_TILE, W), F32),
                        pltpu.VMEM((CONV_HALO + MIX_TILE, W), F32),
                        pltpu.VMEM((MIX_TILE, W), BF16),
                        pltpu.VMEM((MIX_TILE, W), BF16)],
        compiler_params=pltpu.CompilerParams(
            dimension_semantics=("arbitrary",),
            vmem_limit_bytes=VMEM_LIMIT_BYTES),
        name="mlstm_out_proj",
    )(gates_t, m_f32, m_f32, m_f32, m_f32, m_v, m_f32, conv_w, conv_b, norm_g,
      attn, x2, w_o, g_ffn)


def kernel(x, norm_mix_g, w_in, conv_w, conv_b, gate_b, q_norm_g, k_norm_g, mlstm_norm_g,
           w_out, norm_ffn_g, w_gate, w_up, w_down):
    B, S, _ = x.shape
    t = B * S
    layer = 0
    x2 = x.reshape(t, D_MODEL)

    w_t = w_in[layer].T
    a3 = 3 * ATTN_WIDTH
    mw = MLSTM_WIDTH
    n_main = a3 + 4 * mw
    w_main = _cast_rows_bf16(w_t, n_main, 512)
    w_gate_t = w_t[n_main:]
    b_gate = gate_b[layer].reshape(N_GATE, 1)
    head_gain = jnp.stack([
        jnp.tile(q_norm_g[layer] * (HEAD_DIM ** -0.5 * LOG2_E), ATTN_HEADS),
        jnp.tile(k_norm_g[layer], ATTN_HEADS)]).reshape(2, 1, ATTN_WIDTH)

    att, m_f32, m_v, gates_t = _in_proj(
        x2, norm_mix_g[layer].reshape(1, D_MODEL), w_main, w_gate_t, b_gate, head_gain, (), tm=1024)

    attn_out, wo_b, wg_b, wu_b, wd_b = _dilated_attention(
        att, (w_out[layer], w_gate[layer], w_up[layer], w_down[layer]), B, S)

    x1, h2 = _mix_out(gates_t, m_f32, m_v, conv_w[layer], conv_b[layer].reshape(1, 2 * mw),
                      mlstm_norm_g[layer].reshape(1, mw), attn_out, x2, wo_b,
                      norm_ffn_g[layer].reshape(1, D_MODEL), B, S)

    out = _ffn(h2, wg_b, wu_b, wd_b, x1, tm=1024, th=512)
    return out.reshape(B, S, D_MODEL)
```

```python
import functools

import jax
import jax.numpy as jnp
from jax import lax
from jax.experimental import pallas as pl
from jax.experimental.pallas import tpu as pltpu

D_MODEL = 2048
HEAD_DIM = 128
ATTN_HEADS = 8
MLSTM_HEADS = 8
ATTN_WIDTH = ATTN_HEADS * HEAD_DIM
MLSTM_WIDTH = MLSTM_HEADS * HEAD_DIM
ATTN_PATTERNS = ((128, 1), (512, 4), (2048, 16))
ATTN_BLOCK = 128
MLSTM_CHUNK = 128
CONV_WIDTH = 4
FFN_HIDDEN = 5632
EPS = 1e-6
MASK_VALUE = -1e30
M_INIT = -1e30

LOG2_E = 1.4426950408889634
N_GATE = 2 * MLSTM_HEADS
LANES = 128
V7X_VMEM_BYTES = 64 * 1024 * 1024
VMEM_LIMIT_BYTES = V7X_VMEM_BYTES - 5 * 1024 * 1024

F32 = jnp.float32
BF16 = jnp.bfloat16


def _cast_body(w_ref, o_ref):
    o_ref[...] = w_ref[...].astype(BF16)


def _cast_rows_bf16(w, n_rows, tr):
    cols = w.shape[1]
    return pl.pallas_call(
        _cast_body,
        grid=(n_rows // tr,),
        in_specs=[pl.BlockSpec((tr, cols), lambda j: (j, 0))],
        out_specs=pl.BlockSpec((tr, cols), lambda j: (j, 0)),
        out_shape=jax.ShapeDtypeStruct((n_rows, cols), BF16),
        compiler_params=pltpu.CompilerParams(
            dimension_semantics=("arbitrary",),
            vmem_limit_bytes=VMEM_LIMIT_BYTES),
        name="cast_w_in",
    )(w)


def _side_cast_specs(weights, n_step, step_of):
    specs, shapes = [], []
    for wt in weights:
        rows, cols = wt.shape
        n_blk = max(n for n in range(1, n_step + 1)
                    if rows % n == 0 and (rows // n) % BF16_SUBLANES == 0)
        specs.append(pl.BlockSpec(
            (rows // n_blk, cols),
            lambda *idx, n_blk=n_blk: (jnp.minimum(step_of(*idx), n_blk - 1), 0)))
        shapes.append(jax.ShapeDtypeStruct(wt.shape, BF16))
    return specs, shapes


def _side_cast(cast_in, cast_out):
    for src, dst in zip(cast_in, cast_out):
        dst[...] = src[...].astype(BF16)


def _in_proj_body(x_ref, g_ref, w_ref, wg_ref, bg_ref, hg_ref, *rest, n_cast):
    cast_in, rest = rest[:n_cast], rest[n_cast:]
    oatt_ref, omf_ref, omv_ref, ogate_ref = rest[:4]
    cast_out, h_scr = rest[4:4 + n_cast], rest[4 + n_cast]
    j = pl.program_id(1)

    _side_cast(cast_in, cast_out)

    nt = (((1,), (1,)), ((), ()))

    def project(h=None):
        h = h_scr[...] if h is None else h
        return lax.dot_general(h, w_ref[...], nt, preferred_element_type=F32)

    def store_head_normed(acc):
        for hd in range(ATTN_HEADS):
            sl = slice(hd * HEAD_DIM, (hd + 1) * HEAD_DIM)
            a = acc[:, sl]
            ms = jnp.mean(a * a, axis=-1, keepdims=True)
            oatt_ref[hd] = (a * lax.rsqrt(ms + EPS) * hg_ref[:, sl]).astype(BF16)

    @pl.when(j == 0)
    def _():
        x = x_ref[...]
        ms = jnp.mean(x * x, axis=-1, keepdims=True)
        hb = (x * lax.rsqrt(ms + EPS) * g_ref[...]).astype(BF16)
        h_scr[...] = hb
        ogate_ref[...] = lax.dot_general(wg_ref[...].astype(BF16), hb, nt,
                                         preferred_element_type=F32) + bg_ref[...]
        store_head_normed(project(hb))

    @pl.when(j == 1)
    def _():
        store_head_normed(project())

    @pl.when(j == 2)
    def _():
        acc = project()
        for hd in range(ATTN_HEADS):
            oatt_ref[hd] = acc[:, hd * HEAD_DIM:(hd + 1) * HEAD_DIM].astype(BF16)

    @pl.when(jnp.logical_and(j >= 3, j < 6))
    def _():
        omf_ref[...] = project()

    @pl.when(j == 6)
    def _():
        omv_ref[...] = project().astype(BF16)


def _in_proj(x2, g_mix, w_main, w_gate_t, b_gate, head_gain, side_weights, tm):
    t = x2.shape[0]
    tn = ATTN_WIDTH
    n_col = w_main.shape[0] // tn
    n_step = (t // tm) * n_col

    def w_col(j):
        return jnp.where(j == 5, 6, jnp.where(j == 6, 5, j))

    cast_specs, cast_shapes = _side_cast_specs(side_weights, n_step, lambda i, j: i * n_col + j)

    return pl.pallas_call(
        functools.partial(_in_proj_body, n_cast=len(side_weights)),
        grid=(t // tm, n_col),
        in_specs=[
            pl.BlockSpec((tm, D_MODEL), lambda i, j: (i, 0)),
            pl.BlockSpec((1, D_MODEL), lambda i, j: (0, 0)),
            pl.BlockSpec((tn, D_MODEL), lambda i, j: (w_col(j), 0)),
            pl.BlockSpec((N_GATE, D_MODEL), lambda i, j: (0, 0)),
            pl.BlockSpec((N_GATE, 1), lambda i, j: (0, 0)),
            pl.BlockSpec((None, 1, tn), lambda i, j: (jnp.minimum(j, 1), 0, 0)),
        ] + cast_specs,
        out_specs=[
            pl.BlockSpec((None, ATTN_HEADS, tm, HEAD_DIM), lambda i, j: (jnp.minimum(j, 2), 0, i, 0)),
            pl.BlockSpec((tm, tn), lambda i, j: (i, jnp.clip(j - 3, 0, 2))),
            pl.BlockSpec((tm, tn), lambda i, j: (i, 0)),
            pl.BlockSpec((N_GATE, tm), lambda i, j: (0, i)),
        ] + cast_specs,
        out_shape=[
            jax.ShapeDtypeStruct((3, ATTN_HEADS, t, HEAD_DIM), BF16),
            jax.ShapeDtypeStruct((t, 3 * tn), F32),
            jax.ShapeDtypeStruct((t, tn), BF16),
            jax.ShapeDtypeStruct((N_GATE, t), F32),
        ] + cast_shapes,
        scratch_shapes=[pltpu.VMEM((tm, D_MODEL), BF16)],
        compiler_params=pltpu.CompilerParams(
            dimension_semantics=("arbitrary", "arbitrary"),
            vmem_limit_bytes=VMEM_LIMIT_BYTES),
        name="in_proj",
    )(x2, g_mix, w_main, w_gate_t, b_gate, head_gain, *side_weights)


FFN_DOWN_CHUNK = 512


def _ffn_body(h_ref, wg_ref, wu_ref, wd_ref, x1_ref, o_ref, *, n_res):
    c = pl.program_id(1)

    def add_residual():
        slab = x1_ref.shape[0]
        rows = pl.ds(pl.multiple_of(c * slab, slab), slab)
        o_ref[rows, :] += x1_ref[...]

    def hidden_chunk(first):
        h = h_ref[...]
        th = wg_ref.shape[1]
        acts = []
        for k0 in range(0, th, th // 2):
            kc = slice(k0, k0 + th // 2)
            g = jnp.dot(h, wg_ref[:, kc], preferred_element_type=F32)
            u = jnp.dot(h, wu_ref[:, kc], preferred_element_type=F32)
            acts.append((g * jax.nn.sigmoid(g) * u).astype(BF16))
        a = jnp.concatenate(acts, axis=1)
        for n0 in range(0, D_MODEL, FFN_DOWN_CHUNK):
            cols = slice(n0, n0 + FFN_DOWN_CHUNK)
            d = jnp.dot(a, wd_ref[:, cols], preferred_element_type=F32)
            if first:
                o_ref[:, cols] = d
            else:
                o_ref[:, cols] += d

    @pl.when(c == 0)
    def _():
        hidden_chunk(True)
        add_residual()

    @pl.when(c > 0)
    def _():
        pl.when(c < n_res)(add_residual)
        hidden_chunk(False)


def _ffn(h2, w_g, w_u, w_d, x1, tm, th):
    t = h2.shape[0]
    n_chunk = FFN_HIDDEN // th
    n_res = 8
    assert n_res <= n_chunk and tm % n_res == 0
    return pl.pallas_call(
        functools.partial(_ffn_body, n_res=n_res),
        grid=(t // tm, n_chunk),
        in_specs=[
            pl.BlockSpec((tm, D_MODEL), lambda i, c: (i, 0)),
            pl.BlockSpec((D_MODEL, th), lambda i, c: (0, c)),
            pl.BlockSpec((D_MODEL, th), lambda i, c: (0, c)),
            pl.BlockSpec((th, D_MODEL), lambda i, c: (c, 0)),
            pl.BlockSpec((tm // n_res, D_MODEL), lambda i, c: (i * n_res + jnp.minimum(c, n_res - 1), 0)),
        ],
        out_specs=pl.BlockSpec((tm, D_MODEL), lambda i, c: (i, 0)),
        out_shape=jax.ShapeDtypeStruct((t, D_MODEL), F32),
        compiler_params=pltpu.CompilerParams(
            dimension_semantics=("arbitrary", "arbitrary"),
            vmem_limit_bytes=VMEM_LIMIT_BYTES),
        name="ffn",
    )(h2, w_g, w_u, w_d, x1)


ATTN_GROUPS = 16
ATTN_TILE = ATTN_BLOCK * ATTN_GROUPS
TILE_LANES = ATTN_GROUPS * HEAD_DIM
F32_SUBLANES = 8
BF16_SUBLANES = 16
STAGE_PITCH = 24


def _band_mask(slab, n_slab, span):
    blk = ATTN_BLOCK
    shift = slab.bit_length() - 1
    row = lax.broadcasted_iota(jnp.int32, (blk, 2 * blk), 0)
    col = lax.broadcasted_iota(jnp.int32, (blk, 2 * blk), 1)

    def pos(r):
        return (r & (slab - 1)) * n_slab + lax.shift_right_logical(r, shift)

    k_rel = pos(col & (blk - 1)) - jnp.where(col < blk, blk, 0)
    dist = pos(row) - k_rel
    return jnp.logical_and(dist >= 0, dist <= span), col


def _attn_body(*refs, depth, n_cast):
    q_in, k_in, v_in = refs[:3]
    cast_in, o_ref, cast_out = refs[3:3 + n_cast], refs[3 + n_cast], refs[4 + n_cast:4 + 2 * n_cast]
    (acc_scr, m_scr, l_scr, stage_q, stage_k, stage_v, q_ref, k_ref, v_ref, qf_scr, kf_scr, vf_scr,
     kp_ref, vp_ref, kpf_scr, vpf_scr) = refs[4 + 2 * n_cast:]
    jt = pl.program_id(2)
    blk = ATTN_BLOCK
    ones_v = jnp.ones((2 * blk, HEAD_DIM), BF16)
    _side_cast(cast_in, cast_out)

    def lanes(g):
        return slice(g * HEAD_DIM, (g + 1) * HEAD_DIM)

    @pl.when(jt == 0)
    def _():
        kp_ref[...] = jnp.zeros_like(kp_ref)
        vp_ref[...] = jnp.zeros_like(vp_ref)
        kpf_scr[...] = jnp.zeros_like(kpf_scr)
        vpf_scr[...] = jnp.zeros_like(vpf_scr)

    for src, stage, dst_bf16, dst_f32 in ((q_in, stage_q, q_ref, qf_scr),
                                          (k_in, stage_k, k_ref, kf_scr),
                                          (v_in, stage_v, v_ref, vf_scr)):
        for i in range(blk):
            stage[STAGE_PITCH * i:STAGE_PITCH * i + ATTN_GROUPS, :] = (
                src[ATTN_GROUPS * i:ATTN_GROUPS * (i + 1), :].astype(F32))
        for g in range(ATTN_GROUPS):
            cls = stage[pl.ds(g, blk, stride=STAGE_PITCH), :]
            dst_f32[:, lanes(g)] = cls
            dst_bf16[:, lanes(g)] = cls.astype(BF16)

    n_pat = len(ATTN_PATTERNS)

    def pattern_work(pi, window, dil):
        span = window // dil
        n_slab = ATTN_GROUPS // dil
        slab = blk // n_slab
        use_f32 = slab % BF16_SUBLANES != 0
        band, col = _band_mask(slab, n_slab, span)
        band_first = jnp.logical_and(band, jnp.logical_or(col >= blk, jt > 0))

        def gather(cur_ref, cur_f32, prev_ref, prev_f32, res, kb, prev):
            if not prev:
                src, rows = (cur_f32 if use_f32 else cur_ref), slice(slab * kb, slab * (kb + 1))
            elif kb > 0:
                src, rows = (cur_f32 if use_f32 else cur_ref), slice(slab * (kb - 1), slab * kb)
            elif use_f32:
                src, rows = prev_f32, slice(F32_SUBLANES - slab, F32_SUBLANES)
            else:
                src, rows = prev_ref, slice(blk - slab, blk)
            parts = [src[rows, lanes(u * dil + res)] for u in range(n_slab)]
            out = parts[0] if n_slab == 1 else jnp.concatenate(parts, axis=0)
            return out.astype(BF16)

        def st_load(scr, res, kb):
            rows = slice(slab * kb, slab * (kb + 1))
            parts = [scr[rows, lanes(u * dil + res)] for u in range(n_slab)]
            return parts[0] if n_slab == 1 else jnp.concatenate(parts, axis=0)

        def st_store(scr, res, kb, val):
            rows = slice(slab * kb, slab * (kb + 1))
            for u in range(n_slab):
                scr[rows, lanes(u * dil + res)] = val[u * slab:(u + 1) * slab, :]

        def scores_of(res, kb):
            q = gather(q_ref, qf_scr, None, None, res, kb, False)
            kw = jnp.concatenate([gather(k_ref, kf_scr, kp_ref, kpf_scr, res, kb, True),
                                  gather(k_ref, kf_scr, kp_ref, kpf_scr, res, kb, False)], axis=0)
            s = lax.dot_general(q, kw, (((1,), (1,)), ((), ())), preferred_element_type=F32)
            return jnp.where(band_first if kb == 0 else band, s, MASK_VALUE)

        def finish(res, kb, s):
            m = jnp.max(s, axis=-1, keepdims=True)
            p = jnp.exp2(s - m).astype(BF16)
            vw = jnp.concatenate([gather(v_ref, vf_scr, vp_ref, vpf_scr, res, kb, True),
                                  gather(v_ref, vf_scr, vp_ref, vpf_scr, res, kb, False)], axis=0)
            pv = jnp.dot(p, jnp.concatenate([vw, ones_v], axis=1), preferred_element_type=F32)
            num, den = pv[:, :HEAD_DIM], pv[:, HEAD_DIM:]
            m_new = jnp.broadcast_to(m, (blk, HEAD_DIM))
            if pi > 0:
                m_old = st_load(m_scr, res, kb)
                m_new = jnp.maximum(m_old, m_new)
                w_old = jnp.exp2(m_old - m_new)
                w_new = jnp.exp2(m - m_new)
                num = st_load(acc_scr, res, kb) * w_old + num * w_new
                den = st_load(l_scr, res, kb) * w_old + den * w_new
            if pi < n_pat - 1:
                st_store(acc_scr, res, kb, num)
                st_store(m_scr, res, kb, m_new)
                st_store(l_scr, res, kb, den)
            else:
                st_store(acc_scr, res, kb, num / den)

        return [(scores_of, finish, (res, kb)) for res in range(dil) for kb in range(n_slab)]

    work = [w for pi, (window, dil) in enumerate(reversed(ATTN_PATTERNS))
            for w in pattern_work(pi, window, dil)]
    pending = []
    for t in range(len(work) + depth):
        if t < len(work):
            scores_of, _, block = work[t]
            pending.append(scores_of(*block))
        if t >= depth:
            _, finish, block = work[t - depth]
            finish(*block, pending.pop(0))

    for g in range(ATTN_GROUPS):
        stage_q[pl.ds(g, blk, stride=STAGE_PITCH), :] = acc_scr[:, lanes(g)]
    for i in range(blk):
        o_ref[ATTN_GROUPS * i:ATTN_GROUPS * (i + 1), :] = (
            stage_q[STAGE_PITCH * i:STAGE_PITCH * i + ATTN_GROUPS, :].astype(BF16))

    kp_ref[...] = k_ref[...]
    vp_ref[...] = v_ref[...]
    kpf_scr[...] = kf_scr[blk - F32_SUBLANES:, :]
    vpf_scr[...] = vf_scr[blk - F32_SUBLANES:, :]


def _dilated_attention(att, side_weights, batch, seq, depth=3):
    assert seq % ATTN_TILE == 0
    n_tile = seq // ATTN_TILE
    cast_specs, cast_shapes = _side_cast_specs(
        side_weights, batch * ATTN_HEADS * n_tile,
        lambda bi, hd, jt: (bi * ATTN_HEADS + hd) * n_tile + jt)

    def spec(which):
        return pl.BlockSpec((None, None, ATTN_TILE, HEAD_DIM),
                            lambda bi, hd, jt: (which, hd, bi * n_tile + jt, 0))

    tile_f32 = pltpu.VMEM((ATTN_BLOCK, TILE_LANES), F32)
    tile_bf16 = pltpu.VMEM((ATTN_BLOCK, TILE_LANES), BF16)
    tail_f32 = pltpu.VMEM((F32_SUBLANES, TILE_LANES), F32)
    stage = pltpu.VMEM((ATTN_BLOCK * STAGE_PITCH, HEAD_DIM), F32)
    return pl.pallas_call(
        functools.partial(_attn_body, depth=depth, n_cast=len(side_weights)),
        grid=(batch, ATTN_HEADS, n_tile),
        in_specs=[spec(0), spec(1), spec(2)] + cast_specs,
        out_specs=[pl.BlockSpec((None, ATTN_TILE, HEAD_DIM),
                                lambda bi, hd, jt: (hd, bi * n_tile + jt, 0))] + cast_specs,
        out_shape=[jax.ShapeDtypeStruct((ATTN_HEADS, batch * seq, HEAD_DIM), BF16)] + cast_shapes,
        scratch_shapes=[tile_f32, tile_f32, tile_f32,
                        stage, stage, stage,
                        tile_bf16, tile_bf16, tile_bf16,
                        tile_f32, tile_f32, tile_f32,
                        tile_bf16, tile_bf16,
                        tail_f32, tail_f32],
        compiler_params=pltpu.CompilerParams(
            dimension_semantics=("arbitrary", "arbitrary", "arbitrary"),
            vmem_limit_bytes=VMEM_LIMIT_BYTES),
        name="dilated_attn",
    )(att, att, att, *side_weights)


CONV_HALO = 8


def _mlstm_chunk(gt, qb, kf, v_ref, mo_ref, ng_ref, o_ref, rows, cn_scr, m_scr):
    L = MLSTM_CHUNK

    ig = gt[:MLSTM_HEADS]
    fg = gt[MLSTM_HEADS:]
    lf = jnp.minimum(fg, 0.0) - jnp.log1p(jnp.exp(-jnp.abs(fg)))
    lane = lax.broadcasted_iota(jnp.int32, (MLSTM_HEADS, L), 1)
    b = lf
    s = 1
    while s < L:
        b = b + jnp.where(lane >= s, pltpu.roll(b, s, 1), 0.0)
        s *= 2
    c_row = ig - b
    b_last = b[:, L - 1:L]
    c_max = jnp.max(c_row, axis=1, keepdims=True)
    m_prev = m_scr[:, :1]
    m_chunk = b_last + c_max
    m_new = jnp.maximum(b_last + m_prev, m_chunk)
    decay = jnp.exp(b_last + m_prev - m_new)
    scale = jnp.exp(m_chunk - m_new)
    wa = jnp.exp(c_row - c_max)
    m_scr[...] = jnp.broadcast_to(m_new, m_scr.shape)

    row = lax.broadcasted_iota(jnp.int32, (L, L), 0)
    col = lax.broadcasted_iota(jnp.int32, (L, L), 1)
    causal = col <= row
    ones_v = jnp.ones((L, HEAD_DIM), BF16)

    heads = range(MLSTM_HEADS)

    def hs(hd):
        return slice(hd * HEAD_DIM, (hd + 1) * HEAD_DIM)

    def one(hd):
        return slice(hd, hd + 1)

    k_ts, states, s_qks, inters = [], [], [], []
    for hd in heads:
        q_h = qb[:, hs(hd)]
        k_t = kf[:, hs(hd)].T
        state = cn_scr[hd]
        k_ts.append(k_t)
        states.append(state)
        s_qks.append(jnp.dot(q_h, k_t.astype(BF16), preferred_element_type=F32))
        inters.append(jnp.dot(q_h, state.astype(BF16), preferred_element_type=F32))

    ws, mus, gs = [], [], []
    for hd in heads:
        c_b = jnp.where(causal, jnp.broadcast_to(c_row[one(hd), :], (L, L)), MASK_VALUE)
        mu = jnp.maximum(jnp.max(c_b, axis=-1, keepdims=True), m_prev[one(hd), :])
        ws.append(jnp.exp(c_b - mu) * s_qks[hd])
        mus.append(mu)
        gs.append(jnp.exp(m_prev[one(hd), :] - mu))

    ones_sq = jnp.ones((L, L), BF16)

    def split(x):
        hi = x.astype(BF16)
        return hi, (x - hi.astype(F32)).astype(BF16)

    lf_rows = jnp.concatenate([jnp.broadcast_to(lf[one(hd), :], (HEAD_DIM, L)) for hd in heads], axis=0)
    causal_b = jnp.where(causal, 1.0, 0.0).astype(BF16)
    nt = (((1,), (1,)), ((), ()))
    lf_hi, lf_lo = split(lf_rows)
    b_all = (lax.dot_general(causal_b, lf_hi, nt, preferred_element_type=F32)
             + lax.dot_general(causal_b, lf_lo, nt, preferred_element_type=F32))

    intras, kvns, sum_los = [], [], []
    for hd in heads:
        v_aug = jnp.concatenate([v_ref[rows, hs(hd)], ones_v], axis=1)
        w_hi, w_lo = split(ws[hd])
        intras.append(jnp.dot(w_hi, v_aug, preferred_element_type=F32))
        sum_los.append(jnp.dot(w_lo, ones_sq, preferred_element_type=F32))
        a_t = (k_ts[hd] * jnp.broadcast_to(wa[one(hd), :], (HEAD_DIM, L))).astype(BF16)
        kvns.append(jnp.dot(a_t, v_aug, preferred_element_type=F32))

    cells = []
    for hd in heads:
        g, mu = gs[hd], mus[hd]
        num = g * inters[hd][:, :HEAD_DIM] + intras[hd][:, :HEAD_DIM]
        den = g * inters[hd][:, HEAD_DIM:] + intras[hd][:, HEAD_DIM:] + sum_los[hd]
        cells.append(num / jnp.maximum(jnp.abs(den), jnp.exp(-(b_all[:, hs(hd)] + mu))))

    sq_sums = []
    for hd in heads:
        sq = (cells[hd] * cells[hd]).astype(BF16)
        sq_sums.append(jnp.dot(sq, ones_sq, preferred_element_type=F32))
    for hd in heads:
        y = cells[hd] * lax.rsqrt(sq_sums[hd] * (1.0 / HEAD_DIM) + EPS) * ng_ref[:, hs(hd)]
        o_ref[rows, hs(hd)] = (jax.nn.sigmoid(mo_ref[rows, hs(hd)]) * y).astype(BF16)
        cn_scr[hd] = decay[one(hd), :] * states[hd] + scale[one(hd), :] * kvns[hd]


MIX_TILE = 4 * MLSTM_CHUNK


def _mix_out_body(gt_ref, q_ref, k_ref, qp_ref, kp_ref, v_ref, mo_ref, cw_ref, cb_ref, ng_ref,
                  a_ref, x_ref, wo_ref, gf_ref, x1_ref, h2_ref,
                  cn_scr, m_scr, qext_scr, kext_scr, ml_cur, ml_prev, *, tiles_per_seq, n_tiles):
    s = pl.program_id(0)
    L = MLSTM_CHUNK
    W = MLSTM_WIDTH
    n_chunk = MIX_TILE // L
    tile = jnp.minimum(s, n_tiles - 1)
    seq_start = (tile % tiles_per_seq) == 0

    @pl.when(s == 0)
    def _():
        ml_prev[...] = jnp.zeros_like(ml_prev)

    @pl.when(seq_start)
    def _():
        cn_scr[...] = jnp.zeros_like(cn_scr)
        m_scr[...] = jnp.full_like(m_scr, M_INIT)

    has_prev = jnp.logical_not(seq_start)

    def conv_silu(cur_ref, prev_ref, ext, col0):
        ext[:CONV_HALO, :] = jnp.where(has_prev, prev_ref[...], 0.0)
        ext[CONV_HALO:, :] = cur_ref[...]
        y = cb_ref[:, col0:col0 + W]
        for sh in range(CONV_WIDTH):
            tap = cw_ref[CONV_WIDTH - 1 - sh:CONV_WIDTH - sh, col0:col0 + W]
            y = y + ext[CONV_HALO - sh:CONV_HALO - sh + MIX_TILE, :] * tap
        return y * jax.nn.sigmoid(y)

    qb = conv_silu(q_ref, qp_ref, qext_scr, 0).astype(BF16)
    kf = conv_silu(k_ref, kp_ref, kext_scr, W) * (HEAD_DIM ** -0.5)

    mix_prev = jnp.concatenate([a_ref[hd] for hd in range(ATTN_HEADS)] + [ml_prev[...]], axis=1)
    n_slab = D_MODEL // n_chunk
    for ck in range(n_chunk):
        cols = slice(ck * n_slab, (ck + 1) * n_slab)
        x1_ref[:, cols] = (jnp.dot(mix_prev, wo_ref[:, cols], preferred_element_type=F32)
                           + x_ref[:, cols])
        rows = slice(ck * L, (ck + 1) * L)
        _mlstm_chunk(gt_ref[:, rows], qb[rows, :], kf[rows, :], v_ref, mo_ref, ng_ref, ml_cur, rows,
                     cn_scr, m_scr)

    y = x1_ref[...]
    ms = jnp.mean(y * y, axis=-1, keepdims=True)
    h2_ref[...] = (y * lax.rsqrt(ms + EPS) * gf_ref[...]).astype(BF16)
    ml_prev[...] = ml_cur[...]


def _mix_out(gates_t, m_f32, m_v, conv_w, conv_b, norm_g, attn, x2, w_o, g_ffn, batch, seq):
    t = batch * seq
    W = MLSTM_WIDTH
    assert seq % MIX_TILE == 0
    tiles_per_seq = seq // MIX_TILE
    n_tiles = t // MIX_TILE
    halo_per_tile = MIX_TILE // CONV_HALO

    def cur(s):
        return jnp.minimum(s, n_tiles - 1)

    def prev(s):
        return jnp.maximum(s - 1, 0)

    def halo(s):
        return jnp.maximum(cur(s) * halo_per_tile - 1, 0)

    const = lambda s: (0, 0)
    return pl.pallas_call(
        functools.partial(_mix_out_body, tiles_per_seq=tiles_per_seq, n_tiles=n_tiles),
        grid=(n_tiles + 1,),
        in_specs=[
            pl.BlockSpec((N_GATE, MIX_TILE), lambda s: (0, cur(s))),
            pl.BlockSpec((MIX_TILE, W), lambda s: (cur(s), 0)),
            pl.BlockSpec((MIX_TILE, W), lambda s: (cur(s), 1)),
            pl.BlockSpec((CONV_HALO, W), lambda s: (halo(s), 0)),
            pl.BlockSpec((CONV_HALO, W), lambda s: (halo(s), 1)),
            pl.BlockSpec((MIX_TILE, W), lambda s: (cur(s), 0)),
            pl.BlockSpec((MIX_TILE, W), lambda s: (cur(s), 2)),
            pl.BlockSpec((CONV_WIDTH, 2 * W), const),
            pl.BlockSpec((1, 2 * W), const),
            pl.BlockSpec((1, W), const),
            pl.BlockSpec((ATTN_HEADS, MIX_TILE, HEAD_DIM), lambda s: (0, prev(s), 0)),
            pl.BlockSpec((MIX_TILE, D_MODEL), lambda s: (prev(s), 0)),
            pl.BlockSpec((ATTN_WIDTH + MLSTM_WIDTH, D_MODEL), const, pipeline_mode=pl.Buffered(1)),
            pl.BlockSpec((1, D_MODEL), const),
        ],
        out_specs=[
            pl.BlockSpec((MIX_TILE, D_MODEL), lambda s: (prev(s), 0)),
            pl.BlockSpec((MIX_TILE, D_MODEL), lambda s: (prev(s), 0)),
        ],
        out_shape=[
            jax.ShapeDtypeStruct((t, D_MODEL), F32),
            jax.ShapeDtypeStruct((t, D_MODEL), BF16),
        ],
        scratch_shapes=[pltpu.VMEM((MLSTM_HEADS, HEAD_DIM, 2 * HEAD_DIM), F32),
                        pltpu.VMEM((MLSTM_HEADS, LANES), F32),
                        pltpu.VMEM((CONV_HALO + MIX_TILE, W), F32),
                        pltpu.VMEM((CONV_HALO + MIX_TILE, W), F32),
                        pltpu.VMEM((MIX_TILE, W), BF16),
                        pltpu.VMEM((MIX_TILE, W), BF16)],
        compiler_params=pltpu.CompilerParams(
            dimension_semantics=("arbitrary",),
            vmem_limit_bytes=VMEM_LIMIT_BYTES),
        name="mlstm_out_proj",
    )(gates_t, m_f32, m_f32, m_f32, m_f32, m_v, m_f32, conv_w, conv_b, norm_g,
      attn, x2, w_o, g_ffn)


def kernel(x, norm_mix_g, w_in, conv_w, conv_b, gate_b, q_norm_g, k_norm_g, mlstm_norm_g,
           w_out, norm_ffn_g, w_gate, w_up, w_down):
    B, S, _ = x.shape
    t = B * S
    layer = 0
    x2 = x.reshape(t, D_MODEL)

    w_t = w_in[layer].T
    a3 = 3 * ATTN_WIDTH
    mw = MLSTM_WIDTH
    n_main = a3 + 4 * mw
    w_main = _cast_rows_bf16(w_t, n_main, 512)
    w_gate_t = w_t[n_main:]
    b_gate = gate_b[layer].reshape(N_GATE, 1)
    head_gain = jnp.stack([
        jnp.tile(q_norm_g[layer] * (HEAD_DIM ** -0.5 * LOG2_E), ATTN_HEADS),
        jnp.tile(k_norm_g[layer], ATTN_HEADS)]).reshape(2, 1, ATTN_WIDTH)

    att, m_f32, m_v, gates_t = _in_proj(
        x2, norm_mix_g[layer].reshape(1, D_MODEL), w_main, w_gate_t, b_gate, head_gain, (), tm=1024)

    attn_out, wo_b, wg_b, wu_b, wd_b = _dilated_attention(
        att, (w_out[layer], w_gate[layer], w_up[layer], w_down[layer]), B, S)

    x1, h2 = _mix_out(gates_t, m_f32, m_v, conv_w[layer], conv_b[layer].reshape(1, 2 * mw),
                      mlstm_norm_g[layer].reshape(1, mw), attn_out, x2, wo_b,
                      norm_ffn_g[layer].reshape(1, D_MODEL), B, S)

    out = _ffn(h2, wg_b, wu_b, wd_b, x1, tm=1024, th=512)
    return out.reshape(B, S, D_MODEL)
```

```python
import functools

import jax
import jax.numpy as jnp
from jax import lax
from jax.experimental import pallas as pl
from jax.experimental.pallas import tpu as pltpu

D_MODEL = 2048
HEAD_DIM = 128
ATTN_HEADS = 8
MLSTM_HEADS = 8
ATTN_WIDTH = ATTN_HEADS * HEAD_DIM
MLSTM_WIDTH = MLSTM_HEADS * HEAD_DIM
ATTN_PATTERNS = ((128, 1), (512, 4), (2048, 16))
ATTN_BLOCK = 128
MLSTM_CHUNK = 128
CONV_WIDTH = 4
FFN_HIDDEN = 5632
EPS = 1e-6
MASK_VALUE = -1e30
M_INIT = -1e30

LOG2_E = 1.4426950408889634
N_GATE = 2 * MLSTM_HEADS
LANES = 128
V7X_VMEM_BYTES = 64 * 1024 * 1024
VMEM_LIMIT_BYTES = V7X_VMEM_BYTES - 5 * 1024 * 1024

F32 = jnp.float32
BF16 = jnp.bfloat16


def _cast_body(w_ref, o_ref):
    o_ref[...] = w_ref[...].astype(BF16)


def _cast_rows_bf16(w, n_rows, tr):
    cols = w.shape[1]
    return pl.pallas_call(
        _cast_body,
        grid=(n_rows // tr,),
        in_specs=[pl.BlockSpec((tr, cols), lambda j: (j, 0))],
        out_specs=pl.BlockSpec((tr, cols), lambda j: (j, 0)),
        out_shape=jax.ShapeDtypeStruct((n_rows, cols), BF16),
        compiler_params=pltpu.CompilerParams(
            dimension_semantics=("arbitrary",),
            vmem_limit_bytes=VMEM_LIMIT_BYTES),
        name="cast_w_in",
    )(w)


def _side_cast_specs(weights, n_step, step_of, col_chunks=None):
    col_chunks = col_chunks or (None,) * len(weights)
    in_specs, out_specs, shapes = [], [], []
    for wt, chunk in zip(weights, col_chunks):
        rows, cols = wt.shape
        n_blk = max(n for n in range(1, n_step + 1)
                    if rows % n == 0 and (rows // n) % BF16_SUBLANES == 0)

        def slab(*idx, n_blk=n_blk):
            return jnp.minimum(step_of(*idx), n_blk - 1)

        in_specs.append(pl.BlockSpec((rows // n_blk, cols), lambda *idx, slab=slab: (slab(*idx), 0)))
        if chunk is None:
            out_specs.append(pl.BlockSpec((rows // n_blk, cols), lambda *idx, slab=slab: (slab(*idx), 0)))
            shapes.append(jax.ShapeDtypeStruct((rows, cols), BF16))
        else:
            out_specs.append(pl.BlockSpec((cols // chunk, rows // n_blk, chunk),
                                          lambda *idx, slab=slab: (0, slab(*idx), 0)))
            shapes.append(jax.ShapeDtypeStruct((cols // chunk, rows, chunk), BF16))
    return in_specs, out_specs, shapes


def _side_cast(cast_in, cast_out):
    for src, dst in zip(cast_in, cast_out):
        if len(dst.shape) == 2:
            dst[...] = src[...].astype(BF16)
        else:
            chunk = dst.shape[2]
            for c in range(dst.shape[0]):
                dst[c] = src[:, c * chunk:(c + 1) * chunk].astype(BF16)


def _in_proj_body(x_ref, g_ref, w_ref, wg_ref, bg_ref, hg_ref, *rest, n_cast):
    cast_in, rest = rest[:n_cast], rest[n_cast:]
    oatt_ref, omf_ref, omv_ref, ogate_ref = rest[:4]
    cast_out, h_scr = rest[4:4 + n_cast], rest[4 + n_cast]
    j = pl.program_id(1)

    _side_cast(cast_in, cast_out)

    nt = (((1,), (1,)), ((), ()))

    def project(h=None):
        h = h_scr[...] if h is None else h
        return lax.dot_general(h, w_ref[...], nt, preferred_element_type=F32)

    def store_head_normed(acc):
        for hd in range(ATTN_HEADS):
            sl = slice(hd * HEAD_DIM, (hd + 1) * HEAD_DIM)
            a = acc[:, sl]
            ms = jnp.mean(a * a, axis=-1, keepdims=True)
            oatt_ref[hd] = (a * lax.rsqrt(ms + EPS) * hg_ref[:, sl]).astype(BF16)

    @pl.when(j == 0)
    def _():
        x = x_ref[...]
        ms = jnp.mean(x * x, axis=-1, keepdims=True)
        hb = (x * lax.rsqrt(ms + EPS) * g_ref[...]).astype(BF16)
        h_scr[...] = hb
        ogate_ref[...] = lax.dot_general(wg_ref[...].astype(BF16), hb, nt,
                                         preferred_element_type=F32) + bg_ref[...]
        store_head_normed(project(hb))

    @pl.when(j == 1)
    def _():
        store_head_normed(project())

    @pl.when(j == 2)
    def _():
        acc = project()
        for hd in range(ATTN_HEADS):
            oatt_ref[hd] = acc[:, hd * HEAD_DIM:(hd + 1) * HEAD_DIM].astype(BF16)

    @pl.when(jnp.logical_and(j >= 3, j < 6))
    def _():
        omf_ref[...] = project()

    @pl.when(j == 6)
    def _():
        omv_ref[...] = project().astype(BF16)


def _in_proj(x2, g_mix, w_main, w_gate_t, b_gate, head_gain, side_weights, tm):
    t = x2.shape[0]
    tn = ATTN_WIDTH
    n_col = w_main.shape[0] // tn
    n_step = (t // tm) * n_col

    def w_col(j):
        return jnp.where(j == 5, 6, jnp.where(j == 6, 5, j))

    cast_specs, cast_out_specs, cast_shapes = _side_cast_specs(
        side_weights, n_step, lambda i, j: i * n_col + j)

    return pl.pallas_call(
        functools.partial(_in_proj_body, n_cast=len(side_weights)),
        grid=(t // tm, n_col),
        in_specs=[
            pl.BlockSpec((tm, D_MODEL), lambda i, j: (i, 0)),
            pl.BlockSpec((1, D_MODEL), lambda i, j: (0, 0)),
            pl.BlockSpec((tn, D_MODEL), lambda i, j: (w_col(j), 0)),
            pl.BlockSpec((N_GATE, D_MODEL), lambda i, j: (0, 0)),
            pl.BlockSpec((N_GATE, 1), lambda i, j: (0, 0)),
            pl.BlockSpec((None, 1, tn), lambda i, j: (jnp.minimum(j, 1), 0, 0)),
        ] + cast_specs,
        out_specs=[
            pl.BlockSpec((None, ATTN_HEADS, tm, HEAD_DIM), lambda i, j: (jnp.minimum(j, 2), 0, i, 0)),
            pl.BlockSpec((tm, tn), lambda i, j: (i, jnp.clip(j - 3, 0, 2))),
            pl.BlockSpec((tm, tn), lambda i, j: (i, 0)),
            pl.BlockSpec((N_GATE, tm), lambda i, j: (0, i)),
        ] + cast_out_specs,
        out_shape=[
            jax.ShapeDtypeStruct((3, ATTN_HEADS, t, HEAD_DIM), BF16),
            jax.ShapeDtypeStruct((t, 3 * tn), F32),
            jax.ShapeDtypeStruct((t, tn), BF16),
            jax.ShapeDtypeStruct((N_GATE, t), F32),
        ] + cast_shapes,
        scratch_shapes=[pltpu.VMEM((tm, D_MODEL), BF16)],
        compiler_params=pltpu.CompilerParams(
            dimension_semantics=("arbitrary", "arbitrary"),
            vmem_limit_bytes=VMEM_LIMIT_BYTES),
        name="in_proj",
    )(x2, g_mix, w_main, w_gate_t, b_gate, head_gain, *side_weights)


FFN_DOWN_CHUNK = 512


def _ffn_body(h_ref, wg_ref, wu_ref, wd_ref, x1_ref, o_ref, *, n_res):
    c = pl.program_id(1)

    def add_residual():
        slab = x1_ref.shape[0]
        rows = pl.ds(pl.multiple_of(c * slab, slab), slab)
        o_ref[rows, :] += x1_ref[...]

    def hidden_chunk(first):
        h = h_ref[...]
        th = wg_ref.shape[1]
        acts = []
        for k0 in range(0, th, th // 2):
            kc = slice(k0, k0 + th // 2)
            g = jnp.dot(h, wg_ref[:, kc], preferred_element_type=F32)
            u = jnp.dot(h, wu_ref[:, kc], preferred_element_type=F32)
            acts.append((g * jax.nn.sigmoid(g) * u).astype(BF16))
        a = jnp.concatenate(acts, axis=1)
        for n0 in range(0, D_MODEL, FFN_DOWN_CHUNK):
            cols = slice(n0, n0 + FFN_DOWN_CHUNK)
            d = jnp.dot(a, wd_ref[:, cols], preferred_element_type=F32)
            if first:
                o_ref[:, cols] = d
            else:
                o_ref[:, cols] += d

    @pl.when(c == 0)
    def _():
        hidden_chunk(True)
        add_residual()

    @pl.when(c > 0)
    def _():
        pl.when(c < n_res)(add_residual)
        hidden_chunk(False)


def _ffn(h2, w_g, w_u, w_d, x1, tm, th):
    t = h2.shape[0]
    n_chunk = FFN_HIDDEN // th
    n_res = 8
    assert n_res <= n_chunk and tm % n_res == 0
    return pl.pallas_call(
        functools.partial(_ffn_body, n_res=n_res),
        grid=(t // tm, n_chunk),
        in_specs=[
            pl.BlockSpec((tm, D_MODEL), lambda i, c: (i, 0)),
            pl.BlockSpec((None, D_MODEL, th), lambda i, c: (c, 0, 0)),
            pl.BlockSpec((None, D_MODEL, th), lambda i, c: (c, 0, 0)),
            pl.BlockSpec((th, D_MODEL), lambda i, c: (c, 0)),
            pl.BlockSpec((tm // n_res, D_MODEL), lambda i, c: (i * n_res + jnp.minimum(c, n_res - 1), 0)),
        ],
        out_specs=pl.BlockSpec((tm, D_MODEL), lambda i, c: (i, 0)),
        out_shape=jax.ShapeDtypeStruct((t, D_MODEL), F32),
        compiler_params=pltpu.CompilerParams(
            dimension_semantics=("arbitrary", "arbitrary"),
            vmem_limit_bytes=VMEM_LIMIT_BYTES),
        name="ffn",
    )(h2, w_g, w_u, w_d, x1)


ATTN_GROUPS = 16
ATTN_TILE = ATTN_BLOCK * ATTN_GROUPS
TILE_LANES = ATTN_GROUPS * HEAD_DIM
F32_SUBLANES = 8
BF16_SUBLANES = 16
STAGE_PITCH = 24


def _band_mask(slab, n_slab, span):
    blk = ATTN_BLOCK
    shift = slab.bit_length() - 1
    row = lax.broadcasted_iota(jnp.int32, (blk, 2 * blk), 0)
    col = lax.broadcasted_iota(jnp.int32, (blk, 2 * blk), 1)

    def pos(r):
        return (r & (slab - 1)) * n_slab + lax.shift_right_logical(r, shift)

    k_rel = pos(col & (blk - 1)) - jnp.where(col < blk, blk, 0)
    dist = pos(row) - k_rel
    return jnp.logical_and(dist >= 0, dist <= span), col


def _attn_body(*refs, depth, n_cast):
    q_in, k_in, v_in = refs[:3]
    cast_in, o_ref, cast_out = refs[3:3 + n_cast], refs[3 + n_cast], refs[4 + n_cast:4 + 2 * n_cast]
    (acc_scr, m_scr, l_scr, stage_q, stage_k, stage_v, q_ref, k_ref, v_ref, qf_scr, kf_scr, vf_scr,
     kp_ref, vp_ref, kpf_scr, vpf_scr) = refs[4 + 2 * n_cast:]
    jt = pl.program_id(2)
    blk = ATTN_BLOCK
    ones_v = jnp.ones((2 * blk, HEAD_DIM), BF16)
    _side_cast(cast_in, cast_out)

    def lanes(g):
        return slice(g * HEAD_DIM, (g + 1) * HEAD_DIM)

    @pl.when(jt == 0)
    def _():
        kp_ref[...] = jnp.zeros_like(kp_ref)
        vp_ref[...] = jnp.zeros_like(vp_ref)
        kpf_scr[...] = jnp.zeros_like(kpf_scr)
        vpf_scr[...] = jnp.zeros_like(vpf_scr)

    for src, stage, dst_bf16, dst_f32 in ((q_in, stage_q, q_ref, qf_scr),
                                          (k_in, stage_k, k_ref, kf_scr),
                                          (v_in, stage_v, v_ref, vf_scr)):
        for i in range(blk):
            stage[STAGE_PITCH * i:STAGE_PITCH * i + ATTN_GROUPS, :] = (
                src[ATTN_GROUPS * i:ATTN_GROUPS * (i + 1), :].astype(F32))
        for g in range(ATTN_GROUPS):
            cls = stage[pl.ds(g, blk, stride=STAGE_PITCH), :]
            dst_f32[:, lanes(g)] = cls
            dst_bf16[:, lanes(g)] = cls.astype(BF16)

    n_pat = len(ATTN_PATTERNS)

    def pattern_work(pi, window, dil):
        span = window // dil
        n_slab = ATTN_GROUPS // dil
        slab = blk // n_slab
        use_f32 = slab % BF16_SUBLANES != 0
        band, col = _band_mask(slab, n_slab, span)
        band_first = jnp.logical_and(band, jnp.logical_or(col >= blk, jt > 0))

        def gather(cur_ref, cur_f32, prev_ref, prev_f32, res, kb, prev):
            if not prev:
                src, rows = (cur_f32 if use_f32 else cur_ref), slice(slab * kb, slab * (kb + 1))
            elif kb > 0:
                src, rows = (cur_f32 if use_f32 else cur_ref), slice(slab * (kb - 1), slab * kb)
            elif use_f32:
                src, rows = prev_f32, slice(F32_SUBLANES - slab, F32_SUBLANES)
            else:
                src, rows = prev_ref, slice(blk - slab, blk)
            parts = [src[rows, lanes(u * dil + res)] for u in range(n_slab)]
            out = parts[0] if n_slab == 1 else jnp.concatenate(parts, axis=0)
            return out.astype(BF16)

        def st_load(scr, res, kb):
            rows = slice(slab * kb, slab * (kb + 1))
            parts = [scr[rows, lanes(u * dil + res)] for u in range(n_slab)]
            return parts[0] if n_slab == 1 else jnp.concatenate(parts, axis=0)

        def st_store(scr, res, kb, val):
            rows = slice(slab * kb, slab * (kb + 1))
            for u in range(n_slab):
                scr[rows, lanes(u * dil + res)] = val[u * slab:(u + 1) * slab, :]

        def scores_of(res, kb):
            q = gather(q_ref, qf_scr, None, None, res, kb, False)
            kw = jnp.concatenate([gather(k_ref, kf_scr, kp_ref, kpf_scr, res, kb, True),
                                  gather(k_ref, kf_scr, kp_ref, kpf_scr, res, kb, False)], axis=0)
            s = lax.dot_general(q, kw, (((1,), (1,)), ((), ())), preferred_element_type=F32)
            return jnp.where(band_first if kb == 0 else band, s, MASK_VALUE)

        def finish(res, kb, s):
            m = jnp.max(s, axis=-1, keepdims=True)
            p = jnp.exp2(s - m).astype(BF16)
            vw = jnp.concatenate([gather(v_ref, vf_scr, vp_ref, vpf_scr, res, kb, True),
                                  gather(v_ref, vf_scr, vp_ref, vpf_scr, res, kb, False)], axis=0)
            pv = jnp.dot(p, jnp.concatenate([vw, ones_v], axis=1), preferred_element_type=F32)
            num, den = pv[:, :HEAD_DIM], pv[:, HEAD_DIM:]
            m_new = jnp.broadcast_to(m, (blk, HEAD_DIM))
            if pi > 0:
                m_old = st_load(m_scr, res, kb)
                m_new = jnp.maximum(m_old, m_new)
                w_old = jnp.exp2(m_old - m_new)
                w_new = jnp.exp2(m - m_new)
                num = st_load(acc_scr, res, kb) * w_old + num * w_new
                den = st_load(l_scr, res, kb) * w_old + den * w_new
            if pi < n_pat - 1:
                st_store(acc_scr, res, kb, num)
                st_store(m_scr, res, kb, m_new)
                st_store(l_scr, res, kb, den)
            else:
                st_store(acc_scr, res, kb, num / den)

        return [(scores_of, finish, (res, kb)) for res in range(dil) for kb in range(n_slab)]

    work = [w for pi, (window, dil) in enumerate(reversed(ATTN_PATTERNS))
            for w in pattern_work(pi, window, dil)]
    pending = []
    for t in range(len(work) + depth):
        if t < len(work):
            scores_of, _, block = work[t]
            pending.append(scores_of(*block))
        if t >= depth:
            _, finish, block = work[t - depth]
            finish(*block, pending.pop(0))

    for g in range(ATTN_GROUPS):
        stage_q[pl.ds(g, blk, stride=STAGE_PITCH), :] = acc_scr[:, lanes(g)]
    for i in range(blk):
        o_ref[ATTN_GROUPS * i:ATTN_GROUPS * (i + 1), :] = (
            stage_q[STAGE_PITCH * i:STAGE_PITCH * i + ATTN_GROUPS, :].astype(BF16))

    kp_ref[...] = k_ref[...]
    vp_ref[...] = v_ref[...]
    kpf_scr[...] = kf_scr[blk - F32_SUBLANES:, :]
    vpf_scr[...] = vf_scr[blk - F32_SUBLANES:, :]


def _dilated_attention(att, side_weights, col_chunks, batch, seq, depth=3):
    assert seq % ATTN_TILE == 0
    n_tile = seq // ATTN_TILE
    cast_specs, cast_out_specs, cast_shapes = _side_cast_specs(
        side_weights, batch * ATTN_HEADS * n_tile,
        lambda bi, hd, jt: (bi * ATTN_HEADS + hd) * n_tile + jt, col_chunks)

    def spec(which):
        return pl.BlockSpec((None, None, ATTN_TILE, HEAD_DIM),
                            lambda bi, hd, jt: (which, hd, bi * n_tile + jt, 0))

    tile_f32 = pltpu.VMEM((ATTN_BLOCK, TILE_LANES), F32)
    tile_bf16 = pltpu.VMEM((ATTN_BLOCK, TILE_LANES), BF16)
    tail_f32 = pltpu.VMEM((F32_SUBLANES, TILE_LANES), F32)
    stage = pltpu.VMEM((ATTN_BLOCK * STAGE_PITCH, HEAD_DIM), F32)
    return pl.pallas_call(
        functools.partial(_attn_body, depth=depth, n_cast=len(side_weights)),
        grid=(batch, ATTN_HEADS, n_tile),
        in_specs=[spec(0), spec(1), spec(2)] + cast_specs,
        out_specs=[pl.BlockSpec((None, ATTN_TILE, HEAD_DIM),
                                lambda bi, hd, jt: (hd, bi * n_tile + jt, 0))] + cast_out_specs,
        out_shape=[jax.ShapeDtypeStruct((ATTN_HEADS, batch * seq, HEAD_DIM), BF16)] + cast_shapes,
        scratch_shapes=[tile_f32, tile_f32, tile_f32,
                        stage, stage, stage,
                        tile_bf16, tile_bf16, tile_bf16,
                        tile_f32, tile_f32, tile_f32,
                        tile_bf16, tile_bf16,
                        tail_f32, tail_f32],
        compiler_params=pltpu.CompilerParams(
            dimension_semantics=("arbitrary", "arbitrary", "arbitrary"),
            vmem_limit_bytes=VMEM_LIMIT_BYTES),
        name="dilated_attn",
    )(att, att, att, *side_weights)


CONV_HALO = 8


def _mlstm_chunk(gt, qb, kf, v_ref, mo_ref, ng_ref, o_ref, rows, cn_scr, m_scr):
    L = MLSTM_CHUNK

    ig = gt[:MLSTM_HEADS]
    fg = gt[MLSTM_HEADS:]
    lf = jnp.minimum(fg, 0.0) - jnp.log1p(jnp.exp(-jnp.abs(fg)))
    lane = lax.broadcasted_iota(jnp.int32, (MLSTM_HEADS, L), 1)
    b = lf
    s = 1
    while s < L:
        b = b + jnp.where(lane >= s, pltpu.roll(b, s, 1), 0.0)
        s *= 2
    c_row = ig - b
    b_last = b[:, L - 1:L]
    c_max = jnp.max(c_row, axis=1, keepdims=True)
    m_prev = m_scr[:, :1]
    m_chunk = b_last + c_max
    m_new = jnp.maximum(b_last + m_prev, m_chunk)
    decay = jnp.exp(b_last + m_prev - m_new)
    scale = jnp.exp(m_chunk - m_new)
    wa = jnp.exp(c_row - c_max)
    m_scr[...] = jnp.broadcast_to(m_new, m_scr.shape)

    row = lax.broadcasted_iota(jnp.int32, (L, L), 0)
    col = lax.broadcasted_iota(jnp.int32, (L, L), 1)
    causal = col <= row
    ones_v = jnp.ones((L, HEAD_DIM), BF16)

    heads = range(MLSTM_HEADS)

    def hs(hd):
        return slice(hd * HEAD_DIM, (hd + 1) * HEAD_DIM)

    def one(hd):
        return slice(hd, hd + 1)

    k_ts, states, s_qks, inters = [], [], [], []
    for hd in heads:
        q_h = qb[:, hs(hd)]
        k_t = kf[:, hs(hd)].T
        state = cn_scr[hd]
        k_ts.append(k_t)
        states.append(state)
        s_qks.append(jnp.dot(q_h, k_t.astype(BF16), preferred_element_type=F32))
        inters.append(jnp.dot(q_h, state.astype(BF16), preferred_element_type=F32))

    ws, mus, gs = [], [], []
    for hd in heads:
        c_b = jnp.where(causal, jnp.broadcast_to(c_row[one(hd), :], (L, L)), MASK_VALUE)
        mu = jnp.maximum(jnp.max(c_b, axis=-1, keepdims=True), m_prev[one(hd), :])
        ws.append(jnp.exp(c_b - mu) * s_qks[hd])
        mus.append(mu)
        gs.append(jnp.exp(m_prev[one(hd), :] - mu))

    ones_sq = jnp.ones((L, L), BF16)

    def split(x):
        hi = x.astype(BF16)
        return hi, (x - hi.astype(F32)).astype(BF16)

    lf_rows = jnp.concatenate([jnp.broadcast_to(lf[one(hd), :], (HEAD_DIM, L)) for hd in heads], axis=0)
    causal_b = jnp.where(causal, 1.0, 0.0).astype(BF16)
    nt = (((1,), (1,)), ((), ()))
    lf_hi, lf_lo = split(lf_rows)
    b_all = (lax.dot_general(causal_b, lf_hi, nt, preferred_element_type=F32)
             + lax.dot_general(causal_b, lf_lo, nt, preferred_element_type=F32))

    intras, kvns, sum_los = [], [], []
    for hd in heads:
        v_aug = jnp.concatenate([v_ref[rows, hs(hd)], ones_v], axis=1)
        w_hi, w_lo = split(ws[hd])
        intras.append(jnp.dot(w_hi, v_aug, preferred_element_type=F32))
        sum_los.append(jnp.dot(w_lo, ones_sq, preferred_element_type=F32))
        a_t = (k_ts[hd] * jnp.broadcast_to(wa[one(hd), :], (HEAD_DIM, L))).astype(BF16)
        kvns.append(jnp.dot(a_t, v_aug, preferred_element_type=F32))

    cells = []
    for hd in heads:
        g, mu = gs[hd], mus[hd]
        num = g * inters[hd][:, :HEAD_DIM] + intras[hd][:, :HEAD_DIM]
        den = g * inters[hd][:, HEAD_DIM:] + intras[hd][:, HEAD_DIM:] + sum_los[hd]
        cells.append(num / jnp.maximum(jnp.abs(den), jnp.exp(-(b_all[:, hs(hd)] + mu))))

    sq_sums = []
    for hd in heads:
        sq = (cells[hd] * cells[hd]).astype(BF16)
        sq_sums.append(jnp.dot(sq, ones_sq, preferred_element_type=F32))
    for hd in heads:
        y = cells[hd] * lax.rsqrt(sq_sums[hd] * (1.0 / HEAD_DIM) + EPS) * ng_ref[:, hs(hd)]
        o_ref[rows, hs(hd)] = (jax.nn.sigmoid(mo_ref[rows, hs(hd)]) * y).astype(BF16)
        cn_scr[hd] = decay[one(hd), :] * states[hd] + scale[one(hd), :] * kvns[hd]


MIX_TILE = 4 * MLSTM_CHUNK


def _mix_out_body(gt_ref, q_ref, k_ref, qp_ref, kp_ref, v_ref, mo_ref, cw_ref, cb_ref, ng_ref,
                  a_ref, x_ref, wo_ref, gf_ref, x1_ref, h2_ref,
                  cn_scr, m_scr, qext_scr, kext_scr, ml_cur, ml_prev, *, tiles_per_seq, n_tiles):
    s = pl.program_id(0)
    L = MLSTM_CHUNK
    W = MLSTM_WIDTH
    n_chunk = MIX_TILE // L
    tile = jnp.minimum(s, n_tiles - 1)
    seq_start = (tile % tiles_per_seq) == 0

    @pl.when(s == 0)
    def _():
        ml_prev[...] = jnp.zeros_like(ml_prev)

    @pl.when(seq_start)
    def _():
        cn_scr[...] = jnp.zeros_like(cn_scr)
        m_scr[...] = jnp.full_like(m_scr, M_INIT)

    has_prev = jnp.logical_not(seq_start)

    def conv_silu(cur_ref, prev_ref, ext, col0):
        ext[:CONV_HALO, :] = jnp.where(has_prev, prev_ref[...], 0.0)
        ext[CONV_HALO:, :] = cur_ref[...]
        y = cb_ref[:, col0:col0 + W]
        for sh in range(CONV_WIDTH):
            tap = cw_ref[CONV_WIDTH - 1 - sh:CONV_WIDTH - sh, col0:col0 + W]
            y = y + ext[CONV_HALO - sh:CONV_HALO - sh + MIX_TILE, :] * tap
        return y * jax.nn.sigmoid(y)

    qb = conv_silu(q_ref, qp_ref, qext_scr, 0).astype(BF16)
    kf = conv_silu(k_ref, kp_ref, kext_scr, W) * (HEAD_DIM ** -0.5)

    mix_prev = jnp.concatenate([a_ref[hd] for hd in range(ATTN_HEADS)] + [ml_prev[...]], axis=1)
    n_slab = D_MODEL // n_chunk
    for ck in range(n_chunk):
        cols = slice(ck * n_slab, (ck + 1) * n_slab)
        x1_ref[:, cols] = (jnp.dot(mix_prev, wo_ref[:, cols], preferred_element_type=F32)
                           + x_ref[:, cols])
        rows = slice(ck * L, (ck + 1) * L)
        _mlstm_chunk(gt_ref[:, rows], qb[rows, :], kf[rows, :], v_ref, mo_ref, ng_ref, ml_cur, rows,
                     cn_scr, m_scr)

    y = x1_ref[...]
    ms = jnp.mean(y * y, axis=-1, keepdims=True)
    h2_ref[...] = (y * lax.rsqrt(ms + EPS) * gf_ref[...]).astype(BF16)
    ml_prev[...] = ml_cur[...]


def _mix_out(gates_t, m_f32, m_v, conv_w, conv_b, norm_g, attn, x2, w_o, g_ffn, batch, seq):
    t = batch * seq
    W = MLSTM_WIDTH
    assert seq % MIX_TILE == 0
    tiles_per_seq = seq // MIX_TILE
    n_tiles = t // MIX_TILE
    halo_per_tile = MIX_TILE // CONV_HALO

    def cur(s):
        return jnp.minimum(s, n_tiles - 1)

    def prev(s):
        return jnp.maximum(s - 1, 0)

    def halo(s):
        return jnp.maximum(cur(s) * halo_per_tile - 1, 0)

    const = lambda s: (0, 0)
    return pl.pallas_call(
        functools.partial(_mix_out_body, tiles_per_seq=tiles_per_seq, n_tiles=n_tiles),
        grid=(n_tiles + 1,),
        in_specs=[
            pl.BlockSpec((N_GATE, MIX_TILE), lambda s: (0, cur(s))),
            pl.BlockSpec((MIX_TILE, W), lambda s: (cur(s), 0)),
            pl.BlockSpec((MIX_TILE, W), lambda s: (cur(s), 1)),
            pl.BlockSpec((CONV_HALO, W), lambda s: (halo(s), 0)),
            pl.BlockSpec((CONV_HALO, W), lambda s: (halo(s), 1)),
            pl.BlockSpec((MIX_TILE, W), lambda s: (cur(s), 0)),
            pl.BlockSpec((MIX_TILE, W), lambda s: (cur(s), 2)),
            pl.BlockSpec((CONV_WIDTH, 2 * W), const),
            pl.BlockSpec((1, 2 * W), const),
            pl.BlockSpec((1, W), const),
            pl.BlockSpec((ATTN_HEADS, MIX_TILE, HEAD_DIM), lambda s: (0, prev(s), 0)),
            pl.BlockSpec((MIX_TILE, D_MODEL), lambda s: (prev(s), 0)),
            pl.BlockSpec((ATTN_WIDTH + MLSTM_WIDTH, D_MODEL), const, pipeline_mode=pl.Buffered(1)),
            pl.BlockSpec((1, D_MODEL), const),
        ],
        out_specs=[
            pl.BlockSpec((MIX_TILE, D_MODEL), lambda s: (prev(s), 0)),
            pl.BlockSpec((MIX_TILE, D_MODEL), lambda s: (prev(s), 0)),
        ],
        out_shape=[
            jax.ShapeDtypeStruct((t, D_MODEL), F32),
            jax.ShapeDtypeStruct((t, D_MODEL), BF16),
        ],
        scratch_shapes=[pltpu.VMEM((MLSTM_HEADS, HEAD_DIM, 2 * HEAD_DIM), F32),
                        pltpu.VMEM((MLSTM_HEADS, LANES), F32),
                        pltpu.VMEM((CONV_HALO + MIX_TILE, W), F32),
                        pltpu.VMEM((CONV_HALO + MIX_TILE, W), F32),
                        pltpu.VMEM((MIX_TILE, W), BF16),
                        pltpu.VMEM((MIX_TILE, W), BF16)],
        compiler_params=pltpu.CompilerParams(
            dimension_semantics=("arbitrary",),
            vmem_limit_bytes=VMEM_LIMIT_BYTES),
        name="mlstm_out_proj",
    )(gates_t, m_f32, m_f32, m_f32, m_f32, m_v, m_f32, conv_w, conv_b, norm_g,
      attn, x2, w_o, g_ffn)


def kernel(x, norm_mix_g, w_in, conv_w, conv_b, gate_b, q_norm_g, k_norm_g, mlstm_norm_g,
           w_out, norm_ffn_g, w_gate, w_up, w_down):
    B, S, _ = x.shape
    t = B * S
    layer = 0
    x2 = x.reshape(t, D_MODEL)

    w_t = w_in[layer].T
    a3 = 3 * ATTN_WIDTH
    mw = MLSTM_WIDTH
    n_main = a3 + 4 * mw
    w_main = _cast_rows_bf16(w_t, n_main, 512)
    w_gate_t = w_t[n_main:]
    b_gate = gate_b[layer].reshape(N_GATE, 1)
    head_gain = jnp.stack([
        jnp.tile(q_norm_g[layer] * (HEAD_DIM ** -0.5 * LOG2_E), ATTN_HEADS),
        jnp.tile(k_norm_g[layer], ATTN_HEADS)]).reshape(2, 1, ATTN_WIDTH)

    att, m_f32, m_v, gates_t = _in_proj(
        x2, norm_mix_g[layer].reshape(1, D_MODEL), w_main, w_gate_t, b_gate, head_gain, (), tm=1024)

    ffn_chunk = 512
    attn_out, wo_b, wg_b, wu_b, wd_b = _dilated_attention(
        att, (w_out[layer], w_gate[layer], w_up[layer], w_down[layer]),
        (None, ffn_chunk, ffn_chunk, None), B, S)

    x1, h2 = _mix_out(gates_t, m_f32, m_v, conv_w[layer], conv_b[layer].reshape(1, 2 * mw),
                      mlstm_norm_g[layer].reshape(1, mw), attn_out, x2, wo_b,
                      norm_ffn_g[layer].reshape(1, D_MODEL), B, S)

    out = _ffn(h2, wg_b, wu_b, wd_b, x1, tm=1024, th=ffn_chunk)
    return out.reshape(B, S, D_MODEL)
```
